```python
import jax, jax.numpy as jnp
from jax import lax
import numpy as np

D_MODEL = 4096
BATCH = 2
SEQ = 4096
DEPTH = 1

CHUNK = 64
Q_BLOCK = 128
D_MIX = D_MODEL
D_CONV = D_MIX // 2
CONV_WIDTH = 3
N_HEADS = 16
V_HEAD_DIM = 128
D_ATTN = N_HEADS * V_HEAD_DIM
QK_NOPE_DIM = 128
ROPE_DIM = 64
QK_HEAD_DIM = QK_NOPE_DIM + ROPE_DIM
Q_LORA = 1024
KV_LORA = 512
ROPE_THETA = 10000.0
NORM_EPS = 1e-6
ATTN_SCALE = QK_HEAD_DIM ** -0.5

PROJ_SIZES = (D_CONV, D_CONV, D_CONV, D_CONV, Q_LORA, KV_LORA, ROPE_DIM, D_ATTN)
D_IN_PROJ = sum(PROJ_SIZES)
PROJ_SPLITS = tuple(int(v) for v in np.cumsum(PROJ_SIZES)[:-1])

kernel_name = "hymba_shortconv_mla_hybrid_block"


def rmsnorm(x, g):
    xf = x.astype(jnp.float32)
    y = xf * lax.rsqrt(jnp.mean(xf * xf, axis=-1, keepdims=True) + NORM_EPS)
    return (y * g.astype(jnp.float32)).astype(x.dtype)


def rope_tables(seq):
    pos = jnp.arange(seq, dtype=jnp.float32)
    inv_freq = 1.0 / (ROPE_THETA ** (jnp.arange(0, ROPE_DIM, 2, dtype=jnp.float32) / ROPE_DIM))
    ang = pos[:, None] * inv_freq[None, :]
    return jnp.cos(ang), jnp.sin(ang)


def apply_rope(x, cos, sin):
    xf = x.astype(jnp.float32)
    x1, x2 = jnp.split(xf, 2, axis=-1)
    c = cos[None, :, None, :]
    s = sin[None, :, None, :]
    return jnp.concatenate([x1 * c - x2 * s, x2 * c + x1 * s], axis=-1).astype(x.dtype)


def causal_short_conv(u, w):
    s = u.shape[1]
    up = jnp.pad(u, ((0, 0), (CONV_WIDTH - 1, 0), (0, 0)))
    return sum(w[j] * up[:, j:j + s] for j in range(CONV_WIDTH))


def chunk_causal_attention(q, k, v):
    b, s, h, dq = q.shape
    nqb = s // Q_BLOCK
    qb = q.reshape(b, nqb, Q_BLOCK, h, dq).transpose(1, 0, 2, 3, 4)
    key_chunk = jnp.arange(s) // CHUNK

    def one_block(args):
        i, qi = args
        q_chunk = (i * Q_BLOCK + jnp.arange(Q_BLOCK)) // CHUNK
        mask = key_chunk[None, :] <= q_chunk[:, None]
        sc = jnp.einsum('bqhd,bkhd->bhqk', qi, k, preferred_element_type=jnp.float32) * ATTN_SCALE
        sc = jnp.where(mask[None, None], sc, -jnp.inf)
        p = jax.nn.softmax(sc, axis=-1).astype(v.dtype)
        return jnp.einsum('bhqk,bkhd->bqhd', p, v)

    out = lax.map(one_block, (jnp.arange(nqb), qb))
    return out.transpose(1, 0, 2, 3, 4).reshape(b, s, h, v.shape[-1])


def hybrid_layer(x, g_in, w_in, conv_w, q_norm_g, w_uq, kv_norm_g, w_ukv, w_out, cos, sin):
    b, s, _ = x.shape
    xn = rmsnorm(x, g_in)
    proj = jnp.einsum('bsd,de->bse', xn, w_in)
    gB, gC, h, z_conv, c_q, c_kv, k_rope, z_attn = jnp.split(proj, PROJ_SPLITS, axis=-1)

    y_conv = gB * causal_short_conv(gC * h, conv_w)
    y_conv = y_conv * jax.nn.silu(z_conv)

    q = jnp.einsum('bsr,re->bse', rmsnorm(c_q, q_norm_g), w_uq).reshape(b, s, N_HEADS, QK_HEAD_DIM)
    q_nope, q_pe = q[..., :QK_NOPE_DIM], q[..., QK_NOPE_DIM:]
    kv = jnp.einsum('bsr,re->bse', rmsnorm(c_kv, kv_norm_g), w_ukv).reshape(b, s, N_HEADS, QK_NOPE_DIM + V_HEAD_DIM)
    k_nope, v = kv[..., :QK_NOPE_DIM], kv[..., QK_NOPE_DIM:]
    q_pe = apply_rope(q_pe, cos, sin)
    k_pe = apply_rope(k_rope[:, :, None, :], cos, sin)
    q_full = jnp.concatenate([q_nope, q_pe], axis=-1)
    k_full = jnp.concatenate([k_nope, jnp.broadcast_to(k_pe, (b, s, N_HEADS, ROPE_DIM))], axis=-1)
    attn = chunk_causal_attention(q_full, k_full, v).reshape(b, s, D_ATTN)
    y_attn = attn * jax.nn.silu(z_attn)

    y = jnp.concatenate([y_conv, y_attn], axis=-1)
    return x + jnp.einsum('bse,ed->bsd', y, w_out)


def setup_inputs(seed: int = 0) -> dict:
    key = jax.random.key(seed)
    ks = jax.random.split(key, 12)
    nrm = jax.random.normal
    return {
        "x": nrm(ks[0], (BATCH, SEQ, D_MODEL), jnp.float32),
        "g_in": 1.0 + 0.02 * nrm(ks[1], (DEPTH, D_MODEL), jnp.float32),
        "w_in": nrm(ks[2], (DEPTH, D_MODEL, D_IN_PROJ), jnp.float32) * D_MODEL ** -0.5,
        "conv_w": nrm(ks[3], (DEPTH, CONV_WIDTH, D_CONV), jnp.float32) * CONV_WIDTH ** -0.5,
        "q_norm_g": 1.0 + 0.02 * nrm(ks[4], (DEPTH, Q_LORA), jnp.float32),
        "w_uq": nrm(ks[5], (DEPTH, Q_LORA, N_HEADS * QK_HEAD_DIM), jnp.float32) * Q_LORA ** -0.5,
        "kv_norm_g": 1.0 + 0.02 * nrm(ks[6], (DEPTH, KV_LORA), jnp.float32),
        "w_ukv": nrm(ks[7], (DEPTH, KV_LORA, N_HEADS * (QK_NOPE_DIM + V_HEAD_DIM)), jnp.float32) * KV_LORA ** -0.5,
        "w_out": nrm(ks[8], (DEPTH, D_MIX, D_MODEL), jnp.float32) * D_MIX ** -0.5,
        "g_final": 1.0 + 0.02 * nrm(ks[9], (D_MODEL,), jnp.float32),
    }


def reference(x, g_in, w_in, conv_w, q_norm_g, w_uq, kv_norm_g, w_ukv, w_out, g_final):
    cos, sin = rope_tables(x.shape[1])
    h = x
    for l in range(DEPTH):
        h = hybrid_layer(h, g_in[l], w_in[l], conv_w[l], q_norm_g[l], w_uq[l],
                         kv_norm_g[l], w_ukv[l], w_out[l], cos, sin)
    return rmsnorm(h, g_final)
```

```python
import functools

import jax
import jax.numpy as jnp
import numpy as np
from jax import lax
from jax.experimental import pallas as pl
from jax.experimental.pallas import tpu as pltpu

CHUNK = 64
CONV_WIDTH = 3
N_HEADS = 16
V_HEAD_DIM = 128
QK_NOPE_DIM = 128
ROPE_DIM = 64
ROPE_HALF = ROPE_DIM // 2
Q_LORA = 1024
KV_LORA = 512
ROPE_THETA = 10000.0
NORM_EPS = 1e-6
ATTN_SCALE = (QK_NOPE_DIM + ROPE_DIM) ** -0.5

LANES = 128
SUBLANES = 8
QK_PAD_DIM = 2 * LANES
VMEM_LIMIT_BYTES = 56 * 1024 * 1024

BF16 = jnp.bfloat16
F32 = jnp.float32


def _params(*semantics):
    return pltpu.CompilerParams(dimension_semantics=semantics,
                                vmem_limit_bytes=VMEM_LIMIT_BYTES)


def _dot(a, b):
    return jnp.dot(a, b, preferred_element_type=F32)


def _dot_nt(a, b):
    return lax.dot_general(a, b, (((1,), (1,)), ((), ())), preferred_element_type=F32)


def _rmsnorm_rows(x, g):
    ms = jnp.mean(x * x, axis=-1, keepdims=True)
    return (x * lax.rsqrt(ms + NORM_EPS)) * g


def _silu(z):
    return z * jax.nn.sigmoid(z)


def _rmsnorm_kernel(x_ref, g_ref, o_ref):
    o_ref[...] = _rmsnorm_rows(x_ref[...], g_ref[...]).astype(o_ref.dtype)


def _rmsnorm(x, g, out_dtype, tm):
    m, d = x.shape
    return pl.pallas_call(
        _rmsnorm_kernel,
        grid=(m // tm,),
        in_specs=[pl.BlockSpec((tm, d), lambda i: (i, 0)),
                  pl.BlockSpec((1, d), lambda i: (0, 0))],
        out_specs=pl.BlockSpec((tm, d), lambda i: (i, 0)),
        out_shape=jax.ShapeDtypeStruct((m, d), out_dtype),
        compiler_params=_params("parallel"),
        name="rmsnorm",
    )(x, g.reshape(1, d))


def _conv_group_kernel(xn_ref, wb_ref, wc_ref, wh_ref, wz_ref, cw_ref, y_ref, carry_ref,
                       *, tiles_per_seq):
    i = pl.program_id(0)
    j = pl.program_id(1)
    xn = xn_ref[...]
    u = _dot(xn, wc_ref[...]) * _dot(xn, wh_ref[...])
    tm, tc = u.shape

    @pl.when(i % tiles_per_seq == 0)
    def _():
        carry_ref[j] = jnp.zeros((SUBLANES, tc), F32)

    prev = carry_ref[j]
    carry_ref[j] = u[tm - SUBLANES:, :]

    row = lax.broadcasted_iota(jnp.int32, (SUBLANES, tc), 0)

    def shifted(k):
        body = pltpu.roll(u, k, 0)
        head = jnp.where(row < k, pltpu.roll(prev, k, 0), body[:SUBLANES, :])
        return jnp.concatenate([head, body[SUBLANES:, :]], axis=0)

    cw = cw_ref[...]
    conv = cw[0:1, :] * shifted(2) + cw[1:2, :] * shifted(1) + cw[2:3, :] * u
    y = (_dot(xn, wb_ref[...]) * conv) * _silu(_dot(xn, wz_ref[...]))
    y_ref[...] = y.astype(y_ref.dtype)


def _conv_group(xn, w_conv, conv_w, seq, tm, tc):
    m, d = xn.shape
    dc = conv_w.shape[1]
    nj = dc // tc

    def w_spec(k):
        return pl.BlockSpec((d, tc), lambda i, j: (0, j + k * nj))

    return pl.pallas_call(
        functools.partial(_conv_group_kernel, tiles_per_seq=seq // tm),
        grid=(m // tm, nj),
        in_specs=[pl.BlockSpec((tm, d), lambda i, j: (i, 0)),
                  w_spec(0), w_spec(1), w_spec(2), w_spec(3),
                  pl.BlockSpec((CONV_WIDTH, tc), lambda i, j: (0, j))],
        out_specs=pl.BlockSpec((tm, tc), lambda i, j: (i, j)),
        out_shape=jax.ShapeDtypeStruct((m, dc), BF16),
        scratch_shapes=[pltpu.VMEM((nj, SUBLANES, tc), F32)],
        compiler_params=_params("arbitrary", "arbitrary"),
        name="conv_group",
    )(xn, w_conv, w_conv, w_conv, w_conv, conv_w)


def _rope_pair(t, cos, sin):
    return t * cos + pltpu.roll(t, ROPE_DIM, 1) * sin


def _latent_kernel(xn_ref, w_ref, gq_ref, gkv_ref, cos_ref, sin_ref,
                   cq_ref, ckv_ref, kpe_ref):
    c = _dot(xn_ref[...], w_ref[...])
    cq_ref[...] = _rmsnorm_rows(c[:, :Q_LORA], gq_ref[...]).astype(cq_ref.dtype)
    ckv_ref[...] = _rmsnorm_rows(c[:, Q_LORA:Q_LORA + KV_LORA], gkv_ref[...]).astype(ckv_ref.dtype)
    kr = c[:, Q_LORA + KV_LORA:]
    kpe_ref[...] = _rope_pair(kr, cos_ref[...], sin_ref[...]).astype(kpe_ref.dtype)


def _latents(xn, w_lat, gq, gkv, cos, sin, seq, tm):
    m, d = xn.shape
    n = w_lat.shape[1]
    tps = seq // tm
    return pl.pallas_call(
        _latent_kernel,
        grid=(m // tm,),
        in_specs=[pl.BlockSpec((tm, d), lambda i: (i, 0)),
                  pl.BlockSpec((d, n), lambda i: (0, 0)),
                  pl.BlockSpec((1, Q_LORA), lambda i: (0, 0)),
                  pl.BlockSpec((1, KV_LORA), lambda i: (0, 0)),
                  pl.BlockSpec((tm, LANES), lambda i: (i % tps, 0)),
                  pl.BlockSpec((tm, LANES), lambda i: (i % tps, 0))],
        out_specs=[pl.BlockSpec((tm, Q_LORA), lambda i: (i, 0)),
                   pl.BlockSpec((tm, KV_LORA), lambda i: (i, 0)),
                   pl.BlockSpec((tm, LANES), lambda i: (i, 0))],
        out_shape=[jax.ShapeDtypeStruct((m, Q_LORA), BF16),
                   jax.ShapeDtypeStruct((m, KV_LORA), BF16),
                   jax.ShapeDtypeStruct((m, LANES), BF16)],
        compiler_params=_params("parallel"),
        name="latents",
    )(xn, w_lat, gq.reshape(1, -1), gkv.reshape(1, -1), cos, sin)


def _gate_kernel(xn_ref, w_ref, o_ref):
    o_ref[...] = _silu(_dot(xn_ref[...], w_ref[...])).astype(o_ref.dtype)


def _gate(xn, w_z, tm, tn):
    m, d = xn.shape
    n = w_z.shape[1]
    return pl.pallas_call(
        _gate_kernel,
        grid=(m // tm, n // tn),
        in_specs=[pl.BlockSpec((tm, d), lambda i, j: (i, 0)),
                  pl.BlockSpec((d, tn), lambda i, j: (0, j))],
        out_specs=pl.BlockSpec((tm, tn), lambda i, j: (i, j)),
        out_shape=jax.ShapeDtypeStruct((m, n), BF16),
        compiler_params=_params("parallel", "parallel"),
        name="attn_gate",
    )(xn, w_z)


def _q_up_kernel(cq_ref, w_ref, cos_ref, sin_ref, q_ref):
    r = _dot(cq_ref[...], w_ref[...])
    cos = cos_ref[...]
    sin = sin_ref[...]
    for h in range(N_HEADS):
        lo = h * QK_PAD_DIM
        nope = r[:, lo:lo + LANES] * ATTN_SCALE
        pe = _rope_pair(r[:, lo + LANES:lo + QK_PAD_DIM], cos, sin) * ATTN_SCALE
        q_ref[:, lo:lo + LANES] = nope.astype(q_ref.dtype)
        q_ref[:, lo + LANES:lo + QK_PAD_DIM] = pe.astype(q_ref.dtype)


def _q_up(cq, w_q, cos, sin, seq, tm):
    m, k = cq.shape
    n = w_q.shape[1]
    tps = seq // tm
    return pl.pallas_call(
        _q_up_kernel,
        grid=(m // tm,),
        in_specs=[pl.BlockSpec((tm, k), lambda i: (i, 0)),
                  pl.BlockSpec((k, n), lambda i: (0, 0)),
                  pl.BlockSpec((tm, LANES), lambda i: (i % tps, 0)),
                  pl.BlockSpec((tm, LANES), lambda i: (i % tps, 0))],
        out_specs=pl.BlockSpec((tm, n), lambda i: (i, 0)),
        out_shape=jax.ShapeDtypeStruct((m, n), BF16),
        compiler_params=_params("parallel"),
        name="q_up",
    )(cq, w_q, cos, sin)


def _kv_up_kernel(ckv_ref, wk_ref, wvt_ref, k_ref, vt_ref):
    c = ckv_ref[...]
    k_ref[...] = _dot(c, wk_ref[...]).astype(k_ref.dtype)
    vt_ref[...] = _dot_nt(wvt_ref[...], c).astype(vt_ref.dtype)


def _kv_up(ckv, w_k, w_vt, tk):
    m, k = ckv.shape
    n = w_k.shape[1]
    return pl.pallas_call(
        _kv_up_kernel,
        grid=(m // tk,),
        in_specs=[pl.BlockSpec((tk, k), lambda i: (i, 0)),
                  pl.BlockSpec((k, n), lambda i: (0, 0)),
                  pl.BlockSpec((n, k), lambda i: (0, 0))],
        out_specs=[pl.BlockSpec((tk, n), lambda i: (i, 0)),
                   pl.BlockSpec((None, n, tk), lambda i: (i, 0, 0))],
        out_shape=[jax.ShapeDtypeStruct((m, n), BF16),
                   jax.ShapeDtypeStruct((m // tk, n, tk), BF16)],
        compiler_params=_params("parallel"),
        name="kv_up",
    )(ckv, w_k, w_vt)


def _attention_kernel(q_ref, kn_ref, kpe_ref, vt_ref, gate_ref, o_ref, kfull_ref, *, blk):
    seq = q_ref.shape[0]
    nblk = seq // blk
    kfull_ref[:, :LANES] = kn_ref[...]
    kfull_ref[:, LANES:] = kpe_ref[...]

    key_chunk = lax.broadcasted_iota(jnp.int32, (blk, blk), 0) // CHUNK
    qry_chunk = lax.broadcasted_iota(jnp.int32, (blk, blk), 1) // CHUNK
    visible = key_chunk <= qry_chunk

    for qi in range(nblk):
        qb = q_ref[qi * blk:(qi + 1) * blk, :]

        def kv_step(kj, carry, masked):
            m_run, l_run, acc = carry
            kb = kfull_ref[pl.ds(pl.multiple_of(kj * blk, blk), blk), :]
            s = _dot_nt(kb, qb)
            if masked:
                s = jnp.where(visible, s, -jnp.inf)
            m_new = jnp.maximum(m_run, jnp.max(s, axis=0, keepdims=True))
            alpha = jnp.exp(m_run - m_new)
            p = jnp.exp(s - m_new)
            l_new = alpha * l_run + jnp.sum(p, axis=0, keepdims=True)
            acc_new = alpha * acc + _dot(vt_ref[kj], p.astype(BF16))
            return m_new, l_new, acc_new

        init = (jnp.full((1, blk), -jnp.inf, F32),
                jnp.zeros((1, blk), F32),
                jnp.zeros((V_HEAD_DIM, blk), F32))
        carry = lax.fori_loop(0, qi, functools.partial(kv_step, masked=False), init)
        _, l_fin, acc = kv_step(qi, carry, masked=True)
        out = (acc / l_fin).T
        gate = gate_ref[qi * blk:(qi + 1) * blk, :].astype(F32)
        o_ref[qi * blk:(qi + 1) * blk, :] = (out * gate).astype(o_ref.dtype)


def _attention(q, k_nope, k_pe, v_t, gate, batch, seq, blk):
    m = q.shape[0]
    nblk = seq // blk
    return pl.pallas_call(
        functools.partial(_attention_kernel, blk=blk),
        grid=(batch, N_HEADS),
        in_specs=[pl.BlockSpec((seq, QK_PAD_DIM), lambda b, h: (b, h)),
                  pl.BlockSpec((seq, LANES), lambda b, h: (b, h)),
                  pl.BlockSpec((seq, LANES), lambda b, h: (b, 0)),
                  pl.BlockSpec((nblk, V_HEAD_DIM, blk), lambda b, h: (b, h, 0)),
                  pl.BlockSpec((seq, LANES), lambda b, h: (b, h))],
        out_specs=pl.BlockSpec((seq, V_HEAD_DIM), lambda b, h: (b, h)),
        out_shape=jax.ShapeDtypeStruct((m, N_HEADS * V_HEAD_DIM), BF16),
        scratch_shapes=[pltpu.VMEM((seq, QK_PAD_DIM), BF16)],
        compiler_params=_params("parallel", "parallel"),
        name="attention",
    )(q, k_nope, k_pe, v_t, gate)


def _out_proj_kernel(yc_ref, ya_ref, wc_ref, wa_ref, x_ref, g_ref, o_ref, acc_ref,
                     *, final_norm):
    j = pl.program_id(1)
    nj = pl.num_programs(1)
    tn = x_ref.shape[1]
    h = x_ref[...] + (_dot(yc_ref[...], wc_ref[...]) + _dot(ya_ref[...], wa_ref[...]))
    acc_ref[j] = h

    @pl.when(j == nj - 1)
    def _():
        n_tiles = acc_ref.shape[0]
        if final_norm:
            ss = jnp.sum(acc_ref[0] * acc_ref[0], axis=-1, keepdims=True)
            for t in range(1, n_tiles):
                ss = ss + jnp.sum(acc_ref[t] * acc_ref[t], axis=-1, keepdims=True)
            inv = lax.rsqrt(ss / (n_tiles * tn) + NORM_EPS)
        for t in range(n_tiles):
            ht = acc_ref[t]
            if final_norm:
                ht = (ht * inv) * g_ref[:, t * tn:(t + 1) * tn]
            o_ref[:, t * tn:(t + 1) * tn] = ht.astype(o_ref.dtype)


def _out_proj(y_conv, y_attn, w_o, x, g, final_norm, tm, tn):
    m, d = x.shape
    kc = y_conv.shape[1]
    ka = y_attn.shape[1]
    assert kc == ka, "w_out is split into two equal row blocks"
    return pl.pallas_call(
        functools.partial(_out_proj_kernel, final_norm=final_norm),
        grid=(m // tm, d // tn),
        in_specs=[pl.BlockSpec((tm, kc), lambda i, j: (i, 0)),
                  pl.BlockSpec((tm, ka), lambda i, j: (i, 0)),
                  pl.BlockSpec((kc, tn), lambda i, j: (0, j)),
                  pl.BlockSpec((ka, tn), lambda i, j: (1, j)),
                  pl.BlockSpec((tm, tn), lambda i, j: (i, j)),
                  pl.BlockSpec((1, d), lambda i, j: (0, 0))],
        out_specs=pl.BlockSpec((tm, d), lambda i, j: (i, 0)),
        out_shape=jax.ShapeDtypeStruct((m, d), x.dtype),
        scratch_shapes=[pltpu.VMEM((d // tn, tm, tn), F32)],
        compiler_params=_params("parallel", "arbitrary"),
        name="out_proj",
    )(y_conv, y_attn, w_o, w_o, x, g.reshape(1, d))


def _swap_halves_signed(w):
    return jnp.concatenate([-w[..., ROPE_HALF:], w[..., :ROPE_HALF]], axis=-1)


def _rope_tables(seq):
    pos = jnp.arange(seq, dtype=F32)
    inv_freq = 1.0 / (ROPE_THETA ** (jnp.arange(0, ROPE_DIM, 2, dtype=F32) / ROPE_DIM))
    ang = pos[:, None] * inv_freq[None, :]
    zeros = jnp.zeros((seq, LANES - ROPE_DIM), F32)
    cos = jnp.concatenate([jnp.cos(ang), jnp.cos(ang), zeros], axis=-1)
    sin = jnp.concatenate([jnp.sin(ang), jnp.sin(ang), zeros], axis=-1)
    return cos, sin


def _tile(n, pref):
    return pref if n % pref == 0 else n


def _layer(h, g_in, w_in, conv_w, q_norm_g, w_uq, kv_norm_g, w_ukv, w_out, g_out,
           final_norm, cos, sin, batch, seq):
    m, d = h.shape
    dc = conv_w.shape[1]
    o = 4 * dc

    w_conv = w_in[:, :o].astype(BF16)
    w_kr = w_in[:, o + Q_LORA + KV_LORA:o + Q_LORA + KV_LORA + ROPE_DIM]
    w_lat = jnp.concatenate([w_in[:, o:o + Q_LORA + KV_LORA], w_kr, _swap_halves_signed(w_kr)],
                            axis=1).astype(BF16)
    w_z = w_in[:, o + Q_LORA + KV_LORA + ROPE_DIM:].astype(BF16)
    wq = w_uq.reshape(Q_LORA, N_HEADS, QK_NOPE_DIM + ROPE_DIM)
    wq_pe = wq[..., QK_NOPE_DIM:]
    w_q = jnp.concatenate([wq[..., :QK_NOPE_DIM], wq_pe, _swap_halves_signed(wq_pe)],
                          axis=-1).reshape(Q_LORA, N_HEADS * QK_PAD_DIM).astype(BF16)
    wkv = w_ukv.reshape(KV_LORA, N_HEADS, QK_NOPE_DIM + V_HEAD_DIM)
    w_k = wkv[..., :QK_NOPE_DIM].reshape(KV_LORA, N_HEADS * QK_NOPE_DIM).astype(BF16)
    w_vt = wkv[..., QK_NOPE_DIM:].reshape(KV_LORA, N_HEADS * V_HEAD_DIM).T.astype(BF16)
    w_o = w_out.astype(BF16)

    blk = _tile(seq, 512)
    xn = _rmsnorm(h, g_in, BF16, _tile(m, 256))
    y_conv = _conv_group(xn, w_conv, conv_w, seq, _tile(seq, 1024), _tile(dc, 256))
    cq, ckv, k_pe = _latents(xn, w_lat, q_norm_g, kv_norm_g, cos, sin, seq, _tile(seq, 512))
    gate = _gate(xn, w_z, _tile(m, 1024), _tile(w_z.shape[1], 1024))
    q = _q_up(cq, w_q, cos, sin, seq, _tile(seq, 512))
    k_nope, v_t = _kv_up(ckv, w_k, w_vt, blk)
    y_attn = _attention(q, k_nope, k_pe, v_t, gate, batch, seq, blk)
    return _out_proj(y_conv, y_attn, w_o, h, g_out, final_norm, _tile(m, 512), _tile(d, 512))


def kernel(x, g_in, w_in, conv_w, q_norm_g, w_uq, kv_norm_g, w_ukv, w_out, g_final):
    batch, seq, d = x.shape
    depth = g_in.shape[0]
    cos, sin = _rope_tables(seq)
    h = x.reshape(batch * seq, d)
    for l in range(depth):
        h = _layer(h, g_in[l], w_in[l], conv_w[l], q_norm_g[l], w_uq[l], kv_norm_g[l],
                   w_ukv[l], w_out[l], g_final, l == depth - 1, cos, sin, batch, seq)
    return h.reshape(batch, seq, d)
```

```python
import functools

import jax
import jax.numpy as jnp
import numpy as np
from jax import lax
from jax.experimental import pallas as pl
from jax.experimental.pallas import tpu as pltpu

CHUNK = 64
CONV_WIDTH = 3
N_HEADS = 16
V_HEAD_DIM = 128
QK_NOPE_DIM = 128
ROPE_DIM = 64
ROPE_HALF = ROPE_DIM // 2
Q_LORA = 1024
KV_LORA = 512
ROPE_THETA = 10000.0
NORM_EPS = 1e-6
ATTN_SCALE = (QK_NOPE_DIM + ROPE_DIM) ** -0.5
LOG2_E = float(np.log2(np.e))
Q_SCALE = ATTN_SCALE * LOG2_E

LANES = 128
SUBLANES = 8
QK_PAD_DIM = 2 * LANES
VMEM_LIMIT_BYTES = 56 * 1024 * 1024

BF16 = jnp.bfloat16
F32 = jnp.float32


def _params(*semantics):
    return pltpu.CompilerParams(dimension_semantics=semantics,
                                vmem_limit_bytes=VMEM_LIMIT_BYTES)


def _dot(a, b):
    return jnp.dot(a, b, preferred_element_type=F32)


def _dot_nt(a, b):
    return lax.dot_general(a, b, (((1,), (1,)), ((), ())), preferred_element_type=F32)


def _rmsnorm_rows(x, g):
    ms = jnp.mean(x * x, axis=-1, keepdims=True)
    return (x * lax.rsqrt(ms + NORM_EPS)) * g


def _silu(z):
    return z * jax.nn.sigmoid(z)


def _cast_kernel(w_ref, o_ref):
    o_ref[...] = w_ref[...].astype(o_ref.dtype)


def _cast_rows(w, row0, n_rows, tr, tc, name):
    cols = w.shape[1]
    assert row0 % SUBLANES == 0 and n_rows % tr == 0 and cols % tc == 0
    return pl.pallas_call(
        _cast_kernel,
        grid=(n_rows // tr, cols // tc),
        in_specs=[pl.BlockSpec((pl.Element(tr), pl.Element(tc)),
                               lambda r, c: (pl.multiple_of(row0 + r * tr, SUBLANES),
                                             pl.multiple_of(c * tc, LANES)))],
        out_specs=pl.BlockSpec((tr, tc), lambda r, c: (r, c)),
        out_shape=jax.ShapeDtypeStruct((n_rows, cols), BF16),
        compiler_params=_params("parallel", "parallel"),
        name=name,
    )(w)


def _rmsnorm_kernel(x_ref, g_ref, o_ref):
    o_ref[...] = _rmsnorm_rows(x_ref[...], g_ref[...]).astype(o_ref.dtype)


def _rmsnorm(x, g, out_dtype, tm):
    m, d = x.shape
    return pl.pallas_call(
        _rmsnorm_kernel,
        grid=(m // tm,),
        in_specs=[pl.BlockSpec((tm, d), lambda i: (i, 0)),
                  pl.BlockSpec((1, d), lambda i: (0, 0))],
        out_specs=pl.BlockSpec((tm, d), lambda i: (i, 0)),
        out_shape=jax.ShapeDtypeStruct((m, d), out_dtype),
        compiler_params=_params("parallel"),
        name="rmsnorm",
    )(x, g.reshape(1, d))


def _conv_group_kernel(xn_ref, wb_ref, wc_ref, wh_ref, wz_ref, cw_ref, y_ref, carry_ref,
                       *, tiles_per_seq):
    i = pl.program_id(0)
    j = pl.program_id(1)
    xn = xn_ref[...]
    u = _dot_nt(xn, wc_ref[...]) * _dot_nt(xn, wh_ref[...])
    tm, tc = u.shape

    @pl.when(i % tiles_per_seq == 0)
    def _():
        carry_ref[j] = jnp.zeros((SUBLANES, tc), F32)

    prev = carry_ref[j]
    carry_ref[j] = u[tm - SUBLANES:, :]

    row = lax.broadcasted_iota(jnp.int32, (SUBLANES, tc), 0)

    def shifted(k):
        body = pltpu.roll(u, k, 0)
        head = jnp.where(row < k, pltpu.roll(prev, k, 0), body[:SUBLANES, :])
        return jnp.concatenate([head, body[SUBLANES:, :]], axis=0)

    cw = cw_ref[...]
    conv = cw[0:1, :] * shifted(2) + cw[1:2, :] * shifted(1) + cw[2:3, :] * u
    y = (_dot_nt(xn, wb_ref[...]) * conv) * _silu(_dot_nt(xn, wz_ref[...]))
    y_ref[...] = y.astype(y_ref.dtype)


def _conv_group(xn, w_t, conv_w, seq, tm, tc):
    m, d = xn.shape
    dc = conv_w.shape[1]
    nj = dc // tc

    def w_spec(k):
        return pl.BlockSpec((tc, d), lambda i, j: (j + k * nj, 0))

    return pl.pallas_call(
        functools.partial(_conv_group_kernel, tiles_per_seq=seq // tm),
        grid=(m // tm, nj),
        in_specs=[pl.BlockSpec((tm, d), lambda i, j: (i, 0)),
                  w_spec(0), w_spec(1), w_spec(2), w_spec(3),
                  pl.BlockSpec((CONV_WIDTH, tc), lambda i, j: (0, j))],
        out_specs=pl.BlockSpec((tm, tc), lambda i, j: (i, j)),
        out_shape=jax.ShapeDtypeStruct((m, dc), BF16),
        scratch_shapes=[pltpu.VMEM((nj, SUBLANES, tc), F32)],
        compiler_params=_params("arbitrary", "arbitrary"),
        name="conv_group",
    )(xn, w_t, w_t, w_t, w_t, conv_w)


def _rope_pair(t, cos, sin):
    return t * cos + pltpu.roll(t, ROPE_DIM, 1) * sin


def _rope_low_half(t, cos, sin):
    lane = lax.broadcasted_iota(jnp.int32, t.shape, 1)
    partner = jnp.where(lane < ROPE_HALF,
                        -pltpu.roll(t, LANES - ROPE_HALF, 1),
                        pltpu.roll(t, ROPE_HALF, 1))
    return jnp.where(lane < ROPE_DIM, t * cos + partner * sin, 0.0)


def _latent_kernel(xn_ref, wq_ref, wkv_ref, wkr_ref, gq_ref, gkv_ref, cos_ref, sin_ref,
                   cq_ref, ckv_ref, kpe_ref):
    xn = xn_ref[...]
    cq_ref[...] = _rmsnorm_rows(_dot_nt(xn, wq_ref[...]), gq_ref[...]).astype(cq_ref.dtype)
    ckv_ref[...] = _rmsnorm_rows(_dot_nt(xn, wkv_ref[...]), gkv_ref[...]).astype(ckv_ref.dtype)
    kr = _dot_nt(xn, wkr_ref[...])
    kpe_ref[...] = _rope_low_half(kr, cos_ref[...], sin_ref[...]).astype(kpe_ref.dtype)


def _latents(xn, w_t, col0, gq, gkv, cos, sin, seq, tm):
    m, d = xn.shape
    tps = seq // tm
    assert col0 % Q_LORA == 0 and (col0 + Q_LORA) % KV_LORA == 0
    assert (col0 + Q_LORA + KV_LORA) % LANES == 0
    return pl.pallas_call(
        _latent_kernel,
        grid=(m // tm,),
        in_specs=[pl.BlockSpec((tm, d), lambda i: (i, 0)),
                  pl.BlockSpec((Q_LORA, d), lambda i: (col0 // Q_LORA, 0)),
                  pl.BlockSpec((KV_LORA, d), lambda i: ((col0 + Q_LORA) // KV_LORA, 0)),
                  pl.BlockSpec((LANES, d), lambda i: ((col0 + Q_LORA + KV_LORA) // LANES, 0)),
                  pl.BlockSpec((1, Q_LORA), lambda i: (0, 0)),
                  pl.BlockSpec((1, KV_LORA), lambda i: (0, 0)),
                  pl.BlockSpec((tm, LANES), lambda i: (i % tps, 0)),
                  pl.BlockSpec((tm, LANES), lambda i: (i % tps, 0))],
        out_specs=[pl.BlockSpec((tm, Q_LORA), lambda i: (i, 0)),
                   pl.BlockSpec((tm, KV_LORA), lambda i: (i, 0)),
                   pl.BlockSpec((tm, LANES), lambda i: (i, 0))],
        out_shape=[jax.ShapeDtypeStruct((m, Q_LORA), BF16),
                   jax.ShapeDtypeStruct((m, KV_LORA), BF16),
                   jax.ShapeDtypeStruct((m, LANES), BF16)],
        compiler_params=_params("parallel"),
        name="latents",
    )(xn, w_t, w_t, w_t, gq.reshape(1, -1), gkv.reshape(1, -1), cos, sin)


def _gate_kernel(xn_ref, w_ref, o_ref):
    o_ref[...] = _silu(_dot_nt(xn_ref[...], w_ref[...])).astype(o_ref.dtype)


def _gate(xn, w_z_t, tm, tn):
    m, d = xn.shape
    n = w_z_t.shape[0]
    return pl.pallas_call(
        _gate_kernel,
        grid=(m // tm, n // tn),
        in_specs=[pl.BlockSpec((tm, d), lambda i, j: (i, 0)),
                  pl.BlockSpec((tn, d), lambda i, j: (j, 0))],
        out_specs=pl.BlockSpec((tm, tn), lambda i, j: (i, j)),
        out_shape=jax.ShapeDtypeStruct((m, n), BF16),
        compiler_params=_params("parallel", "parallel"),
        name="attn_gate",
    )(xn, w_z_t)


def _q_up_kernel(cq_ref, w_ref, cos_ref, sin_ref, q_ref):
    r = _dot(cq_ref[...], w_ref[...])
    cos = cos_ref[...]
    sin = sin_ref[...]
    for h in range(N_HEADS):
        lo = h * QK_PAD_DIM
        nope = r[:, lo:lo + LANES] * Q_SCALE
        pe = _rope_pair(r[:, lo + LANES:lo + QK_PAD_DIM], cos, sin) * Q_SCALE
        q_ref[:, lo:lo + LANES] = nope.astype(q_ref.dtype)
        q_ref[:, lo + LANES:lo + QK_PAD_DIM] = pe.astype(q_ref.dtype)


def _q_up(cq, w_q, cos, sin, seq, tm):
    m, k = cq.shape
    n = w_q.shape[1]
    tps = seq // tm
    return pl.pallas_call(
        _q_up_kernel,
        grid=(m // tm,),
        in_specs=[pl.BlockSpec((tm, k), lambda i: (i, 0)),
                  pl.BlockSpec((k, n), lambda i: (0, 0)),
                  pl.BlockSpec((tm, LANES), lambda i: (i % tps, 0)),
                  pl.BlockSpec((tm, LANES), lambda i: (i % tps, 0))],
        out_specs=pl.BlockSpec((tm, n), lambda i: (i, 0)),
        out_shape=jax.ShapeDtypeStruct((m, n), BF16),
        compiler_params=_params("parallel"),
        name="q_up",
    )(cq, w_q, cos, sin)


def _kv_up_kernel(ckv_ref, wk_ref, wvt_ref, k_ref, vt_ref):
    c = ckv_ref[...]
    k_ref[...] = _dot(c, wk_ref[...]).astype(k_ref.dtype)
    vt_ref[...] = _dot_nt(wvt_ref[...], c).astype(vt_ref.dtype)


def _kv_up(ckv, w_k, w_vt, tk):
    m, k = ckv.shape
    n = w_k.shape[1]
    return pl.pallas_call(
        _kv_up_kernel,
        grid=(m // tk,),
        in_specs=[pl.BlockSpec((tk, k), lambda i: (i, 0)),
                  pl.BlockSpec((k, n), lambda i: (0, 0)),
                  pl.BlockSpec((n, k), lambda i: (0, 0))],
        out_specs=[pl.BlockSpec((tk, n), lambda i: (i, 0)),
                   pl.BlockSpec((None, n, tk), lambda i: (i, 0, 0))],
        out_shape=[jax.ShapeDtypeStruct((m, n), BF16),
                   jax.ShapeDtypeStruct((m // tk, n, tk), BF16)],
        compiler_params=_params("parallel"),
        name="kv_up",
    )(ckv, w_k, w_vt)


def _attention_kernel(q_ref, kn_ref, kpe_ref, vt_ref, gate_ref, o_ref, kfull_ref, *, blk):
    seq = q_ref.shape[0]
    nblk = seq // blk
    kfull_ref[:, :LANES] = kn_ref[...]
    kfull_ref[:, LANES:] = kpe_ref[...]

    key_chunk = lax.broadcasted_iota(jnp.int32, (blk, blk), 0) // CHUNK
    qry_chunk = lax.broadcasted_iota(jnp.int32, (blk, blk), 1) // CHUNK
    visible = key_chunk <= qry_chunk

    pairs = [(qi, kj) for qi in range(nblk) for kj in range(qi + 1)]

    def scores(qi, kj):
        s = _dot_nt(kfull_ref[kj * blk:(kj + 1) * blk, :],
                    q_ref[qi * blk:(qi + 1) * blk, :])
        return jnp.where(visible, s, -jnp.inf) if kj == qi else s

    m_run = l_run = acc = None
    s_next = scores(*pairs[0])
    for n, (qi, kj) in enumerate(pairs):
        s = s_next
        if n + 1 < len(pairs):
            s_next = scores(*pairs[n + 1])
        m_blk = jnp.max(s, axis=0, keepdims=True)
        if kj == 0:
            m_run = m_blk
            p = jnp.exp2(s - m_run)
            l_run = jnp.sum(p, axis=0, keepdims=True)
            acc = _dot(vt_ref[kj], p.astype(BF16))
        else:
            m_new = jnp.maximum(m_run, m_blk)
            alpha = jnp.exp2(m_run - m_new)
            p = jnp.exp2(s - m_new)
            l_run = alpha * l_run + jnp.sum(p, axis=0, keepdims=True)
            acc = alpha * acc + _dot(vt_ref[kj], p.astype(BF16))
            m_run = m_new
        if kj == qi:
            out = (acc / l_run).T
            gate = gate_ref[qi * blk:(qi + 1) * blk, :].astype(F32)
            o_ref[qi * blk:(qi + 1) * blk, :] = (out * gate).astype(o_ref.dtype)


def _attention(q, k_nope, k_pe, v_t, gate, batch, seq, blk):
    m = q.shape[0]
    nblk = seq // blk
    return pl.pallas_call(
        functools.partial(_attention_kernel, blk=blk),
        grid=(batch, N_HEADS),
        in_specs=[pl.BlockSpec((seq, QK_PAD_DIM), lambda b, h: (b, h)),
                  pl.BlockSpec((seq, LANES), lambda b, h: (b, h)),
                  pl.BlockSpec((seq, LANES), lambda b, h: (b, 0)),
                  pl.BlockSpec((nblk, V_HEAD_DIM, blk), lambda b, h: (b, h, 0)),
                  pl.BlockSpec((seq, LANES), lambda b, h: (b, h))],
        out_specs=pl.BlockSpec((seq, V_HEAD_DIM), lambda b, h: (b, h)),
        out_shape=jax.ShapeDtypeStruct((m, N_HEADS * V_HEAD_DIM), BF16),
        scratch_shapes=[pltpu.VMEM((seq, QK_PAD_DIM), BF16)],
        compiler_params=_params("parallel", "parallel"),
        name="attention",
    )(q, k_nope, k_pe, v_t, gate)


def _out_proj_kernel(yc_ref, ya_ref, wc_ref, wa_ref, x_ref, g_ref, o_ref, acc_ref,
                     *, final_norm):
    j = pl.program_id(1)
    nj = pl.num_programs(1)
    tn = x_ref.shape[1]
    h = x_ref[...] + (_dot(yc_ref[...], wc_ref[...]) + _dot(ya_ref[...], wa_ref[...]))
    acc_ref[j] = h

    @pl.when(j == nj - 1)
    def _():
        n_tiles = acc_ref.shape[0]
        if final_norm:
            ss = jnp.sum(acc_ref[0] * acc_ref[0], axis=-1, keepdims=True)
            for t in range(1, n_tiles):
                ss = ss + jnp.sum(acc_ref[t] * acc_ref[t], axis=-1, keepdims=True)
            inv = lax.rsqrt(ss / (n_tiles * tn) + NORM_EPS)
        for t in range(n_tiles):
            ht = acc_ref[t]
            if final_norm:
                ht = (ht * inv) * g_ref[:, t * tn:(t + 1) * tn]
            o_ref[:, t * tn:(t + 1) * tn] = ht.astype(o_ref.dtype)


def _out_proj(y_conv, y_attn, w_o, x, g, final_norm, tm, tn):
    m, d = x.shape
    kc = y_conv.shape[1]
    ka = y_attn.shape[1]
    assert kc == ka, "w_out is split into two equal row blocks"
    return pl.pallas_call(
        functools.partial(_out_proj_kernel, final_norm=final_norm),
        grid=(m // tm, d // tn),
        in_specs=[pl.BlockSpec((tm, kc), lambda i, j: (i, 0)),
                  pl.BlockSpec((tm, ka), lambda i, j: (i, 0)),
                  pl.BlockSpec((kc, tn), lambda i, j: (0, j)),
                  pl.BlockSpec((ka, tn), lambda i, j: (1, j)),
                  pl.BlockSpec((tm, tn), lambda i, j: (i, j)),
                  pl.BlockSpec((1, d), lambda i, j: (0, 0))],
        out_specs=pl.BlockSpec((tm, d), lambda i, j: (i, 0)),
        out_shape=jax.ShapeDtypeStruct((m, d), x.dtype),
        scratch_shapes=[pltpu.VMEM((d // tn, tm, tn), F32)],
        compiler_params=_params("parallel", "arbitrary"),
        name="out_proj",
    )(y_conv, y_attn, w_o, w_o, x, g.reshape(1, d))


def _swap_halves_signed(w):
    return jnp.concatenate([-w[..., ROPE_HALF:], w[..., :ROPE_HALF]], axis=-1)


def _rope_tables(seq):
    pos = jnp.arange(seq, dtype=F32)
    inv_freq = 1.0 / (ROPE_THETA ** (jnp.arange(0, ROPE_DIM, 2, dtype=F32) / ROPE_DIM))
    ang = pos[:, None] * inv_freq[None, :]
    zeros = jnp.zeros((seq, LANES - ROPE_DIM), F32)
    cos = jnp.concatenate([jnp.cos(ang), jnp.cos(ang), zeros], axis=-1)
    sin = jnp.concatenate([jnp.sin(ang), jnp.sin(ang), zeros], axis=-1)
    return cos, sin


def _tile(n, pref):
    return pref if n % pref == 0 else n


def _layer(h, g_in, w_in, conv_w, q_norm_g, w_uq, kv_norm_g, w_ukv, w_out, g_out,
           final_norm, cos, sin, batch, seq):
    m, d = h.shape
    dc = conv_w.shape[1]
    o = 4 * dc

    w_in_t = w_in.T
    kr0 = o + Q_LORA + KV_LORA
    w_main_t = _cast_rows(w_in_t, 0, kr0 + LANES, 7 * LANES, d // 2, "cast_w_main")
    w_z_t = _cast_rows(w_in_t, kr0 + ROPE_DIM, w_in_t.shape[0] - kr0 - ROPE_DIM, 512, d // 2,
                       "cast_w_gate")
    wq = w_uq.reshape(Q_LORA, N_HEADS, QK_NOPE_DIM + ROPE_DIM)
    wq_pe = wq[..., QK_NOPE_DIM:]
    w_q = jnp.concatenate([wq[..., :QK_NOPE_DIM], wq_pe, _swap_halves_signed(wq_pe)],
                          axis=-1).reshape(Q_LORA, N_HEADS * QK_PAD_DIM).astype(BF16)
    wkv = w_ukv.reshape(KV_LORA, N_HEADS, QK_NOPE_DIM + V_HEAD_DIM)
    w_k = wkv[..., :QK_NOPE_DIM].reshape(KV_LORA, N_HEADS * QK_NOPE_DIM).astype(BF16)
    w_vt = wkv[..., QK_NOPE_DIM:].reshape(KV_LORA, N_HEADS * V_HEAD_DIM).T.astype(BF16)
    w_o = w_out.astype(BF16)

    blk = _tile(seq, 512)
    xn = _rmsnorm(h, g_in, BF16, _tile(m, 256))
    y_conv = _conv_group(xn, w_main_t, conv_w, seq, _tile(seq, 1024), _tile(dc, 256))
    cq, ckv, k_pe = _latents(xn, w_main_t, o, q_norm_g, kv_norm_g, cos, sin, seq, _tile(seq, 512))
    gate = _gate(xn, w_z_t, _tile(m, 1024), _tile(w_z_t.shape[0], 1024))
    q = _q_up(cq, w_q, cos, sin, seq, _tile(seq, 512))
    k_nope, v_t = _kv_up(ckv, w_k, w_vt, blk)
    y_attn = _attention(q, k_nope, k_pe, v_t, gate, batch, seq, blk)
    return _out_proj(y_conv, y_attn, w_o, h, g_out, final_norm, _tile(m, 512), _tile(d, 512))


def kernel(x, g_in, w_in, conv_w, q_norm_g, w_uq, kv_norm_g, w_ukv, w_out, g_final):
    batch, seq, d = x.shape
    depth = g_in.shape[0]
    cos, sin = _rope_tables(seq)
    h = x.reshape(batch * seq, d)
    for l in range(depth):
        h = _layer(h, g_in[l], w_in[l], conv_w[l], q_norm_g[l], w_uq[l], kv_norm_g[l],
                   w_ukv[l], w_out[l], g_final, l == depth - 1, cos, sin, batch, seq)
    return h.reshape(batch, seq, d)
```

```python
import functools

import jax
import jax.numpy as jnp
import numpy as np
from jax import lax
from jax.experimental import pallas as pl
from jax.experimental.pallas import tpu as pltpu

CHUNK = 64
CONV_WIDTH = 3
N_HEADS = 16
V_HEAD_DIM = 128
QK_NOPE_DIM = 128
ROPE_DIM = 64
ROPE_HALF = ROPE_DIM // 2
Q_LORA = 1024
KV_LORA = 512
ROPE_THETA = 10000.0
NORM_EPS = 1e-6
ATTN_SCALE = (QK_NOPE_DIM + ROPE_DIM) ** -0.5
LOG2_E = float(np.log2(np.e))
Q_SCALE = ATTN_SCALE * LOG2_E

LANES = 128
SUBLANES = 8
QK_PAD_DIM = 2 * LANES
SCORE_LOOKAHEAD = 2
VMEM_LIMIT_BYTES = 56 * 1024 * 1024

BF16 = jnp.bfloat16
F32 = jnp.float32


def _params(*semantics):
    return pltpu.CompilerParams(dimension_semantics=semantics,
                                vmem_limit_bytes=VMEM_LIMIT_BYTES)


def _dot(a, b):
    return jnp.dot(a, b, preferred_element_type=F32)


def _dot_nt(a, b):
    return lax.dot_general(a, b, (((1,), (1,)), ((), ())), preferred_element_type=F32)


def _rmsnorm_rows(x, g):
    ms = jnp.mean(x * x, axis=-1, keepdims=True)
    return (x * lax.rsqrt(ms + NORM_EPS)) * g


def _silu(z):
    return z * jax.nn.sigmoid(z)


def _cast_kernel(w_ref, o_ref):
    o_ref[...] = w_ref[...].astype(o_ref.dtype)


def _cast_rows(w, row0, n_rows, tr, tc, name):
    cols = w.shape[1]
    assert row0 % SUBLANES == 0 and n_rows % tr == 0 and cols % tc == 0
    return pl.pallas_call(
        _cast_kernel,
        grid=(n_rows // tr, cols // tc),
        in_specs=[pl.BlockSpec((pl.Element(tr), pl.Element(tc)),
                               lambda r, c: (pl.multiple_of(row0 + r * tr, SUBLANES),
                                             pl.multiple_of(c * tc, LANES)))],
        out_specs=pl.BlockSpec((tr, tc), lambda r, c: (r, c)),
        out_shape=jax.ShapeDtypeStruct((n_rows, cols), BF16),
        compiler_params=_params("parallel", "parallel"),
        name=name,
    )(w)


def _rmsnorm_kernel(x_ref, g_ref, o_ref):
    o_ref[...] = _rmsnorm_rows(x_ref[...], g_ref[...]).astype(o_ref.dtype)


def _rmsnorm(x, g, out_dtype, tm):
    m, d = x.shape
    return pl.pallas_call(
        _rmsnorm_kernel,
        grid=(m // tm,),
        in_specs=[pl.BlockSpec((tm, d), lambda i: (i, 0)),
                  pl.BlockSpec((1, d), lambda i: (0, 0))],
        out_specs=pl.BlockSpec((tm, d), lambda i: (i, 0)),
        out_shape=jax.ShapeDtypeStruct((m, d), out_dtype),
        compiler_params=_params("parallel"),
        name="rmsnorm",
    )(x, g.reshape(1, d))


def _conv_group_kernel(xn_ref, wb_ref, wc_ref, wh_ref, wz_ref, cw_ref, y_ref, carry_ref,
                       *, tiles_per_seq):
    i = pl.program_id(0)
    j = pl.program_id(1)
    xn = xn_ref[...]
    u = _dot_nt(xn, wc_ref[...]) * _dot_nt(xn, wh_ref[...])
    tm, tc = u.shape

    @pl.when(i % tiles_per_seq == 0)
    def _():
        carry_ref[j] = jnp.zeros((SUBLANES, tc), F32)

    prev = carry_ref[j]
    carry_ref[j] = u[tm - SUBLANES:, :]

    row = lax.broadcasted_iota(jnp.int32, (SUBLANES, tc), 0)

    def shifted(k):
        body = pltpu.roll(u, k, 0)
        head = jnp.where(row < k, pltpu.roll(prev, k, 0), body[:SUBLANES, :])
        return jnp.concatenate([head, body[SUBLANES:, :]], axis=0)

    cw = cw_ref[...]
    conv = cw[0:1, :] * shifted(2) + cw[1:2, :] * shifted(1) + cw[2:3, :] * u
    y = (_dot_nt(xn, wb_ref[...]) * conv) * _silu(_dot_nt(xn, wz_ref[...]))
    y_ref[...] = y.astype(y_ref.dtype)


def _conv_group(xn, w_t, conv_w, seq, tm, tc):
    m, d = xn.shape
    dc = conv_w.shape[1]
    nj = dc // tc

    def w_spec(k):
        return pl.BlockSpec((tc, d), lambda i, j: (j + k * nj, 0))

    return pl.pallas_call(
        functools.partial(_conv_group_kernel, tiles_per_seq=seq // tm),
        grid=(m // tm, nj),
        in_specs=[pl.BlockSpec((tm, d), lambda i, j: (i, 0)),
                  w_spec(0), w_spec(1), w_spec(2), w_spec(3),
                  pl.BlockSpec((CONV_WIDTH, tc), lambda i, j: (0, j))],
        out_specs=pl.BlockSpec((tm, tc), lambda i, j: (i, j)),
        out_shape=jax.ShapeDtypeStruct((m, dc), BF16),
        scratch_shapes=[pltpu.VMEM((nj, SUBLANES, tc), F32)],
        compiler_params=_params("arbitrary", "arbitrary"),
        name="conv_group",
    )(xn, w_t, w_t, w_t, w_t, conv_w)


def _rope_pair(t, cos, sin):
    return t * cos + pltpu.roll(t, ROPE_DIM, 1) * sin


def _rope_low_half(t, cos, sin):
    lane = lax.broadcasted_iota(jnp.int32, t.shape, 1)
    partner = jnp.where(lane < ROPE_HALF,
                        -pltpu.roll(t, LANES - ROPE_HALF, 1),
                        pltpu.roll(t, ROPE_HALF, 1))
    return jnp.where(lane < ROPE_DIM, t * cos + partner * sin, 0.0)


def _latent_kernel(xn_ref, wq_ref, wkv_ref, wkr_ref, gq_ref, gkv_ref, cos_ref, sin_ref,
                   cq_ref, ckv_ref, kpe_ref):
    xn = xn_ref[...]
    cq_ref[...] = _rmsnorm_rows(_dot_nt(xn, wq_ref[...]), gq_ref[...]).astype(cq_ref.dtype)
    ckv_ref[...] = _rmsnorm_rows(_dot_nt(xn, wkv_ref[...]), gkv_ref[...]).astype(ckv_ref.dtype)
    kr = _dot_nt(xn, wkr_ref[...])
    kpe_ref[...] = _rope_low_half(kr, cos_ref[...], sin_ref[...]).astype(kpe_ref.dtype)


def _latents(xn, w_t, col0, gq, gkv, cos, sin, seq, tm):
    m, d = xn.shape
    tps = seq // tm
    assert col0 % Q_LORA == 0 and (col0 + Q_LORA) % KV_LORA == 0
    assert (col0 + Q_LORA + KV_LORA) % LANES == 0
    return pl.pallas_call(
        _latent_kernel,
        grid=(m // tm,),
        in_specs=[pl.BlockSpec((tm, d), lambda i: (i, 0)),
                  pl.BlockSpec((Q_LORA, d), lambda i: (col0 // Q_LORA, 0)),
                  pl.BlockSpec((KV_LORA, d), lambda i: ((col0 + Q_LORA) // KV_LORA, 0)),
                  pl.BlockSpec((LANES, d), lambda i: ((col0 + Q_LORA + KV_LORA) // LANES, 0)),
                  pl.BlockSpec((1, Q_LORA), lambda i: (0, 0)),
                  pl.BlockSpec((1, KV_LORA), lambda i: (0, 0)),
                  pl.BlockSpec((tm, LANES), lambda i: (i % tps, 0)),
                  pl.BlockSpec((tm, LANES), lambda i: (i % tps, 0))],
        out_specs=[pl.BlockSpec((tm, Q_LORA), lambda i: (i, 0)),
                   pl.BlockSpec((tm, KV_LORA), lambda i: (i, 0)),
                   pl.BlockSpec((tm, LANES), lambda i: (i, 0))],
        out_shape=[jax.ShapeDtypeStruct((m, Q_LORA), BF16),
                   jax.ShapeDtypeStruct((m, KV_LORA), BF16),
                   jax.ShapeDtypeStruct((m, LANES), BF16)],
        compiler_params=_params("parallel"),
        name="latents",
    )(xn, w_t, w_t, w_t, gq.reshape(1, -1), gkv.reshape(1, -1), cos, sin)


def _gate_kernel(xn_ref, w_ref, o_ref):
    o_ref[...] = _silu(_dot_nt(xn_ref[...], w_ref[...])).astype(o_ref.dtype)


def _gate(xn, w_z_t, tm, tn):
    m, d = xn.shape
    n = w_z_t.shape[0]
    return pl.pallas_call(
        _gate_kernel,
        grid=(m // tm, n // tn),
        in_specs=[pl.BlockSpec((tm, d), lambda i, j: (i, 0)),
                  pl.BlockSpec((tn, d), lambda i, j: (j, 0))],
        out_specs=pl.BlockSpec((tm, tn), lambda i, j: (i, j)),
        out_shape=jax.ShapeDtypeStruct((m, n), BF16),
        compiler_params=_params("parallel", "parallel"),
        name="attn_gate",
    )(xn, w_z_t)


def _q_up_kernel(cq_ref, w_ref, cos_ref, sin_ref, q_ref):
    r = _dot(cq_ref[...], w_ref[...])
    cos = cos_ref[...]
    sin = sin_ref[...]
    for h in range(N_HEADS):
        lo = h * QK_PAD_DIM
        nope = r[:, lo:lo + LANES] * Q_SCALE
        pe = _rope_pair(r[:, lo + LANES:lo + QK_PAD_DIM], cos, sin) * Q_SCALE
        q_ref[:, lo:lo + LANES] = nope.astype(q_ref.dtype)
        q_ref[:, lo + LANES:lo + QK_PAD_DIM] = pe.astype(q_ref.dtype)


def _q_up(cq, w_q, cos, sin, seq, tm):
    m, k = cq.shape
    n = w_q.shape[1]
    tps = seq // tm
    return pl.pallas_call(
        _q_up_kernel,
        grid=(m // tm,),
        in_specs=[pl.BlockSpec((tm, k), lambda i: (i, 0)),
                  pl.BlockSpec((k, n), lambda i: (0, 0)),
                  pl.BlockSpec((tm, LANES), lambda i: (i % tps, 0)),
                  pl.BlockSpec((tm, LANES), lambda i: (i % tps, 0))],
        out_specs=pl.BlockSpec((tm, n), lambda i: (i, 0)),
        out_shape=jax.ShapeDtypeStruct((m, n), BF16),
        compiler_params=_params("parallel"),
        name="q_up",
    )(cq, w_q, cos, sin)


def _kv_up_kernel(ckv_ref, wk_ref, wvt_ref, k_ref, vt_ref):
    c = ckv_ref[...]
    k_ref[...] = _dot(c, wk_ref[...]).astype(k_ref.dtype)
    vt_ref[...] = _dot_nt(wvt_ref[...], c).astype(vt_ref.dtype)


def _kv_up(ckv, w_k, w_vt, tk):
    m, k = ckv.shape
    n = w_k.shape[1]
    return pl.pallas_call(
        _kv_up_kernel,
        grid=(m // tk,),
        in_specs=[pl.BlockSpec((tk, k), lambda i: (i, 0)),
                  pl.BlockSpec((k, n), lambda i: (0, 0)),
                  pl.BlockSpec((n, k), lambda i: (0, 0))],
        out_specs=[pl.BlockSpec((tk, n), lambda i: (i, 0)),
                   pl.BlockSpec((None, n, tk), lambda i: (i, 0, 0))],
        out_shape=[jax.ShapeDtypeStruct((m, n), BF16),
                   jax.ShapeDtypeStruct((m // tk, n, tk), BF16)],
        compiler_params=_params("parallel"),
        name="kv_up",
    )(ckv, w_k, w_vt)


def _attention_kernel(q_ref, kn_ref, kpe_ref, vt_ref, gate_ref, o_ref, kfull_ref, *, blk):
    seq = q_ref.shape[0]
    nblk = seq // blk
    kfull_ref[:, :LANES] = kn_ref[...]
    kfull_ref[:, LANES:] = kpe_ref[...]

    key_chunk = lax.broadcasted_iota(jnp.int32, (blk, blk), 0) // CHUNK
    qry_chunk = lax.broadcasted_iota(jnp.int32, (blk, blk), 1) // CHUNK
    visible = key_chunk <= qry_chunk

    pairs = [(qi, kj) for qi in range(nblk) for kj in range(qi + 1)]

    def scores(qi, kj):
        s = _dot_nt(kfull_ref[kj * blk:(kj + 1) * blk, :],
                    q_ref[qi * blk:(qi + 1) * blk, :])
        return jnp.where(visible, s, -jnp.inf) if kj == qi else s

    m_run = l_run = acc = None
    ahead = [scores(*pairs[n]) for n in range(min(SCORE_LOOKAHEAD, len(pairs)))]
    for n, (qi, kj) in enumerate(pairs):
        s = ahead.pop(0)
        if n + SCORE_LOOKAHEAD < len(pairs):
            ahead.append(scores(*pairs[n + SCORE_LOOKAHEAD]))
        m_blk = jnp.max(s, axis=0, keepdims=True)
        if kj == 0:
            m_run = m_blk
            p = jnp.exp2(s - m_run)
            l_run = jnp.sum(p, axis=0, keepdims=True)
            acc = _dot(vt_ref[kj], p.astype(BF16))
        else:
            m_new = jnp.maximum(m_run, m_blk)
            alpha = jnp.exp2(m_run - m_new)
            p = jnp.exp2(s - m_new)
            l_run = alpha * l_run + jnp.sum(p, axis=0, keepdims=True)
            acc = alpha * acc + _dot(vt_ref[kj], p.astype(BF16))
            m_run = m_new
        if kj == qi:
            out = (acc / l_run).T
            gate = gate_ref[qi * blk:(qi + 1) * blk, :].astype(F32)
            o_ref[qi * blk:(qi + 1) * blk, :] = (out * gate).astype(o_ref.dtype)


def _attention(q, k_nope, k_pe, v_t, gate, batch, seq, blk):
    m = q.shape[0]
    nblk = seq // blk
    return pl.pallas_call(
        functools.partial(_attention_kernel, blk=blk),
        grid=(batch, N_HEADS),
        in_specs=[pl.BlockSpec((seq, QK_PAD_DIM), lambda b, h: (b, h)),
                  pl.BlockSpec((seq, LANES), lambda b, h: (b, h)),
                  pl.BlockSpec((seq, LANES), lambda b, h: (b, 0)),
                  pl.BlockSpec((nblk, V_HEAD_DIM, blk), lambda b, h: (b, h, 0)),
                  pl.BlockSpec((seq, LANES), lambda b, h: (b, h))],
        out_specs=pl.BlockSpec((seq, V_HEAD_DIM), lambda b, h: (b, h)),
        out_shape=jax.ShapeDtypeStruct((m, N_HEADS * V_HEAD_DIM), BF16),
        scratch_shapes=[pltpu.VMEM((seq, QK_PAD_DIM), BF16)],
        compiler_params=_params("parallel", "parallel"),
        name="attention",
    )(q, k_nope, k_pe, v_t, gate)


def _out_proj_kernel(yc_ref, ya_ref, wc_ref, wa_ref, x_ref, g_ref, o_ref, ss_ref,
                     *, final_norm):
    j = pl.program_id(1)
    tn = x_ref.shape[1]
    nj = o_ref.shape[1] // tn
    h = x_ref[...] + (_dot(yc_ref[...], wc_ref[...]) + _dot(ya_ref[...], wa_ref[...]))
    for t in range(nj):
        @pl.when(j == t)
        def _(t=t):
            o_ref[:, t * tn:(t + 1) * tn] = h.astype(o_ref.dtype)

    if final_norm:
        ss = jnp.sum(h * h, axis=-1, keepdims=True)

        @pl.when(j == 0)
        def _():
            ss_ref[...] = ss

        @pl.when(j > 0)
        def _():
            ss_ref[...] += ss

        @pl.when(j == nj - 1)
        def _():
            inv = lax.rsqrt(ss_ref[...] / (nj * tn) + NORM_EPS)
            for t in range(nj):
                ht = o_ref[:, t * tn:(t + 1) * tn]
                o_ref[:, t * tn:(t + 1) * tn] = (ht * inv) * g_ref[:, t * tn:(t + 1) * tn]


def _out_proj(y_conv, y_attn, w_o, x, g, final_norm, tm, tn):
    m, d = x.shape
    kc = y_conv.shape[1]
    ka = y_attn.shape[1]
    assert kc == ka, "w_out is split into two equal row blocks"
    return pl.pallas_call(
        functools.partial(_out_proj_kernel, final_norm=final_norm),
        grid=(m // tm, d // tn),
        in_specs=[pl.BlockSpec((tm, kc), lambda i, j: (i, 0)),
                  pl.BlockSpec((tm, ka), lambda i, j: (i, 0)),
                  pl.BlockSpec((kc, tn), lambda i, j: (0, j)),
                  pl.BlockSpec((ka, tn), lambda i, j: (1, j)),
                  pl.BlockSpec((tm, tn), lambda i, j: (i, j)),
                  pl.BlockSpec((1, d), lambda i, j: (0, 0))],
        out_specs=pl.BlockSpec((tm, d), lambda i, j: (i, 0)),
        out_shape=jax.ShapeDtypeStruct((m, d), x.dtype),
        scratch_shapes=[pltpu.VMEM((tm, 1), F32)],
        compiler_params=_params("parallel", "arbitrary"),
        name="out_proj",
    )(y_conv, y_attn, w_o, w_o, x, g.reshape(1, d))


def _swap_halves_signed(w):
    return jnp.concatenate([-w[..., ROPE_HALF:], w[..., :ROPE_HALF]], axis=-1)


def _rope_tables(seq):
    pos = jnp.arange(seq, dtype=F32)
    inv_freq = 1.0 / (ROPE_THETA ** (jnp.arange(0, ROPE_DIM, 2, dtype=F32) / ROPE_DIM))
    ang = pos[:, None] * inv_freq[None, :]
    zeros = jnp.zeros((seq, LANES - ROPE_DIM), F32)
    cos = jnp.concatenate([jnp.cos(ang), jnp.cos(ang), zeros], axis=-1)
    sin = jnp.concatenate([jnp.sin(ang), jnp.sin(ang), zeros], axis=-1)
    return cos, sin


def _tile(n, pref):
    return pref if n % pref == 0 else n


def _layer(h, g_in, w_in, conv_w, q_norm_g, w_uq, kv_norm_g, w_ukv, w_out, g_out,
           final_norm, cos, sin, batch, seq):
    m, d = h.shape
    dc = conv_w.shape[1]
    o = 4 * dc

    w_in_t = w_in.T
    kr0 = o + Q_LORA + KV_LORA
    w_main_t = _cast_rows(w_in_t, 0, kr0 + LANES, 7 * LANES, d // 2, "cast_w_main")
    w_z_t = _cast_rows(w_in_t, kr0 + ROPE_DIM, w_in_t.shape[0] - kr0 - ROPE_DIM, 512, d // 2,
                       "cast_w_gate")
    wq = w_uq.reshape(Q_LORA, N_HEADS, QK_NOPE_DIM + ROPE_DIM)
    wq_pe = wq[..., QK_NOPE_DIM:]
    w_q = jnp.concatenate([wq[..., :QK_NOPE_DIM], wq_pe, _swap_halves_signed(wq_pe)],
                          axis=-1).reshape(Q_LORA, N_HEADS * QK_PAD_DIM).astype(BF16)
    wkv = w_ukv.reshape(KV_LORA, N_HEADS, QK_NOPE_DIM + V_HEAD_DIM)
    w_k = wkv[..., :QK_NOPE_DIM].reshape(KV_LORA, N_HEADS * QK_NOPE_DIM).astype(BF16)
    w_vt = wkv[..., QK_NOPE_DIM:].reshape(KV_LORA, N_HEADS * V_HEAD_DIM).T.astype(BF16)
    w_o = w_out.astype(BF16)

    blk = _tile(seq, 512)
    xn = _rmsnorm(h, g_in, BF16, _tile(m, 256))
    y_conv = _conv_group(xn, w_main_t, conv_w, seq, _tile(seq, 1024), _tile(dc, 256))
    cq, ckv, k_pe = _latents(xn, w_main_t, o, q_norm_g, kv_norm_g, cos, sin, seq, _tile(seq, 512))
    gate = _gate(xn, w_z_t, _tile(m, 1024), _tile(w_z_t.shape[0], 1024))
    q = _q_up(cq, w_q, cos, sin, seq, _tile(seq, 512))
    k_nope, v_t = _kv_up(ckv, w_k, w_vt, blk)
    y_attn = _attention(q, k_nope, k_pe, v_t, gate, batch, seq, blk)
    return _out_proj(y_conv, y_attn, w_o, h, g_out, final_norm, _tile(m, 512), _tile(d, 1024))


def kernel(x, g_in, w_in, conv_w, q_norm_g, w_uq, kv_norm_g, w_ukv, w_out, g_final):
    batch, seq, d = x.shape
    depth = g_in.shape[0]
    cos, sin = _rope_tables(seq)
    h = x.reshape(batch * seq, d)
    for l in range(depth):
        h = _layer(h, g_in[l], w_in[l], conv_w[l], q_norm_g[l], w_uq[l], kv_norm_g[l],
                   w_ukv[l], w_out[l], g_final, l == depth - 1, cos, sin, batch, seq)
    return h.reshape(batch, seq, d)
```

```python
import functools

import jax
import jax.numpy as jnp
import numpy as np
from jax import lax
from jax.experimental import pallas as pl
from jax.experimental.pallas import tpu as pltpu

CHUNK = 64
CONV_WIDTH = 3
N_HEADS = 16
V_HEAD_DIM = 128
QK_NOPE_DIM = 128
ROPE_DIM = 64
ROPE_HALF = ROPE_DIM // 2
Q_LORA = 1024
KV_LORA = 512
ROPE_THETA = 10000.0
NORM_EPS = 1e-6
ATTN_SCALE = (QK_NOPE_DIM + ROPE_DIM) ** -0.5
LOG2_E = float(np.log2(np.e))
Q_SCALE = ATTN_SCALE * LOG2_E

LANES = 128
SUBLANES = 8
QK_PAD_DIM = 2 * LANES
SCORE_LOOKAHEAD = 2
VMEM_LIMIT_BYTES = 56 * 1024 * 1024

BF16 = jnp.bfloat16
F32 = jnp.float32


def _params(*semantics):
    return pltpu.CompilerParams(dimension_semantics=semantics,
                                vmem_limit_bytes=VMEM_LIMIT_BYTES)


def _dot(a, b):
    return jnp.dot(a, b, preferred_element_type=F32)


def _dot_nt(a, b):
    return lax.dot_general(a, b, (((1,), (1,)), ((), ())), preferred_element_type=F32)


def _rmsnorm_rows(x, g):
    ms = jnp.mean(x * x, axis=-1, keepdims=True)
    return (x * lax.rsqrt(ms + NORM_EPS)) * g


def _silu(z):
    return z * jax.nn.sigmoid(z)


def _cast_kernel(w_ref, o_ref):
    o_ref[...] = w_ref[...].astype(o_ref.dtype)


def _cast_rows(w, row0, n_rows, tr, tc, name):
    cols = w.shape[1]
    assert row0 % SUBLANES == 0 and n_rows % tr == 0 and cols % tc == 0
    return pl.pallas_call(
        _cast_kernel,
        grid=(n_rows // tr, cols // tc),
        in_specs=[pl.BlockSpec((pl.Element(tr), pl.Element(tc)),
                               lambda r, c: (pl.multiple_of(row0 + r * tr, SUBLANES),
                                             pl.multiple_of(c * tc, LANES)))],
        out_specs=pl.BlockSpec((tr, tc), lambda r, c: (r, c)),
        out_shape=jax.ShapeDtypeStruct((n_rows, cols), BF16),
        compiler_params=_params("parallel", "parallel"),
        name=name,
    )(w)


def _rmsnorm_kernel(x_ref, g_ref, o_ref):
    o_ref[...] = _rmsnorm_rows(x_ref[...], g_ref[...]).astype(o_ref.dtype)


def _rmsnorm(x, g, out_dtype, tm):
    m, d = x.shape
    return pl.pallas_call(
        _rmsnorm_kernel,
        grid=(m // tm,),
        in_specs=[pl.BlockSpec((tm, d), lambda i: (i, 0)),
                  pl.BlockSpec((1, d), lambda i: (0, 0))],
        out_specs=pl.BlockSpec((tm, d), lambda i: (i, 0)),
        out_shape=jax.ShapeDtypeStruct((m, d), out_dtype),
        compiler_params=_params("parallel"),
        name="rmsnorm",
    )(x, g.reshape(1, d))


def _conv_group_kernel(xn_ref, wb_ref, wc_ref, wh_ref, wz_ref, cw_ref, y_ref, carry_ref,
                       *, tiles_per_seq):
    i = pl.program_id(0)
    j = pl.program_id(1)
    xn = xn_ref[...]
    u = _dot_nt(xn, wc_ref[...]) * _dot_nt(xn, wh_ref[...])
    tm, tc = u.shape

    @pl.when(i % tiles_per_seq == 0)
    def _():
        carry_ref[j] = jnp.zeros((SUBLANES, tc), F32)

    prev = carry_ref[j]
    carry_ref[j] = u[tm - SUBLANES:, :]

    row = lax.broadcasted_iota(jnp.int32, (SUBLANES, tc), 0)

    def shifted(k):
        body = pltpu.roll(u, k, 0)
        head = jnp.where(row < k, pltpu.roll(prev, k, 0), body[:SUBLANES, :])
        return jnp.concatenate([head, body[SUBLANES:, :]], axis=0)

    cw = cw_ref[...]
    conv = cw[0:1, :] * shifted(2) + cw[1:2, :] * shifted(1) + cw[2:3, :] * u
    y = (_dot_nt(xn, wb_ref[...]) * conv) * _silu(_dot_nt(xn, wz_ref[...]))
    y_ref[...] = y.astype(y_ref.dtype)


def _conv_group(xn, w_t, conv_w, seq, tm, tc):
    m, d = xn.shape
    dc = conv_w.shape[1]
    nj = dc // tc

    def w_spec(k):
        return pl.BlockSpec((tc, d), lambda i, j: (j + k * nj, 0))

    return pl.pallas_call(
        functools.partial(_conv_group_kernel, tiles_per_seq=seq // tm),
        grid=(m // tm, nj),
        in_specs=[pl.BlockSpec((tm, d), lambda i, j: (i, 0)),
                  w_spec(0), w_spec(1), w_spec(2), w_spec(3),
                  pl.BlockSpec((CONV_WIDTH, tc), lambda i, j: (0, j))],
        out_specs=pl.BlockSpec((tm, tc), lambda i, j: (i, j)),
        out_shape=jax.ShapeDtypeStruct((m, dc), BF16),
        scratch_shapes=[pltpu.VMEM((nj, SUBLANES, tc), F32)],
        compiler_params=_params("arbitrary", "arbitrary"),
        name="conv_group",
    )(xn, w_t, w_t, w_t, w_t, conv_w)


def _rope_pair(t, cos, sin):
    return t * cos + pltpu.roll(t, ROPE_DIM, 1) * sin


def _rope_low_half(t, cos, sin):
    lane = lax.broadcasted_iota(jnp.int32, t.shape, 1)
    partner = jnp.where(lane < ROPE_HALF,
                        -pltpu.roll(t, LANES - ROPE_HALF, 1),
                        pltpu.roll(t, ROPE_HALF, 1))
    return jnp.where(lane < ROPE_DIM, t * cos + partner * sin, 0.0)


def _latent_kernel(xn_ref, wq_ref, wkv_ref, wkr_ref, gq_ref, gkv_ref, cos_ref, sin_ref,
                   cq_ref, ckv_ref, kpe_ref):
    xn = xn_ref[...]
    cq_ref[...] = _rmsnorm_rows(_dot_nt(xn, wq_ref[...]), gq_ref[...]).astype(cq_ref.dtype)
    ckv_ref[...] = _rmsnorm_rows(_dot_nt(xn, wkv_ref[...]), gkv_ref[...]).astype(ckv_ref.dtype)
    kr = _dot_nt(xn, wkr_ref[...])
    kpe_ref[...] = _rope_low_half(kr, cos_ref[...], sin_ref[...]).astype(kpe_ref.dtype)


def _latents(xn, w_t, col0, gq, gkv, cos, sin, seq, tm):
    m, d = xn.shape
    tps = seq // tm
    assert col0 % Q_LORA == 0 and (col0 + Q_LORA) % KV_LORA == 0
    assert (col0 + Q_LORA + KV_LORA) % LANES == 0
    return pl.pallas_call(
        _latent_kernel,
        grid=(m // tm,),
        in_specs=[pl.BlockSpec((tm, d), lambda i: (i, 0)),
                  pl.BlockSpec((Q_LORA, d), lambda i: (col0 // Q_LORA, 0)),
                  pl.BlockSpec((KV_LORA, d), lambda i: ((col0 + Q_LORA) // KV_LORA, 0)),
                  pl.BlockSpec((LANES, d), lambda i: ((col0 + Q_LORA + KV_LORA) // LANES, 0)),
                  pl.BlockSpec((1, Q_LORA), lambda i: (0, 0)),
                  pl.BlockSpec((1, KV_LORA), lambda i: (0, 0)),
                  pl.BlockSpec((tm, LANES), lambda i: (i % tps, 0)),
                  pl.BlockSpec((tm, LANES), lambda i: (i % tps, 0))],
        out_specs=[pl.BlockSpec((tm, Q_LORA), lambda i: (i, 0)),
                   pl.BlockSpec((tm, KV_LORA), lambda i: (i, 0)),
                   pl.BlockSpec((tm, LANES), lambda i: (i, 0))],
        out_shape=[jax.ShapeDtypeStruct((m, Q_LORA), BF16),
                   jax.ShapeDtypeStruct((m, KV_LORA), BF16),
                   jax.ShapeDtypeStruct((m, LANES), BF16)],
        compiler_params=_params("parallel"),
        name="latents",
    )(xn, w_t, w_t, w_t, gq.reshape(1, -1), gkv.reshape(1, -1), cos, sin)


def _gate_kernel(xn_ref, w_ref, o_ref):
    o_ref[...] = _silu(_dot_nt(xn_ref[...], w_ref[...])).astype(o_ref.dtype)


def _gate(xn, w_t, row0, n, tm, tn):
    m, d = xn.shape
    assert row0 % SUBLANES == 0 and n % tn == 0
    return pl.pallas_call(
        _gate_kernel,
        grid=(m // tm, n // tn),
        in_specs=[pl.BlockSpec((pl.Element(tm), pl.Element(d)),
                               lambda i, j: (pl.multiple_of(i * tm, tm), 0)),
                  pl.BlockSpec((pl.Element(tn), pl.Element(d)),
                               lambda i, j: (pl.multiple_of(row0 + j * tn, SUBLANES), 0))],
        out_specs=pl.BlockSpec((tm, tn), lambda i, j: (i, j)),
        out_shape=jax.ShapeDtypeStruct((m, n), BF16),
        compiler_params=_params("parallel", "parallel"),
        name="attn_gate",
    )(xn, w_t)


def _q_up_kernel(cq_ref, w_ref, cos_ref, sin_ref, q_ref):
    r = _dot(cq_ref[...], w_ref[...])
    cos = cos_ref[...]
    sin = sin_ref[...]
    for h in range(N_HEADS):
        lo = h * QK_PAD_DIM
        nope = r[:, lo:lo + LANES] * Q_SCALE
        pe = _rope_pair(r[:, lo + LANES:lo + QK_PAD_DIM], cos, sin) * Q_SCALE
        q_ref[:, lo:lo + LANES] = nope.astype(q_ref.dtype)
        q_ref[:, lo + LANES:lo + QK_PAD_DIM] = pe.astype(q_ref.dtype)


def _q_up(cq, w_q, cos, sin, seq, tm):
    m, k = cq.shape
    n = w_q.shape[1]
    tps = seq // tm
    return pl.pallas_call(
        _q_up_kernel,
        grid=(m // tm,),
        in_specs=[pl.BlockSpec((tm, k), lambda i: (i, 0)),
                  pl.BlockSpec((k, n), lambda i: (0, 0)),
                  pl.BlockSpec((tm, LANES), lambda i: (i % tps, 0)),
                  pl.BlockSpec((tm, LANES), lambda i: (i % tps, 0))],
        out_specs=pl.BlockSpec((tm, n), lambda i: (i, 0)),
        out_shape=jax.ShapeDtypeStruct((m, n), BF16),
        compiler_params=_params("parallel"),
        name="q_up",
    )(cq, w_q, cos, sin)


def _kv_up_kernel(ckv_ref, wk_ref, wvt_ref, k_ref, vt_ref):
    c = ckv_ref[...]
    k_ref[...] = _dot(c, wk_ref[...]).astype(k_ref.dtype)
    vt_ref[...] = _dot_nt(wvt_ref[...], c).astype(vt_ref.dtype)


def _kv_up(ckv, w_k, w_vt, tk):
    m, k = ckv.shape
    n = w_k.shape[1]
    return pl.pallas_call(
        _kv_up_kernel,
        grid=(m // tk,),
        in_specs=[pl.BlockSpec((tk, k), lambda i: (i, 0)),
                  pl.BlockSpec((k, n), lambda i: (0, 0)),
                  pl.BlockSpec((n, k), lambda i: (0, 0))],
        out_specs=[pl.BlockSpec((tk, n), lambda i: (i, 0)),
                   pl.BlockSpec((None, n, tk), lambda i: (i, 0, 0))],
        out_shape=[jax.ShapeDtypeStruct((m, n), BF16),
                   jax.ShapeDtypeStruct((m // tk, n, tk), BF16)],
        compiler_params=_params("parallel"),
        name="kv_up",
    )(ckv, w_k, w_vt)


def _attention_kernel(q_ref, kn_ref, kpe_ref, vt_ref, gate_ref, o_ref, kfull_ref, *, blk):
    seq = q_ref.shape[0]
    nblk = seq // blk
    kfull_ref[:, :LANES] = kn_ref[...]
    kfull_ref[:, LANES:] = kpe_ref[...]

    key_chunk = lax.broadcasted_iota(jnp.int32, (blk, blk), 0) // CHUNK
    qry_chunk = lax.broadcasted_iota(jnp.int32, (blk, blk), 1) // CHUNK
    visible = key_chunk <= qry_chunk

    pairs = [(qi, kj) for qi in range(nblk) for kj in range(qi + 1)]

    def scores(qi, kj):
        s = _dot_nt(kfull_ref[kj * blk:(kj + 1) * blk, :],
                    q_ref[qi * blk:(qi + 1) * blk, :])
        return jnp.where(visible, s, -jnp.inf) if kj == qi else s

    m_run = l_run = acc = None
    ahead = [scores(*pairs[n]) for n in range(min(SCORE_LOOKAHEAD, len(pairs)))]
    for n, (qi, kj) in enumerate(pairs):
        s = ahead.pop(0)
        if n + SCORE_LOOKAHEAD < len(pairs):
            ahead.append(scores(*pairs[n + SCORE_LOOKAHEAD]))
        m_blk = jnp.max(s, axis=0, keepdims=True)
        if kj == 0:
            m_run = m_blk
            p = jnp.exp2(s - m_run)
            l_run = jnp.sum(p, axis=0, keepdims=True)
            acc = _dot(vt_ref[kj], p.astype(BF16))
        else:
            m_new = jnp.maximum(m_run, m_blk)
            alpha = jnp.exp2(m_run - m_new)
            p = jnp.exp2(s - m_new)
            l_run = alpha * l_run + jnp.sum(p, axis=0, keepdims=True)
            acc = alpha * acc + _dot(vt_ref[kj], p.astype(BF16))
            m_run = m_new
        if kj == qi:
            out = (acc / l_run).T
            gate = gate_ref[qi * blk:(qi + 1) * blk, :].astype(F32)
            o_ref[qi * blk:(qi + 1) * blk, :] = (out * gate).astype(o_ref.dtype)


def _attention(q, k_nope, k_pe, v_t, gate, batch, seq, blk):
    m = q.shape[0]
    nblk = seq // blk
    return pl.pallas_call(
        functools.partial(_attention_kernel, blk=blk),
        grid=(batch, N_HEADS),
        in_specs=[pl.BlockSpec((seq, QK_PAD_DIM), lambda b, h: (b, h)),
                  pl.BlockSpec((seq, LANES), lambda b, h: (b, h)),
                  pl.BlockSpec((seq, LANES), lambda b, h: (b, 0)),
                  pl.BlockSpec((nblk, V_HEAD_DIM, blk), lambda b, h: (b, h, 0)),
                  pl.BlockSpec((seq, LANES), lambda b, h: (b, h))],
        out_specs=pl.BlockSpec((seq, V_HEAD_DIM), lambda b, h: (b, h)),
        out_shape=jax.ShapeDtypeStruct((m, N_HEADS * V_HEAD_DIM), BF16),
        scratch_shapes=[pltpu.VMEM((seq, QK_PAD_DIM), BF16)],
        compiler_params=_params("parallel", "parallel"),
        name="attention",
    )(q, k_nope, k_pe, v_t, gate)


def _out_proj_kernel(yc_ref, ya_ref, wc_ref, wa_ref, x_ref, g_ref, o_ref, ss_ref,
                     *, final_norm):
    j = pl.program_id(1)
    tn = x_ref.shape[1]
    nj = o_ref.shape[1] // tn
    h = x_ref[...] + (_dot(yc_ref[...], wc_ref[...]) + _dot(ya_ref[...], wa_ref[...]))
    for t in range(nj):
        @pl.when(j == t)
        def _(t=t):
            o_ref[:, t * tn:(t + 1) * tn] = h.astype(o_ref.dtype)

    if final_norm:
        ss = jnp.sum(h * h, axis=-1, keepdims=True)

        @pl.when(j == 0)
        def _():
            ss_ref[...] = ss

        @pl.when(j > 0)
        def _():
            ss_ref[...] += ss

        @pl.when(j == nj - 1)
        def _():
            inv = lax.rsqrt(ss_ref[...] / (nj * tn) + NORM_EPS)
            for t in range(nj):
                ht = o_ref[:, t * tn:(t + 1) * tn]
                o_ref[:, t * tn:(t + 1) * tn] = (ht * inv) * g_ref[:, t * tn:(t + 1) * tn]


def _out_proj(y_conv, y_attn, w_o, x, g, final_norm, tm, tn):
    m, d = x.shape
    kc = y_conv.shape[1]
    ka = y_attn.shape[1]
    assert kc == ka, "w_out is split into two equal row blocks"
    return pl.pallas_call(
        functools.partial(_out_proj_kernel, final_norm=final_norm),
        grid=(m // tm, d // tn),
        in_specs=[pl.BlockSpec((tm, kc), lambda i, j: (i, 0)),
                  pl.BlockSpec((tm, ka), lambda i, j: (i, 0)),
                  pl.BlockSpec((kc, tn), lambda i, j: (0, j)),
                  pl.BlockSpec((ka, tn), lambda i, j: (1, j)),
                  pl.BlockSpec((tm, tn), lambda i, j: (i, j)),
                  pl.BlockSpec((1, d), lambda i, j: (0, 0))],
        out_specs=pl.BlockSpec((tm, d), lambda i, j: (i, 0), pipeline_mode=pl.Buffered(1)),
        out_shape=jax.ShapeDtypeStruct((m, d), x.dtype),
        scratch_shapes=[pltpu.VMEM((tm, 1), F32)],
        compiler_params=_params("parallel", "arbitrary"),
        name="out_proj",
    )(y_conv, y_attn, w_o, w_o, x, g.reshape(1, d))


def _swap_halves_signed(w):
    return jnp.concatenate([-w[..., ROPE_HALF:], w[..., :ROPE_HALF]], axis=-1)


def _rope_tables(seq):
    pos = jnp.arange(seq, dtype=F32)
    inv_freq = 1.0 / (ROPE_THETA ** (jnp.arange(0, ROPE_DIM, 2, dtype=F32) / ROPE_DIM))
    ang = pos[:, None] * inv_freq[None, :]
    zeros = jnp.zeros((seq, LANES - ROPE_DIM), F32)
    cos = jnp.concatenate([jnp.cos(ang), jnp.cos(ang), zeros], axis=-1)
    sin = jnp.concatenate([jnp.sin(ang), jnp.sin(ang), zeros], axis=-1)
    return cos, sin


def _tile(n, pref):
    return pref if n % pref == 0 else n


def _layer(h, g_in, w_in, conv_w, q_norm_g, w_uq, kv_norm_g, w_ukv, w_out, g_out,
           final_norm, cos, sin, batch, seq):
    m, d = h.shape
    dc = conv_w.shape[1]
    o = 4 * dc

    w_in_t = w_in.T
    kr0 = o + Q_LORA + KV_LORA
    w_main_t = _cast_rows(w_in_t, 0, kr0 + LANES, 7 * LANES, d // 2, "cast_w_main")
    wq = w_uq.reshape(Q_LORA, N_HEADS, QK_NOPE_DIM + ROPE_DIM)
    wq_pe = wq[..., QK_NOPE_DIM:]
    w_q = jnp.concatenate([wq[..., :QK_NOPE_DIM], wq_pe, _swap_halves_signed(wq_pe)],
                          axis=-1).reshape(Q_LORA, N_HEADS * QK_PAD_DIM).astype(BF16)
    wkv = w_ukv.reshape(KV_LORA, N_HEADS, QK_NOPE_DIM + V_HEAD_DIM)
    w_k = wkv[..., :QK_NOPE_DIM].reshape(KV_LORA, N_HEADS * QK_NOPE_DIM).astype(BF16)
    w_vt = wkv[..., QK_NOPE_DIM:].reshape(KV_LORA, N_HEADS * V_HEAD_DIM).T.astype(BF16)
    w_o = w_out.astype(BF16)

    blk = _tile(seq, 512)
    xn = _rmsnorm(h, g_in, BF16, _tile(m, 256))
    y_conv = _conv_group(xn, w_main_t, conv_w, seq, _tile(seq, 1024), _tile(dc, 256))
    cq, ckv, k_pe = _latents(xn, w_main_t, o, q_norm_g, kv_norm_g, cos, sin, seq, _tile(seq, 512))
    gate = _gate(xn, w_in_t, kr0 + ROPE_DIM, w_in_t.shape[0] - kr0 - ROPE_DIM,
                 _tile(m, 1024), 512)
    q = _q_up(cq, w_q, cos, sin, seq, _tile(seq, 512))
    k_nope, v_t = _kv_up(ckv, w_k, w_vt, blk)
    y_attn = _attention(q, k_nope, k_pe, v_t, gate, batch, seq, blk)
    return _out_proj(y_conv, y_attn, w_o, h, g_out, final_norm, _tile(m, 1024), _tile(d, 512))


def kernel(x, g_in, w_in, conv_w, q_norm_g, w_uq, kv_norm_g, w_ukv, w_out, g_final):
    batch, seq, d = x.shape
    depth = g_in.shape[0]
    cos, sin = _rope_tables(seq)
    h = x.reshape(batch * seq, d)
    for l in range(depth):
        h = _layer(h, g_in[l], w_in[l], conv_w[l], q_norm_g[l], w_uq[l], kv_norm_g[l],
                   w_ukv[l], w_out[l], g_final, l == depth - 1, cos, sin, batch, seq)
    return h.reshape(batch, seq, d)
```

```python
import functools

import jax
import jax.numpy as jnp
import numpy as np
from jax import lax
from jax.experimental import pallas as pl
from jax.experimental.pallas import tpu as pltpu

CHUNK = 64
CONV_WIDTH = 3
N_HEADS = 16
V_HEAD_DIM = 128
QK_NOPE_DIM = 128
ROPE_DIM = 64
ROPE_HALF = ROPE_DIM // 2
Q_LORA = 1024
KV_LORA = 512
ROPE_THETA = 10000.0
NORM_EPS = 1e-6
ATTN_SCALE = (QK_NOPE_DIM + ROPE_DIM) ** -0.5
LOG2_E = float(np.log2(np.e))
Q_SCALE = ATTN_SCALE * LOG2_E

LANES = 128
SUBLANES = 8
QK_PAD_DIM = 2 * LANES
SCORE_LOOKAHEAD = 2
VMEM_LIMIT_BYTES = 56 * 1024 * 1024

BF16 = jnp.bfloat16
F32 = jnp.float32


def _params(*semantics):
    return pltpu.CompilerParams(dimension_semantics=semantics,
                                vmem_limit_bytes=VMEM_LIMIT_BYTES)


def _dot(a, b):
    return jnp.dot(a, b, preferred_element_type=F32)


def _dot_nt(a, b):
    return lax.dot_general(a, b, (((1,), (1,)), ((), ())), preferred_element_type=F32)


def _rmsnorm_rows(x, g):
    ms = jnp.mean(x * x, axis=-1, keepdims=True)
    return (x * lax.rsqrt(ms + NORM_EPS)) * g


def _silu(z):
    return z * jax.nn.sigmoid(z)


def _cast_kernel(w_ref, o_ref):
    o_ref[...] = w_ref[...].astype(o_ref.dtype)


def _cast_rows(w, row0, n_rows, tr, tc, name):
    cols = w.shape[1]
    assert row0 % SUBLANES == 0 and n_rows % tr == 0 and cols % tc == 0
    return pl.pallas_call(
        _cast_kernel,
        grid=(n_rows // tr, cols // tc),
        in_specs=[pl.BlockSpec((pl.Element(tr), pl.Element(tc)),
                               lambda r, c: (pl.multiple_of(row0 + r * tr, SUBLANES),
                                             pl.multiple_of(c * tc, LANES)))],
        out_specs=pl.BlockSpec((tr, tc), lambda r, c: (r, c)),
        out_shape=jax.ShapeDtypeStruct((n_rows, cols), BF16),
        compiler_params=_params("parallel", "parallel"),
        name=name,
    )(w)


def _rmsnorm_kernel(x_ref, g_ref, o_ref):
    o_ref[...] = _rmsnorm_rows(x_ref[...], g_ref[...]).astype(o_ref.dtype)


def _rmsnorm(x, g, out_dtype, tm):
    m, d = x.shape
    return pl.pallas_call(
        _rmsnorm_kernel,
        grid=(m // tm,),
        in_specs=[pl.BlockSpec((tm, d), lambda i: (i, 0)),
                  pl.BlockSpec((1, d), lambda i: (0, 0))],
        out_specs=pl.BlockSpec((tm, d), lambda i: (i, 0)),
        out_shape=jax.ShapeDtypeStruct((m, d), out_dtype),
        compiler_params=_params("parallel"),
        name="rmsnorm",
    )(x, g.reshape(1, d))


def _conv_group_kernel(xn_ref, wb_ref, wc_ref, wh_ref, wz_ref, cw_ref, y_ref, carry_ref,
                       *, tiles_per_seq):
    i = pl.program_id(0)
    j = pl.program_id(1)
    xn = xn_ref[...]
    u = _dot_nt(xn, wc_ref[...]) * _dot_nt(xn, wh_ref[...])
    tm, tc = u.shape

    @pl.when(i % tiles_per_seq == 0)
    def _():
        carry_ref[j] = jnp.zeros((SUBLANES, tc), F32)

    prev = carry_ref[j]
    carry_ref[j] = u[tm - SUBLANES:, :]

    row = lax.broadcasted_iota(jnp.int32, (SUBLANES, tc), 0)

    def shifted(k):
        body = pltpu.roll(u, k, 0)
        head = jnp.where(row < k, pltpu.roll(prev, k, 0), body[:SUBLANES, :])
        return jnp.concatenate([head, body[SUBLANES:, :]], axis=0)

    cw = cw_ref[...]
    conv = cw[0:1, :] * shifted(2) + cw[1:2, :] * shifted(1) + cw[2:3, :] * u
    y = (_dot_nt(xn, wb_ref[...]) * conv) * _silu(_dot_nt(xn, wz_ref[...]))
    y_ref[...] = y.astype(y_ref.dtype)


def _conv_group(xn, w_t, conv_w, seq, tm, tc):
    m, d = xn.shape
    dc = conv_w.shape[1]
    nj = dc // tc

    def w_spec(k):
        return pl.BlockSpec((tc, d), lambda i, j: (j + k * nj, 0))

    return pl.pallas_call(
        functools.partial(_conv_group_kernel, tiles_per_seq=seq // tm),
        grid=(m // tm, nj),
        in_specs=[pl.BlockSpec((tm, d), lambda i, j: (i, 0)),
                  w_spec(0), w_spec(1), w_spec(2), w_spec(3),
                  pl.BlockSpec((CONV_WIDTH, tc), lambda i, j: (0, j))],
        out_specs=pl.BlockSpec((tm, tc), lambda i, j: (i, j)),
        out_shape=jax.ShapeDtypeStruct((m, dc), BF16),
        scratch_shapes=[pltpu.VMEM((nj, SUBLANES, tc), F32)],
        compiler_params=_params("arbitrary", "arbitrary"),
        name="conv_group",
    )(xn, w_t, w_t, w_t, w_t, conv_w)


def _rope_pair(t, cos, sin):
    return t * cos + pltpu.roll(t, ROPE_DIM, 1) * sin


def _rope_low_half(t, cos, sin):
    lane = lax.broadcasted_iota(jnp.int32, t.shape, 1)
    partner = jnp.where(lane < ROPE_HALF,
                        -pltpu.roll(t, LANES - ROPE_HALF, 1),
                        pltpu.roll(t, ROPE_HALF, 1))
    return jnp.where(lane < ROPE_DIM, t * cos + partner * sin, 0.0)


def _latent_kernel(xn_ref, wq_ref, wkv_ref, wkr_ref, gq_ref, gkv_ref, cos_ref, sin_ref,
                   cq_ref, ckv_ref, kpe_ref):
    xn = xn_ref[...]
    cq_ref[...] = _rmsnorm_rows(_dot_nt(xn, wq_ref[...]), gq_ref[...]).astype(cq_ref.dtype)
    ckv_ref[...] = _rmsnorm_rows(_dot_nt(xn, wkv_ref[...]), gkv_ref[...]).astype(ckv_ref.dtype)
    kr = _dot_nt(xn, wkr_ref[...])
    kpe_ref[...] = _rope_low_half(kr, cos_ref[...], sin_ref[...]).astype(kpe_ref.dtype)


def _latents(xn, w_t, col0, gq, gkv, cos, sin, seq, tm):
    m, d = xn.shape
    tps = seq // tm
    assert col0 % Q_LORA == 0 and (col0 + Q_LORA) % KV_LORA == 0
    assert (col0 + Q_LORA + KV_LORA) % LANES == 0
    return pl.pallas_call(
        _latent_kernel,
        grid=(m // tm,),
        in_specs=[pl.BlockSpec((tm, d), lambda i: (i, 0)),
                  pl.BlockSpec((Q_LORA, d), lambda i: (col0 // Q_LORA, 0)),
                  pl.BlockSpec((KV_LORA, d), lambda i: ((col0 + Q_LORA) // KV_LORA, 0)),
                  pl.BlockSpec((LANES, d), lambda i: ((col0 + Q_LORA + KV_LORA) // LANES, 0)),
                  pl.BlockSpec((1, Q_LORA), lambda i: (0, 0)),
                  pl.BlockSpec((1, KV_LORA), lambda i: (0, 0)),
                  pl.BlockSpec((tm, LANES), lambda i: (i % tps, 0)),
                  pl.BlockSpec((tm, LANES), lambda i: (i % tps, 0))],
        out_specs=[pl.BlockSpec((tm, Q_LORA), lambda i: (i, 0)),
                   pl.BlockSpec((tm, KV_LORA), lambda i: (i, 0)),
                   pl.BlockSpec((tm, LANES), lambda i: (i, 0))],
        out_shape=[jax.ShapeDtypeStruct((m, Q_LORA), BF16),
                   jax.ShapeDtypeStruct((m, KV_LORA), BF16),
                   jax.ShapeDtypeStruct((m, LANES), BF16)],
        compiler_params=_params("parallel"),
        name="latents",
    )(xn, w_t, w_t, w_t, gq.reshape(1, -1), gkv.reshape(1, -1), cos, sin)


def _gate_kernel(xn_ref, w_ref, o_ref):
    o_ref[...] = _silu(_dot_nt(xn_ref[...], w_ref[...])).astype(o_ref.dtype)


def _gate(xn, w_z_t, tm, tn):
    m, d = xn.shape
    n = w_z_t.shape[0]
    return pl.pallas_call(
        _gate_kernel,
        grid=(m // tm, n // tn),
        in_specs=[pl.BlockSpec((tm, d), lambda i, j: (i, 0)),
                  pl.BlockSpec((tn, d), lambda i, j: (j, 0))],
        out_specs=pl.BlockSpec((tm, tn), lambda i, j: (i, j)),
        out_shape=jax.ShapeDtypeStruct((m, n), BF16),
        compiler_params=_params("parallel", "parallel"),
        name="attn_gate",
    )(xn, w_z_t)


def _q_up_kernel(cq_ref, w_ref, cos_ref, sin_ref, q_ref):
    r = _dot(cq_ref[...], w_ref[...])
    cos = cos_ref[...]
    sin = sin_ref[...]
    for h in range(N_HEADS):
        lo = h * QK_PAD_DIM
        nope = r[:, lo:lo + LANES] * Q_SCALE
        pe = _rope_pair(r[:, lo + LANES:lo + QK_PAD_DIM], cos, sin) * Q_SCALE
        q_ref[:, lo:lo + LANES] = nope.astype(q_ref.dtype)
        q_ref[:, lo + LANES:lo + QK_PAD_DIM] = pe.astype(q_ref.dtype)


def _q_up(cq, w_q, cos, sin, seq, tm):
    m, k = cq.shape
    n = w_q.shape[1]
    tps = seq // tm
    return pl.pallas_call(
        _q_up_kernel,
        grid=(m // tm,),
        in_specs=[pl.BlockSpec((tm, k), lambda i: (i, 0)),
                  pl.BlockSpec((k, n), lambda i: (0, 0)),
                  pl.BlockSpec((tm, LANES), lambda i: (i % tps, 0)),
                  pl.BlockSpec((tm, LANES), lambda i: (i % tps, 0))],
        out_specs=pl.BlockSpec((tm, n), lambda i: (i, 0)),
        out_shape=jax.ShapeDtypeStruct((m, n), BF16),
        compiler_params=_params("parallel"),
        name="q_up",
    )(cq, w_q, cos, sin)


def _kv_up_kernel(ckv_ref, wk_ref, wvt_ref, k_ref, vt_ref):
    c = ckv_ref[...]
    k_ref[...] = _dot(c, wk_ref[...]).astype(k_ref.dtype)
    vt_ref[...] = _dot_nt(wvt_ref[...], c).astype(vt_ref.dtype)


def _kv_up(ckv, w_k, w_vt, tk):
    m, k = ckv.shape
    n = w_k.shape[1]
    return pl.pallas_call(
        _kv_up_kernel,
        grid=(m // tk,),
        in_specs=[pl.BlockSpec((tk, k), lambda i: (i, 0)),
                  pl.BlockSpec((k, n), lambda i: (0, 0)),
                  pl.BlockSpec((n, k), lambda i: (0, 0))],
        out_specs=[pl.BlockSpec((tk, n), lambda i: (i, 0)),
                   pl.BlockSpec((None, n, tk), lambda i: (i, 0, 0))],
        out_shape=[jax.ShapeDtypeStruct((m, n), BF16),
                   jax.ShapeDtypeStruct((m // tk, n, tk), BF16)],
        compiler_params=_params("parallel"),
        name="kv_up",
    )(ckv, w_k, w_vt)


CAST_PIECE_ROWS = 16


def _attention_kernel(q_ref, kn_ref, kpe_ref, vt_ref, gate_ref, *rest, blk, n_casts):
    cast_in = rest[:n_casts]
    o_ref = rest[n_casts]
    cast_out = rest[n_casts + 1:2 * n_casts + 1]
    kfull_ref = rest[2 * n_casts + 1]
    cast_pieces = [(src, dst, r) for src, dst in zip(cast_in, cast_out)
                   for r in range(0, src.shape[0], CAST_PIECE_ROWS)]
    seq = q_ref.shape[0]
    nblk = seq // blk
    kfull_ref[:, :LANES] = kn_ref[...]
    kfull_ref[:, LANES:] = kpe_ref[...]

    key_chunk = lax.broadcasted_iota(jnp.int32, (blk, blk), 0) // CHUNK
    qry_chunk = lax.broadcasted_iota(jnp.int32, (blk, blk), 1) // CHUNK
    visible = key_chunk <= qry_chunk

    pairs = [(qi, kj) for qi in range(nblk) for kj in range(qi + 1)]

    def scores(qi, kj):
        s = _dot_nt(kfull_ref[kj * blk:(kj + 1) * blk, :],
                    q_ref[qi * blk:(qi + 1) * blk, :])
        return jnp.where(visible, s, -jnp.inf) if kj == qi else s

    m_run = l_run = acc = None
    ahead = [scores(*pairs[n]) for n in range(min(SCORE_LOOKAHEAD, len(pairs)))]
    for n, (qi, kj) in enumerate(pairs):
        s = ahead.pop(0)
        if n + SCORE_LOOKAHEAD < len(pairs):
            ahead.append(scores(*pairs[n + SCORE_LOOKAHEAD]))
        m_blk = jnp.max(s, axis=0, keepdims=True)
        if kj == 0:
            m_run = m_blk
            p = jnp.exp2(s - m_run)
            l_run = jnp.sum(p, axis=0, keepdims=True)
            acc = _dot(vt_ref[kj], p.astype(BF16))
        else:
            m_new = jnp.maximum(m_run, m_blk)
            alpha = jnp.exp2(m_run - m_new)
            p = jnp.exp2(s - m_new)
            l_run = alpha * l_run + jnp.sum(p, axis=0, keepdims=True)
            acc = alpha * acc + _dot(vt_ref[kj], p.astype(BF16))
            m_run = m_new
        if kj == qi:
            out = (acc / l_run).T
            gate = gate_ref[qi * blk:(qi + 1) * blk, :].astype(F32)
            o_ref[qi * blk:(qi + 1) * blk, :] = (out * gate).astype(o_ref.dtype)
        lo = n * len(cast_pieces) // len(pairs)
        hi = (n + 1) * len(cast_pieces) // len(pairs)
        for src, dst, r in cast_pieces[lo:hi]:
            dst[r:r + CAST_PIECE_ROWS, :] = src[r:r + CAST_PIECE_ROWS, :].astype(dst.dtype)


def _attention(q, k_nope, k_pe, v_t, gate, batch, seq, blk, casts):
    m = q.shape[0]
    nblk = seq // blk
    steps = batch * N_HEADS
    cast_specs, cast_shapes = [], []
    for w, n_rows in casts:
        rows = n_rows // steps
        assert rows * steps == n_rows and rows % CAST_PIECE_ROWS == 0
        cast_specs.append(pl.BlockSpec((rows, w.shape[1]), lambda b, h: (b * N_HEADS + h, 0)))
        cast_shapes.append(jax.ShapeDtypeStruct((n_rows, w.shape[1]), BF16))
    return pl.pallas_call(
        functools.partial(_attention_kernel, blk=blk, n_casts=len(casts)),
        grid=(batch, N_HEADS),
        in_specs=[pl.BlockSpec((seq, QK_PAD_DIM), lambda b, h: (b, h)),
                  pl.BlockSpec((seq, LANES), lambda b, h: (b, h)),
                  pl.BlockSpec((seq, LANES), lambda b, h: (b, 0)),
                  pl.BlockSpec((nblk, V_HEAD_DIM, blk), lambda b, h: (b, h, 0)),
                  pl.BlockSpec((seq, LANES), lambda b, h: (b, h))] + cast_specs,
        out_specs=[pl.BlockSpec((seq, V_HEAD_DIM), lambda b, h: (b, h))] + cast_specs,
        out_shape=[jax.ShapeDtypeStruct((m, N_HEADS * V_HEAD_DIM), BF16)] + cast_shapes,
        scratch_shapes=[pltpu.VMEM((seq, QK_PAD_DIM), BF16)],
        compiler_params=_params("parallel", "parallel"),
        name="attention",
    )(q, k_nope, k_pe, v_t, gate, *[w for w, _ in casts])


def _out_proj_kernel(yc_ref, ya_ref, wc_ref, wa_ref, x_ref, g_ref, o_ref, ss_ref,
                     *, final_norm):
    j = pl.program_id(1)
    tn = x_ref.shape[1]
    nj = o_ref.shape[1] // tn
    h = x_ref[...] + (_dot(yc_ref[...], wc_ref[...]) + _dot(ya_ref[...], wa_ref[...]))
    for t in range(nj):
        @pl.when(j == t)
        def _(t=t):
            o_ref[:, t * tn:(t + 1) * tn] = h.astype(o_ref.dtype)

    if final_norm:
        ss = jnp.sum(h * h, axis=-1, keepdims=True)

        @pl.when(j == 0)
        def _():
            ss_ref[...] = ss

        @pl.when(j > 0)
        def _():
            ss_ref[...] += ss

        @pl.when(j == nj - 1)
        def _():
            inv = lax.rsqrt(ss_ref[...] / (nj * tn) + NORM_EPS)
            for t in range(nj):
                ht = o_ref[:, t * tn:(t + 1) * tn]
                o_ref[:, t * tn:(t + 1) * tn] = (ht * inv) * g_ref[:, t * tn:(t + 1) * tn]


def _out_proj(y_conv, y_attn, w_o, x, g, final_norm, tm, tn):
    m, d = x.shape
    kc = y_conv.shape[1]
    ka = y_attn.shape[1]
    assert kc == ka, "w_out is split into two equal row blocks"
    return pl.pallas_call(
        functools.partial(_out_proj_kernel, final_norm=final_norm),
        grid=(m // tm, d // tn),
        in_specs=[pl.BlockSpec((tm, kc), lambda i, j: (i, 0)),
                  pl.BlockSpec((tm, ka), lambda i, j: (i, 0)),
                  pl.BlockSpec((kc, tn), lambda i, j: (0, j)),
                  pl.BlockSpec((ka, tn), lambda i, j: (1, j)),
                  pl.BlockSpec((tm, tn), lambda i, j: (i, j)),
                  pl.BlockSpec((1, d), lambda i, j: (0, 0))],
        out_specs=pl.BlockSpec((tm, d), lambda i, j: (i, 0)),
        out_shape=jax.ShapeDtypeStruct((m, d), x.dtype),
        scratch_shapes=[pltpu.VMEM((tm, 1), F32)],
        compiler_params=_params("parallel", "arbitrary"),
        name="out_proj",
    )(y_conv, y_attn, w_o, w_o, x, g.reshape(1, d))


def _swap_halves_signed(w):
    return jnp.concatenate([-w[..., ROPE_HALF:], w[..., :ROPE_HALF]], axis=-1)


def _rope_tables(seq):
    pos = jnp.arange(seq, dtype=F32)
    inv_freq = 1.0 / (ROPE_THETA ** (jnp.arange(0, ROPE_DIM, 2, dtype=F32) / ROPE_DIM))
    ang = pos[:, None] * inv_freq[None, :]
    zeros = jnp.zeros((seq, LANES - ROPE_DIM), F32)
    cos = jnp.concatenate([jnp.cos(ang), jnp.cos(ang), zeros], axis=-1)
    sin = jnp.concatenate([jnp.sin(ang), jnp.sin(ang), zeros], axis=-1)
    return cos, sin


def _tile(n, pref):
    return pref if n % pref == 0 else n


def _layer(h, g_in, w_in, conv_w, q_norm_g, w_uq, kv_norm_g, w_ukv, w_out, g_out,
           final_norm, cos, sin, batch, seq):
    m, d = h.shape
    dc = conv_w.shape[1]
    o = 4 * dc

    w_in_t = w_in.T
    kr0 = o + Q_LORA + KV_LORA
    n_lat = Q_LORA + KV_LORA + LANES
    w_lat_t = _cast_rows(w_in_t, o, n_lat, n_lat // 2, d // 2, "cast_w_latent")
    w_z_t = _cast_rows(w_in_t, kr0 + ROPE_DIM, w_in_t.shape[0] - kr0 - ROPE_DIM, 512, d // 2,
                       "cast_w_gate")
    wq = w_uq.reshape(Q_LORA, N_HEADS, QK_NOPE_DIM + ROPE_DIM)
    wq_pe = wq[..., QK_NOPE_DIM:]
    w_q = jnp.concatenate([wq[..., :QK_NOPE_DIM], wq_pe, _swap_halves_signed(wq_pe)],
                          axis=-1).reshape(Q_LORA, N_HEADS * QK_PAD_DIM).astype(BF16)
    wkv = w_ukv.reshape(KV_LORA, N_HEADS, QK_NOPE_DIM + V_HEAD_DIM)
    w_k = wkv[..., :QK_NOPE_DIM].reshape(KV_LORA, N_HEADS * QK_NOPE_DIM).astype(BF16)
    w_vt = wkv[..., QK_NOPE_DIM:].reshape(KV_LORA, N_HEADS * V_HEAD_DIM).T.astype(BF16)

    blk = _tile(seq, 512)
    xn = _rmsnorm(h, g_in, BF16, _tile(m, 256))
    cq, ckv, k_pe = _latents(xn, w_lat_t, 0, q_norm_g, kv_norm_g, cos, sin, seq, _tile(seq, 512))
    gate = _gate(xn, w_z_t, _tile(m, 1024), _tile(w_z_t.shape[0], 1024))
    q = _q_up(cq, w_q, cos, sin, seq, _tile(seq, 512))
    k_nope, v_t = _kv_up(ckv, w_k, w_vt, blk)
    y_attn, w_conv_t, w_o = _attention(q, k_nope, k_pe, v_t, gate, batch, seq, blk,
                                       [(w_in_t, o), (w_out, w_out.shape[0])])
    y_conv = _conv_group(xn, w_conv_t, conv_w, seq, _tile(seq, 1024), _tile(dc, 256))
    return _out_proj(y_conv, y_attn, w_o, h, g_out, final_norm, _tile(m, 512), _tile(d, 1024))


def kernel(x, g_in, w_in, conv_w, q_norm_g, w_uq, kv_norm_g, w_ukv, w_out, g_final):
    batch, seq, d = x.shape
    depth = g_in.shape[0]
    cos, sin = _rope_tables(seq)
    h = x.reshape(batch * seq, d)
    for l in range(depth):
        h = _layer(h, g_in[l], w_in[l], conv_w[l], q_norm_g[l], w_uq[l], kv_norm_g[l],
                   w_ukv[l], w_out[l], g_final, l == depth - 1, cos, sin, batch, seq)
    return h.reshape(batch, seq, d)
```

```python
import functools

import jax
import jax.numpy as jnp
import numpy as np
from jax import lax
from jax.experimental import pallas as pl
from jax.experimental.pallas import tpu as pltpu

CHUNK = 64
CONV_WIDTH = 3
N_HEADS = 16
V_HEAD_DIM = 128
QK_NOPE_DIM = 128
ROPE_DIM = 64
ROPE_HALF = ROPE_DIM // 2
Q_LORA = 1024
KV_LORA = 512
ROPE_THETA = 10000.0
NORM_EPS = 1e-6
ATTN_SCALE = (QK_NOPE_DIM + ROPE_DIM) ** -0.5
LOG2_E = float(np.log2(np.e))
Q_SCALE = ATTN_SCALE * LOG2_E

LANES = 128
SUBLANES = 8
QK_PAD_DIM = 2 * LANES
SCORE_LOOKAHEAD = 2
VMEM_LIMIT_BYTES = 56 * 1024 * 1024
OUT_PROJ_VMEM_LIMIT_BYTES = 60 * 1024 * 1024

BF16 = jnp.bfloat16
F32 = jnp.float32


def _params(*semantics):
    return pltpu.CompilerParams(dimension_semantics=semantics,
                                vmem_limit_bytes=VMEM_LIMIT_BYTES)


def _dot(a, b):
    return jnp.dot(a, b, preferred_element_type=F32)


def _dot_nt(a, b):
    return lax.dot_general(a, b, (((1,), (1,)), ((), ())), preferred_element_type=F32)


def _rmsnorm_rows(x, g):
    ms = jnp.mean(x * x, axis=-1, keepdims=True)
    return (x * lax.rsqrt(ms + NORM_EPS)) * g


def _silu(z):
    return z * jax.nn.sigmoid(z)


def _cast_kernel(w_ref, o_ref):
    o_ref[...] = w_ref[...].astype(o_ref.dtype)


def _cast_rows(w, row0, n_rows, tr, tc, name):
    cols = w.shape[1]
    assert row0 % SUBLANES == 0 and n_rows % tr == 0 and cols % tc == 0
    return pl.pallas_call(
        _cast_kernel,
        grid=(n_rows // tr, cols // tc),
        in_specs=[pl.BlockSpec((pl.Element(tr), pl.Element(tc)),
                               lambda r, c: (pl.multiple_of(row0 + r * tr, SUBLANES),
                                             pl.multiple_of(c * tc, LANES)))],
        out_specs=pl.BlockSpec((tr, tc), lambda r, c: (r, c)),
        out_shape=jax.ShapeDtypeStruct((n_rows, cols), BF16),
        compiler_params=_params("parallel", "parallel"),
        name=name,
    )(w)


def _rmsnorm_kernel(x_ref, g_ref, o_ref):
    o_ref[...] = _rmsnorm_rows(x_ref[...], g_ref[...]).astype(o_ref.dtype)


def _rmsnorm(x, g, out_dtype, tm):
    m, d = x.shape
    return pl.pallas_call(
        _rmsnorm_kernel,
        grid=(m // tm,),
        in_specs=[pl.BlockSpec((tm, d), lambda i: (i, 0)),
                  pl.BlockSpec((1, d), lambda i: (0, 0))],
        out_specs=pl.BlockSpec((tm, d), lambda i: (i, 0)),
        out_shape=jax.ShapeDtypeStruct((m, d), out_dtype),
        compiler_params=_params("parallel"),
        name="rmsnorm",
    )(x, g.reshape(1, d))


def _conv_group_kernel(xn_ref, wb_ref, wc_ref, wh_ref, wz_ref, cw_ref, y_ref, carry_ref,
                       *, tiles_per_seq):
    i = pl.program_id(0)
    j = pl.program_id(1)
    xn = xn_ref[...]
    u = _dot_nt(xn, wc_ref[...]) * _dot_nt(xn, wh_ref[...])
    tm, tc = u.shape

    @pl.when(i % tiles_per_seq == 0)
    def _():
        carry_ref[j] = jnp.zeros((SUBLANES, tc), F32)

    prev = carry_ref[j]
    carry_ref[j] = u[tm - SUBLANES:, :]

    row = lax.broadcasted_iota(jnp.int32, (SUBLANES, tc), 0)

    def shifted(k):
        body = pltpu.roll(u, k, 0)
        head = jnp.where(row < k, pltpu.roll(prev, k, 0), body[:SUBLANES, :])
        return jnp.concatenate([head, body[SUBLANES:, :]], axis=0)

    cw = cw_ref[...]
    conv = cw[0:1, :] * shifted(2) + cw[1:2, :] * shifted(1) + cw[2:3, :] * u
    y = (_dot_nt(xn, wb_ref[...]) * conv) * _silu(_dot_nt(xn, wz_ref[...]))
    y_ref[...] = y.astype(y_ref.dtype)


def _conv_group(xn, w_t, conv_w, seq, tm, tc):
    m, d = xn.shape
    dc = conv_w.shape[1]
    nj = dc // tc

    def w_spec(k):
        return pl.BlockSpec((tc, d), lambda i, j: (j + k * nj, 0))

    return pl.pallas_call(
        functools.partial(_conv_group_kernel, tiles_per_seq=seq // tm),
        grid=(m // tm, nj),
        in_specs=[pl.BlockSpec((tm, d), lambda i, j: (i, 0)),
                  w_spec(0), w_spec(1), w_spec(2), w_spec(3),
                  pl.BlockSpec((CONV_WIDTH, tc), lambda i, j: (0, j))],
        out_specs=pl.BlockSpec((tm, tc), lambda i, j: (i, j)),
        out_shape=jax.ShapeDtypeStruct((m, dc), BF16),
        scratch_shapes=[pltpu.VMEM((nj, SUBLANES, tc), F32)],
        compiler_params=_params("arbitrary", "arbitrary"),
        name="conv_group",
    )(xn, w_t, w_t, w_t, w_t, conv_w)


def _rope_pair(t, cos, sin):
    return t * cos + pltpu.roll(t, ROPE_DIM, 1) * sin


def _rope_low_half(t, cos, sin):
    lane = lax.broadcasted_iota(jnp.int32, t.shape, 1)
    partner = jnp.where(lane < ROPE_HALF,
                        -pltpu.roll(t, LANES - ROPE_HALF, 1),
                        pltpu.roll(t, ROPE_HALF, 1))
    return jnp.where(lane < ROPE_DIM, t * cos + partner * sin, 0.0)


def _latent_kernel(xn_ref, wq_ref, wkv_ref, wkr_ref, gq_ref, gkv_ref, cos_ref, sin_ref,
                   cq_ref, ckv_ref, kpe_ref):
    xn = xn_ref[...]
    cq_ref[...] = _rmsnorm_rows(_dot_nt(xn, wq_ref[...]), gq_ref[...]).astype(cq_ref.dtype)
    ckv_ref[...] = _rmsnorm_rows(_dot_nt(xn, wkv_ref[...]), gkv_ref[...]).astype(ckv_ref.dtype)
    kr = _dot_nt(xn, wkr_ref[...])
    kpe_ref[...] = _rope_low_half(kr, cos_ref[...], sin_ref[...]).astype(kpe_ref.dtype)


def _latents(xn, w_t, col0, gq, gkv, cos, sin, seq, tm):
    m, d = xn.shape
    tps = seq // tm
    assert col0 % Q_LORA == 0 and (col0 + Q_LORA) % KV_LORA == 0
    assert (col0 + Q_LORA + KV_LORA) % LANES == 0
    return pl.pallas_call(
        _latent_kernel,
        grid=(m // tm,),
        in_specs=[pl.BlockSpec((tm, d), lambda i: (i, 0)),
                  pl.BlockSpec((Q_LORA, d), lambda i: (col0 // Q_LORA, 0)),
                  pl.BlockSpec((KV_LORA, d), lambda i: ((col0 + Q_LORA) // KV_LORA, 0)),
                  pl.BlockSpec((LANES, d), lambda i: ((col0 + Q_LORA + KV_LORA) // LANES, 0)),
                  pl.BlockSpec((1, Q_LORA), lambda i: (0, 0)),
                  pl.BlockSpec((1, KV_LORA), lambda i: (0, 0)),
                  pl.BlockSpec((tm, LANES), lambda i: (i % tps, 0)),
                  pl.BlockSpec((tm, LANES), lambda i: (i % tps, 0))],
        out_specs=[pl.BlockSpec((tm, Q_LORA), lambda i: (i, 0)),
                   pl.BlockSpec((tm, KV_LORA), lambda i: (i, 0)),
                   pl.BlockSpec((tm, LANES), lambda i: (i, 0))],
        out_shape=[jax.ShapeDtypeStruct((m, Q_LORA), BF16),
                   jax.ShapeDtypeStruct((m, KV_LORA), BF16),
                   jax.ShapeDtypeStruct((m, LANES), BF16)],
        compiler_params=_params("parallel"),
        name="latents",
    )(xn, w_t, w_t, w_t, gq.reshape(1, -1), gkv.reshape(1, -1), cos, sin)


def _gate_kernel(xn_ref, w_ref, o_ref):
    o_ref[...] = _silu(_dot_nt(xn_ref[...], w_ref[...])).astype(o_ref.dtype)


def _gate(xn, w_z_t, tm, tn):
    m, d = xn.shape
    n = w_z_t.shape[0]
    return pl.pallas_call(
        _gate_kernel,
        grid=(m // tm, n // tn),
        in_specs=[pl.BlockSpec((tm, d), lambda i, j: (i, 0)),
                  pl.BlockSpec((tn, d), lambda i, j: (j, 0))],
        out_specs=pl.BlockSpec((tm, tn), lambda i, j: (i, j)),
        out_shape=jax.ShapeDtypeStruct((m, n), BF16),
        compiler_params=_params("parallel", "parallel"),
        name="attn_gate",
    )(xn, w_z_t)


def _q_up_kernel(cq_ref, w_ref, cos_ref, sin_ref, q_ref):
    r = _dot(cq_ref[...], w_ref[...])
    cos = cos_ref[...]
    sin = sin_ref[...]
    for h in range(N_HEADS):
        lo = h * QK_PAD_DIM
        nope = r[:, lo:lo + LANES] * Q_SCALE
        pe = _rope_pair(r[:, lo + LANES:lo + QK_PAD_DIM], cos, sin) * Q_SCALE
        q_ref[:, lo:lo + LANES] = nope.astype(q_ref.dtype)
        q_ref[:, lo + LANES:lo + QK_PAD_DIM] = pe.astype(q_ref.dtype)


def _q_up(cq, w_q, cos, sin, seq, tm):
    m, k = cq.shape
    n = w_q.shape[1]
    tps = seq // tm
    return pl.pallas_call(
        _q_up_kernel,
        grid=(m // tm,),
        in_specs=[pl.BlockSpec((tm, k), lambda i: (i, 0)),
                  pl.BlockSpec((k, n), lambda i: (0, 0)),
                  pl.BlockSpec((tm, LANES), lambda i: (i % tps, 0)),
                  pl.BlockSpec((tm, LANES), lambda i: (i % tps, 0))],
        out_specs=pl.BlockSpec((tm, n), lambda i: (i, 0)),
        out_shape=jax.ShapeDtypeStruct((m, n), BF16),
        compiler_params=_params("parallel"),
        name="q_up",
    )(cq, w_q, cos, sin)


def _kv_up_kernel(ckv_ref, wk_ref, wvt_ref, k_ref, vt_ref):
    c = ckv_ref[...]
    k_ref[...] = _dot(c, wk_ref[...]).astype(k_ref.dtype)
    vt_ref[...] = _dot_nt(wvt_ref[...], c).astype(vt_ref.dtype)


def _kv_up(ckv, w_k, w_vt, tk):
    m, k = ckv.shape
    n = w_k.shape[1]
    return pl.pallas_call(
        _kv_up_kernel,
        grid=(m // tk,),
        in_specs=[pl.BlockSpec((tk, k), lambda i: (i, 0)),
                  pl.BlockSpec((k, n), lambda i: (0, 0)),
                  pl.BlockSpec((n, k), lambda i: (0, 0))],
        out_specs=[pl.BlockSpec((tk, n), lambda i: (i, 0)),
                   pl.BlockSpec((None, n, tk), lambda i: (i, 0, 0))],
        out_shape=[jax.ShapeDtypeStruct((m, n), BF16),
                   jax.ShapeDtypeStruct((m // tk, n, tk), BF16)],
        compiler_params=_params("parallel"),
        name="kv_up",
    )(ckv, w_k, w_vt)


CAST_PIECE_ROWS = 16


def _attention_kernel(q_ref, kn_ref, kpe_ref, vt_ref, gate_ref, *rest, blk, n_casts):
    cast_in = rest[:n_casts]
    o_ref = rest[n_casts]
    cast_out = rest[n_casts + 1:2 * n_casts + 1]
    kfull_ref = rest[2 * n_casts + 1]
    cast_pieces = [(src, dst, r) for src, dst in zip(cast_in, cast_out)
                   for r in range(0, src.shape[0], CAST_PIECE_ROWS)]
    seq = q_ref.shape[0]
    nblk = seq // blk
    kfull_ref[:, :LANES] = kn_ref[...]
    kfull_ref[:, LANES:] = kpe_ref[...]

    key_chunk = lax.broadcasted_iota(jnp.int32, (blk, blk), 0) // CHUNK
    qry_chunk = lax.broadcasted_iota(jnp.int32, (blk, blk), 1) // CHUNK
    visible = key_chunk <= qry_chunk

    pairs = [(qi, kj) for qi in range(nblk) for kj in range(qi + 1)]

    def scores(qi, kj):
        s = _dot_nt(kfull_ref[kj * blk:(kj + 1) * blk, :],
                    q_ref[qi * blk:(qi + 1) * blk, :])
        return jnp.where(visible, s, -jnp.inf) if kj == qi else s

    m_run = l_run = acc = None
    ahead = [scores(*pairs[n]) for n in range(min(SCORE_LOOKAHEAD, len(pairs)))]
    for n, (qi, kj) in enumerate(pairs):
        s = ahead.pop(0)
        if n + SCORE_LOOKAHEAD < len(pairs):
            ahead.append(scores(*pairs[n + SCORE_LOOKAHEAD]))
        m_blk = jnp.max(s, axis=0, keepdims=True)
        if kj == 0:
            m_run = m_blk
            p = jnp.exp2(s - m_run)
            l_run = jnp.sum(p, axis=0, keepdims=True)
            acc = _dot(vt_ref[kj], p.astype(BF16))
        else:
            m_new = jnp.maximum(m_run, m_blk)
            alpha = jnp.exp2(m_run - m_new)
            p = jnp.exp2(s - m_new)
            l_run = alpha * l_run + jnp.sum(p, axis=0, keepdims=True)
            acc = alpha * acc + _dot(vt_ref[kj], p.astype(BF16))
            m_run = m_new
        if kj == qi:
            out = (acc / l_run).T
            gate = gate_ref[qi * blk:(qi + 1) * blk, :].astype(F32)
            o_ref[qi * blk:(qi + 1) * blk, :] = (out * gate).astype(o_ref.dtype)
        lo = n * len(cast_pieces) // len(pairs)
        hi = (n + 1) * len(cast_pieces) // len(pairs)
        for src, dst, r in cast_pieces[lo:hi]:
            dst[r:r + CAST_PIECE_ROWS, :] = src[r:r + CAST_PIECE_ROWS, :].astype(dst.dtype)


def _attention(q, k_nope, k_pe, v_t, gate, batch, seq, blk, casts):
    m = q.shape[0]
    nblk = seq // blk
    steps = batch * N_HEADS
    cast_specs, cast_shapes = [], []
    for w, n_rows in casts:
        rows = n_rows // steps
        assert rows * steps == n_rows and rows % CAST_PIECE_ROWS == 0
        cast_specs.append(pl.BlockSpec((rows, w.shape[1]), lambda b, h: (b * N_HEADS + h, 0)))
        cast_shapes.append(jax.ShapeDtypeStruct((n_rows, w.shape[1]), BF16))
    return pl.pallas_call(
        functools.partial(_attention_kernel, blk=blk, n_casts=len(casts)),
        grid=(batch, N_HEADS),
        in_specs=[pl.BlockSpec((seq, QK_PAD_DIM), lambda b, h: (b, h)),
                  pl.BlockSpec((seq, LANES), lambda b, h: (b, h)),
                  pl.BlockSpec((seq, LANES), lambda b, h: (b, 0)),
                  pl.BlockSpec((nblk, V_HEAD_DIM, blk), lambda b, h: (b, h, 0)),
                  pl.BlockSpec((seq, LANES), lambda b, h: (b, h))] + cast_specs,
        out_specs=[pl.BlockSpec((seq, V_HEAD_DIM), lambda b, h: (b, h))] + cast_specs,
        out_shape=[jax.ShapeDtypeStruct((m, N_HEADS * V_HEAD_DIM), BF16)] + cast_shapes,
        scratch_shapes=[pltpu.VMEM((seq, QK_PAD_DIM), BF16)],
        compiler_params=_params("parallel", "parallel"),
        name="attention",
    )(q, k_nope, k_pe, v_t, gate, *[w for w, _ in casts])


def _out_proj_kernel(yc_ref, ya_ref, w_ref, x_ref, g_ref, o_ref, *, final_norm, tn):
    kc = yc_ref.shape[1]
    d = o_ref.shape[1]
    yc = yc_ref[...]
    ya = ya_ref[...]
    ss = None
    for t in range(d // tn):
        cols = slice(t * tn, (t + 1) * tn)
        h = x_ref[:, cols] + (_dot(yc, w_ref[:kc, cols]) + _dot(ya, w_ref[kc:, cols]))
        o_ref[:, cols] = h.astype(o_ref.dtype)
        if final_norm:
            part = jnp.sum(h * h, axis=-1, keepdims=True)
            ss = part if ss is None else ss + part
    if final_norm:
        inv = lax.rsqrt(ss / d + NORM_EPS)
        for t in range(d // tn):
            cols = slice(t * tn, (t + 1) * tn)
            o_ref[:, cols] = (o_ref[:, cols] * inv) * g_ref[:, cols]


def _out_proj(y_conv, y_attn, w_o, x, g, final_norm, tm, tn):
    m, d = x.shape
    kc = y_conv.shape[1]
    ka = y_attn.shape[1]
    return pl.pallas_call(
        functools.partial(_out_proj_kernel, final_norm=final_norm, tn=tn),
        grid=(m // tm,),
        in_specs=[pl.BlockSpec((tm, kc), lambda i: (i, 0)),
                  pl.BlockSpec((tm, ka), lambda i: (i, 0)),
                  pl.BlockSpec((kc + ka, d), lambda i: (0, 0), pipeline_mode=pl.Buffered(1)),
                  pl.BlockSpec((tm, d), lambda i: (i, 0)),
                  pl.BlockSpec((1, d), lambda i: (0, 0))],
        out_specs=pl.BlockSpec((tm, d), lambda i: (i, 0)),
        out_shape=jax.ShapeDtypeStruct((m, d), x.dtype),
        compiler_params=pltpu.CompilerParams(dimension_semantics=("parallel",),
                                             vmem_limit_bytes=OUT_PROJ_VMEM_LIMIT_BYTES),
        name="out_proj",
    )(y_conv, y_attn, w_o, x, g.reshape(1, d))


def _swap_halves_signed(w):
    return jnp.concatenate([-w[..., ROPE_HALF:], w[..., :ROPE_HALF]], axis=-1)


def _rope_tables(seq):
    pos = jnp.arange(seq, dtype=F32)
    inv_freq = 1.0 / (ROPE_THETA ** (jnp.arange(0, ROPE_DIM, 2, dtype=F32) / ROPE_DIM))
    ang = pos[:, None] * inv_freq[None, :]
    zeros = jnp.zeros((seq, LANES - ROPE_DIM), F32)
    cos = jnp.concatenate([jnp.cos(ang), jnp.cos(ang), zeros], axis=-1)
    sin = jnp.concatenate([jnp.sin(ang), jnp.sin(ang), zeros], axis=-1)
    return cos, sin


def _tile(n, pref):
    return pref if n % pref == 0 else n


def _layer(h, g_in, w_in, conv_w, q_norm_g, w_uq, kv_norm_g, w_ukv, w_out, g_out,
           final_norm, cos, sin, batch, seq):
    m, d = h.shape
    dc = conv_w.shape[1]
    o = 4 * dc

    w_in_t = w_in.T
    kr0 = o + Q_LORA + KV_LORA
    n_lat = Q_LORA + KV_LORA + LANES
    w_lat_t = _cast_rows(w_in_t, o, n_lat, n_lat // 2, d // 2, "cast_w_latent")
    w_z_t = _cast_rows(w_in_t, kr0 + ROPE_DIM, w_in_t.shape[0] - kr0 - ROPE_DIM, 512, d // 2,
                       "cast_w_gate")
    wq = w_uq.reshape(Q_LORA, N_HEADS, QK_NOPE_DIM + ROPE_DIM)
    wq_pe = wq[..., QK_NOPE_DIM:]
    w_q = jnp.concatenate([wq[..., :QK_NOPE_DIM], wq_pe, _swap_halves_signed(wq_pe)],
                          axis=-1).reshape(Q_LORA, N_HEADS * QK_PAD_DIM).astype(BF16)
    wkv = w_ukv.reshape(KV_LORA, N_HEADS, QK_NOPE_DIM + V_HEAD_DIM)
    w_k = wkv[..., :QK_NOPE_DIM].reshape(KV_LORA, N_HEADS * QK_NOPE_DIM).astype(BF16)
    w_vt = wkv[..., QK_NOPE_DIM:].reshape(KV_LORA, N_HEADS * V_HEAD_DIM).T.astype(BF16)

    blk = _tile(seq, 512)
    xn = _rmsnorm(h, g_in, BF16, _tile(m, 256))
    cq, ckv, k_pe = _latents(xn, w_lat_t, 0, q_norm_g, kv_norm_g, cos, sin, seq, _tile(seq, 512))
    gate = _gate(xn, w_z_t, _tile(m, 1024), _tile(w_z_t.shape[0], 1024))
    q = _q_up(cq, w_q, cos, sin, seq, _tile(seq, 512))
    k_nope, v_t = _kv_up(ckv, w_k, w_vt, blk)
    y_attn, w_conv_t, w_o = _attention(q, k_nope, k_pe, v_t, gate, batch, seq, blk,
                                       [(w_in_t, o), (w_out, w_out.shape[0])])
    y_conv = _conv_group(xn, w_conv_t, conv_w, seq, _tile(seq, 1024), _tile(dc, 256))
    return _out_proj(y_conv, y_attn, w_o, h, g_out, final_norm, _tile(m, 256), _tile(d, 512))


def kernel(x, g_in, w_in, conv_w, q_norm_g, w_uq, kv_norm_g, w_ukv, w_out, g_final):
    batch, seq, d = x.shape
    depth = g_in.shape[0]
    cos, sin = _rope_tables(seq)
    h = x.reshape(batch * seq, d)
    for l in range(depth):
        h = _layer(h, g_in[l], w_in[l], conv_w[l], q_norm_g[l], w_uq[l], kv_norm_g[l],
                   w_ukv[l], w_out[l], g_final, l == depth - 1, cos, sin, batch, seq)
    return h.reshape(batch, seq, d)
```

```python
import functools

import jax
import jax.numpy as jnp
import numpy as np
from jax import lax
from jax.experimental import pallas as pl
from jax.experimental.pallas import tpu as pltpu

CHUNK = 64
CONV_WIDTH = 3
N_HEADS = 16
V_HEAD_DIM = 128
QK_NOPE_DIM = 128
ROPE_DIM = 64
ROPE_HALF = ROPE_DIM // 2
Q_LORA = 1024
KV_LORA = 512
ROPE_THETA = 10000.0
NORM_EPS = 1e-6
ATTN_SCALE = (QK_NOPE_DIM + ROPE_DIM) ** -0.5
LOG2_E = float(np.log2(np.e))
Q_SCALE = ATTN_SCALE * LOG2_E

LANES = 128
SUBLANES = 8
QK_PAD_DIM = 2 * LANES
SCORE_LOOKAHEAD = 2
VMEM_LIMIT_BYTES = 56 * 1024 * 1024
BIG_VMEM_LIMIT_BYTES = 60 * 1024 * 1024

BF16 = jnp.bfloat16
F32 = jnp.float32


def _params(*semantics):
    return pltpu.CompilerParams(dimension_semantics=semantics,
                                vmem_limit_bytes=VMEM_LIMIT_BYTES)


def _dot(a, b):
    return jnp.dot(a, b, preferred_element_type=F32)


def _dot_nt(a, b):
    return lax.dot_general(a, b, (((1,), (1,)), ((), ())), preferred_element_type=F32)


def _rmsnorm_rows(x, g):
    ms = jnp.mean(x * x, axis=-1, keepdims=True)
    return (x * lax.rsqrt(ms + NORM_EPS)) * g


def _silu(z):
    return z * jax.nn.sigmoid(z)


def _cast_kernel(w_ref, o_ref):
    o_ref[...] = w_ref[...].astype(o_ref.dtype)


def _cast_rows(w, row0, n_rows, tr, tc, name):
    cols = w.shape[1]
    assert row0 % SUBLANES == 0 and n_rows % tr == 0 and cols % tc == 0
    return pl.pallas_call(
        _cast_kernel,
        grid=(n_rows // tr, cols // tc),
        in_specs=[pl.BlockSpec((pl.Element(tr), pl.Element(tc)),
                               lambda r, c: (pl.multiple_of(row0 + r * tr, SUBLANES),
                                             pl.multiple_of(c * tc, LANES)))],
        out_specs=pl.BlockSpec((tr, tc), lambda r, c: (r, c)),
        out_shape=jax.ShapeDtypeStruct((n_rows, cols), BF16),
        compiler_params=_params("parallel", "parallel"),
        name=name,
    )(w)


def _conv_group_kernel(xn_ref, wb_ref, wc_ref, wh_ref, wz_ref, cw_ref, y_ref, carry_ref,
                       *, tiles_per_seq):
    i = pl.program_id(0)
    j = pl.program_id(1)
    xn = xn_ref[...]
    u = _dot_nt(xn, wc_ref[...]) * _dot_nt(xn, wh_ref[...])
    tm, tc = u.shape

    @pl.when(i % tiles_per_seq == 0)
    def _():
        carry_ref[j] = jnp.zeros((SUBLANES, tc), F32)

    prev = carry_ref[j]
    carry_ref[j] = u[tm - SUBLANES:, :]

    row = lax.broadcasted_iota(jnp.int32, (SUBLANES, tc), 0)

    def shifted(k):
        body = pltpu.roll(u, k, 0)
        head = jnp.where(row < k, pltpu.roll(prev, k, 0), body[:SUBLANES, :])
        return jnp.concatenate([head, body[SUBLANES:, :]], axis=0)

    cw = cw_ref[...]
    conv = cw[0:1, :] * shifted(2) + cw[1:2, :] * shifted(1) + cw[2:3, :] * u
    y = (_dot_nt(xn, wb_ref[...]) * conv) * _silu(_dot_nt(xn, wz_ref[...]))
    y_ref[...] = y.astype(y_ref.dtype)


def _conv_group(xn, w_t, conv_w, seq, tm, tc):
    m, d = xn.shape
    dc = conv_w.shape[1]
    nj = dc // tc

    def w_spec(k):
        return pl.BlockSpec((tc, d), lambda i, j: (j + k * nj, 0))

    return pl.pallas_call(
        functools.partial(_conv_group_kernel, tiles_per_seq=seq // tm),
        grid=(m // tm, nj),
        in_specs=[pl.BlockSpec((tm, d), lambda i, j: (i, 0)),
                  w_spec(0), w_spec(1), w_spec(2), w_spec(3),
                  pl.BlockSpec((CONV_WIDTH, tc), lambda i, j: (0, j))],
        out_specs=pl.BlockSpec((tm, tc), lambda i, j: (i, j)),
        out_shape=jax.ShapeDtypeStruct((m, dc), BF16),
        scratch_shapes=[pltpu.VMEM((nj, SUBLANES, tc), F32)],
        compiler_params=_params("arbitrary", "arbitrary"),
        name="conv_group",
    )(xn, w_t, w_t, w_t, w_t, conv_w)


def _rope_pair(t, cos, sin):
    return t * cos + pltpu.roll(t, ROPE_DIM, 1) * sin


def _rope_low_half(t, cos, sin):
    lane = lax.broadcasted_iota(jnp.int32, t.shape, 1)
    partner = jnp.where(lane < ROPE_HALF,
                        -pltpu.roll(t, LANES - ROPE_HALF, 1),
                        pltpu.roll(t, ROPE_HALF, 1))
    return jnp.where(lane < ROPE_DIM, t * cos + partner * sin, 0.0)


def _latent_kernel(x_ref, g_ref, wq_ref, wkv_ref, wkr_ref, gq_ref, gkv_ref, cos_ref, sin_ref,
                   xn_ref, cq_ref, ckv_ref, kpe_ref):
    x = x_ref[...]
    g = g_ref[...]
    inv = lax.rsqrt(jnp.mean(x * x, axis=-1, keepdims=True) + NORM_EPS)
    xn_ref[...] = ((x * inv) * g).astype(xn_ref.dtype)
    lhs = (x * g).astype(BF16)
    cq = _dot_nt(lhs, wq_ref[...]) * inv
    cq_ref[...] = _rmsnorm_rows(cq, gq_ref[...]).astype(cq_ref.dtype)
    ckv = _dot_nt(lhs, wkv_ref[...]) * inv
    ckv_ref[...] = _rmsnorm_rows(ckv, gkv_ref[...]).astype(ckv_ref.dtype)
    kr = _dot_nt(lhs, wkr_ref[...]) * inv
    kpe_ref[...] = _rope_low_half(kr, cos_ref[...], sin_ref[...]).astype(kpe_ref.dtype)


def _latents(x, g_in, w_t, col0, gq, gkv, cos, sin, seq, tm):
    m, d = x.shape
    tps = seq // tm
    assert col0 % Q_LORA == 0 and (col0 + Q_LORA) % KV_LORA == 0
    assert (col0 + Q_LORA + KV_LORA) % LANES == 0

    def w_spec(rows, row0):
        return pl.BlockSpec((rows, d), lambda i: (row0 // rows, 0), pipeline_mode=pl.Buffered(1))

    return pl.pallas_call(
        _latent_kernel,
        grid=(m // tm,),
        in_specs=[pl.BlockSpec((tm, d), lambda i: (i, 0)),
                  pl.BlockSpec((1, d), lambda i: (0, 0)),
                  w_spec(Q_LORA, col0),
                  w_spec(KV_LORA, col0 + Q_LORA),
                  w_spec(LANES, col0 + Q_LORA + KV_LORA),
                  pl.BlockSpec((1, Q_LORA), lambda i: (0, 0)),
                  pl.BlockSpec((1, KV_LORA), lambda i: (0, 0)),
                  pl.BlockSpec((tm, LANES), lambda i: (i % tps, 0)),
                  pl.BlockSpec((tm, LANES), lambda i: (i % tps, 0))],
        out_specs=[pl.BlockSpec((tm, d), lambda i: (i, 0)),
                   pl.BlockSpec((tm, Q_LORA), lambda i: (i, 0)),
                   pl.BlockSpec((tm, KV_LORA), lambda i: (i, 0)),
                   pl.BlockSpec((tm, LANES), lambda i: (i, 0))],
        out_shape=[jax.ShapeDtypeStruct((m, d), BF16),
                   jax.ShapeDtypeStruct((m, Q_LORA), BF16),
                   jax.ShapeDtypeStruct((m, KV_LORA), BF16),
                   jax.ShapeDtypeStruct((m, LANES), BF16)],
        compiler_params=pltpu.CompilerParams(dimension_semantics=("parallel",),
                                             vmem_limit_bytes=BIG_VMEM_LIMIT_BYTES),
        name="latents",
    )(x, g_in.reshape(1, d), w_t, w_t, w_t, gq.reshape(1, -1), gkv.reshape(1, -1), cos, sin)


def _gate_kernel(xn_ref, w_ref, o_ref):
    o_ref[...] = _silu(_dot_nt(xn_ref[...], w_ref[...])).astype(o_ref.dtype)


def _gate(xn, w_z_t, tm, tn):
    m, d = xn.shape
    n = w_z_t.shape[0]
    return pl.pallas_call(
        _gate_kernel,
        grid=(m // tm, n // tn),
        in_specs=[pl.BlockSpec((tm, d), lambda i, j: (i, 0)),
                  pl.BlockSpec((tn, d), lambda i, j: (j, 0))],
        out_specs=pl.BlockSpec((tm, tn), lambda i, j: (i, j)),
        out_shape=jax.ShapeDtypeStruct((m, n), BF16),
        compiler_params=_params("parallel", "parallel"),
        name="attn_gate",
    )(xn, w_z_t)


def _q_up_kernel(cq_ref, w_ref, cos_ref, sin_ref, q_ref):
    r = _dot(cq_ref[...], w_ref[...])
    cos = cos_ref[...]
    sin = sin_ref[...]
    for h in range(N_HEADS):
        lo = h * QK_PAD_DIM
        nope = r[:, lo:lo + LANES] * Q_SCALE
        pe = _rope_pair(r[:, lo + LANES:lo + QK_PAD_DIM], cos, sin) * Q_SCALE
        q_ref[:, lo:lo + LANES] = nope.astype(q_ref.dtype)
        q_ref[:, lo + LANES:lo + QK_PAD_DIM] = pe.astype(q_ref.dtype)


def _q_up(cq, w_q, cos, sin, seq, tm):
    m, k = cq.shape
    n = w_q.shape[1]
    tps = seq // tm
    return pl.pallas_call(
        _q_up_kernel,
        grid=(m // tm,),
        in_specs=[pl.BlockSpec((tm, k), lambda i: (i, 0)),
                  pl.BlockSpec((k, n), lambda i: (0, 0)),
                  pl.BlockSpec((tm, LANES), lambda i: (i % tps, 0)),
                  pl.BlockSpec((tm, LANES), lambda i: (i % tps, 0))],
        out_specs=pl.BlockSpec((tm, n), lambda i: (i, 0)),
        out_shape=jax.ShapeDtypeStruct((m, n), BF16),
        compiler_params=_params("parallel"),
        name="q_up",
    )(cq, w_q, cos, sin)


def _kv_up_kernel(ckv_ref, wk_ref, wvt_ref, k_ref, vt_ref):
    c = ckv_ref[...]
    k_ref[...] = _dot(c, wk_ref[...]).astype(k_ref.dtype)
    vt_ref[...] = _dot_nt(wvt_ref[...], c).astype(vt_ref.dtype)


def _kv_up(ckv, w_k, w_vt, tk):
    m, k = ckv.shape
    n = w_k.shape[1]
    return pl.pallas_call(
        _kv_up_kernel,
        grid=(m // tk,),
        in_specs=[pl.BlockSpec((tk, k), lambda i: (i, 0)),
                  pl.BlockSpec((k, n), lambda i: (0, 0)),
                  pl.BlockSpec((n, k), lambda i: (0, 0))],
        out_specs=[pl.BlockSpec((tk, n), lambda i: (i, 0)),
                   pl.BlockSpec((None, n, tk), lambda i: (i, 0, 0))],
        out_shape=[jax.ShapeDtypeStruct((m, n), BF16),
                   jax.ShapeDtypeStruct((m // tk, n, tk), BF16)],
        compiler_params=_params("parallel"),
        name="kv_up",
    )(ckv, w_k, w_vt)


CAST_PIECE_ROWS = 16


def _attention_kernel(q_ref, kn_ref, kpe_ref, vt_ref, gate_ref, *rest, blk, n_casts):
    cast_in = rest[:n_casts]
    o_ref = rest[n_casts]
    cast_out = rest[n_casts + 1:2 * n_casts + 1]
    kfull_ref = rest[2 * n_casts + 1]
    cast_pieces = [(src, dst, r) for src, dst in zip(cast_in, cast_out)
                   for r in range(0, src.shape[0], CAST_PIECE_ROWS)]
    seq = q_ref.shape[0]
    nblk = seq // blk
    kfull_ref[:, :LANES] = kn_ref[...]
    kfull_ref[:, LANES:] = kpe_ref[...]

    key_chunk = lax.broadcasted_iota(jnp.int32, (blk, blk), 0) // CHUNK
    qry_chunk = lax.broadcasted_iota(jnp.int32, (blk, blk), 1) // CHUNK
    visible = key_chunk <= qry_chunk

    pairs = [(qi, kj) for qi in range(nblk) for kj in range(qi + 1)]

    def scores(qi, kj):
        s = _dot_nt(kfull_ref[kj * blk:(kj + 1) * blk, :],
                    q_ref[qi * blk:(qi + 1) * blk, :])
        return jnp.where(visible, s, -jnp.inf) if kj == qi else s

    m_run = l_run = acc = None
    ahead = [scores(*pairs[n]) for n in range(min(SCORE_LOOKAHEAD, len(pairs)))]
    for n, (qi, kj) in enumerate(pairs):
        s = ahead.pop(0)
        if n + SCORE_LOOKAHEAD < len(pairs):
            ahead.append(scores(*pairs[n + SCORE_LOOKAHEAD]))
        m_blk = jnp.max(s, axis=0, keepdims=True)
        if kj == 0:
            m_run = m_blk
            p = jnp.exp2(s - m_run)
            l_run = jnp.sum(p, axis=0, keepdims=True)
            acc = _dot(vt_ref[kj], p.astype(BF16))
        else:
            m_new = jnp.maximum(m_run, m_blk)
            alpha = jnp.exp2(m_run - m_new)
            p = jnp.exp2(s - m_new)
            l_run = alpha * l_run + jnp.sum(p, axis=0, keepdims=True)
            acc = alpha * acc + _dot(vt_ref[kj], p.astype(BF16))
            m_run = m_new
        if kj == qi:
            out = (acc / l_run).T
            gate = gate_ref[qi * blk:(qi + 1) * blk, :].astype(F32)
            o_ref[qi * blk:(qi + 1) * blk, :] = (out * gate).astype(o_ref.dtype)
        lo = n * len(cast_pieces) // len(pairs)
        hi = (n + 1) * len(cast_pieces) // len(pairs)
        for src, dst, r in cast_pieces[lo:hi]:
            dst[r:r + CAST_PIECE_ROWS, :] = src[r:r + CAST_PIECE_ROWS, :].astype(dst.dtype)


def _attention(q, k_nope, k_pe, v_t, gate, batch, seq, blk, casts):
    m = q.shape[0]
    nblk = seq // blk
    steps = batch * N_HEADS
    cast_specs, cast_shapes = [], []
    for w, n_rows in casts:
        rows = n_rows // steps
        assert rows * steps == n_rows and rows % CAST_PIECE_ROWS == 0
        cast_specs.append(pl.BlockSpec((rows, w.shape[1]), lambda b, h: (b * N_HEADS + h, 0)))
        cast_shapes.append(jax.ShapeDtypeStruct((n_rows, w.shape[1]), BF16))
    return pl.pallas_call(
        functools.partial(_attention_kernel, blk=blk, n_casts=len(casts)),
        grid=(batch, N_HEADS),
        in_specs=[pl.BlockSpec((seq, QK_PAD_DIM), lambda b, h: (b, h)),
                  pl.BlockSpec((seq, LANES), lambda b, h: (b, h)),
                  pl.BlockSpec((seq, LANES), lambda b, h: (b, 0)),
                  pl.BlockSpec((nblk, V_HEAD_DIM, blk), lambda b, h: (b, h, 0)),
                  pl.BlockSpec((seq, LANES), lambda b, h: (b, h))] + cast_specs,
        out_specs=[pl.BlockSpec((seq, V_HEAD_DIM), lambda b, h: (b, h))] + cast_specs,
        out_shape=[jax.ShapeDtypeStruct((m, N_HEADS * V_HEAD_DIM), BF16)] + cast_shapes,
        scratch_shapes=[pltpu.VMEM((seq, QK_PAD_DIM), BF16)],
        compiler_params=_params("parallel", "parallel"),
        name="attention",
    )(q, k_nope, k_pe, v_t, gate, *[w for w, _ in casts])


def _out_proj_kernel(yc_ref, ya_ref, w_ref, x_ref, g_ref, o_ref, *, final_norm, tn):
    kc = yc_ref.shape[1]
    d = o_ref.shape[1]
    yc = yc_ref[...]
    ya = ya_ref[...]
    ss = None
    for t in range(d // tn):
        cols = slice(t * tn, (t + 1) * tn)
        h = x_ref[:, cols] + (_dot(yc, w_ref[:kc, cols]) + _dot(ya, w_ref[kc:, cols]))
        o_ref[:, cols] = h.astype(o_ref.dtype)
        if final_norm:
            part = jnp.sum(h * h, axis=-1, keepdims=True)
            ss = part if ss is None else ss + part
    if final_norm:
        inv = lax.rsqrt(ss / d + NORM_EPS)
        for t in range(d // tn):
            cols = slice(t * tn, (t + 1) * tn)
            o_ref[:, cols] = (o_ref[:, cols] * inv) * g_ref[:, cols]


def _out_proj(y_conv, y_attn, w_o, x, g, final_norm, tm, tn):
    m, d = x.shape
    kc = y_conv.shape[1]
    ka = y_attn.shape[1]
    return pl.pallas_call(
        functools.partial(_out_proj_kernel, final_norm=final_norm, tn=tn),
        grid=(m // tm,),
        in_specs=[pl.BlockSpec((tm, kc), lambda i: (i, 0)),
                  pl.BlockSpec((tm, ka), lambda i: (i, 0)),
                  pl.BlockSpec((kc + ka, d), lambda i: (0, 0), pipeline_mode=pl.Buffered(1)),
                  pl.BlockSpec((tm, d), lambda i: (i, 0)),
                  pl.BlockSpec((1, d), lambda i: (0, 0))],
        out_specs=pl.BlockSpec((tm, d), lambda i: (i, 0)),
        out_shape=jax.ShapeDtypeStruct((m, d), x.dtype),
        compiler_params=pltpu.CompilerParams(dimension_semantics=("parallel",),
                                             vmem_limit_bytes=BIG_VMEM_LIMIT_BYTES),
        name="out_proj",
    )(y_conv, y_attn, w_o, x, g.reshape(1, d))


def _swap_halves_signed(w):
    return jnp.concatenate([-w[..., ROPE_HALF:], w[..., :ROPE_HALF]], axis=-1)


def _rope_tables(seq):
    pos = jnp.arange(seq, dtype=F32)
    inv_freq = 1.0 / (ROPE_THETA ** (jnp.arange(0, ROPE_DIM, 2, dtype=F32) / ROPE_DIM))
    ang = pos[:, None] * inv_freq[None, :]
    zeros = jnp.zeros((seq, LANES - ROPE_DIM), F32)
    cos = jnp.concatenate([jnp.cos(ang), jnp.cos(ang), zeros], axis=-1)
    sin = jnp.concatenate([jnp.sin(ang), jnp.sin(ang), zeros], axis=-1)
    return cos, sin


def _tile(n, pref):
    return pref if n % pref == 0 else n


def _layer(h, g_in, w_in, conv_w, q_norm_g, w_uq, kv_norm_g, w_ukv, w_out, g_out,
           final_norm, cos, sin, batch, seq):
    m, d = h.shape
    dc = conv_w.shape[1]
    o = 4 * dc

    w_in_t = w_in.T
    kr0 = o + Q_LORA + KV_LORA
    n_lat = Q_LORA + KV_LORA + LANES
    w_lat_t = _cast_rows(w_in_t, o, n_lat, n_lat // 2, d // 2, "cast_w_latent")
    w_z_t = _cast_rows(w_in_t, kr0 + ROPE_DIM, w_in_t.shape[0] - kr0 - ROPE_DIM, 512, d // 2,
                       "cast_w_gate")
    wq = w_uq.reshape(Q_LORA, N_HEADS, QK_NOPE_DIM + ROPE_DIM)
    wq_pe = wq[..., QK_NOPE_DIM:]
    w_q = jnp.concatenate([wq[..., :QK_NOPE_DIM], wq_pe, _swap_halves_signed(wq_pe)],
                          axis=-1).reshape(Q_LORA, N_HEADS * QK_PAD_DIM).astype(BF16)
    wkv = w_ukv.reshape(KV_LORA, N_HEADS, QK_NOPE_DIM + V_HEAD_DIM)
    w_k = wkv[..., :QK_NOPE_DIM].reshape(KV_LORA, N_HEADS * QK_NOPE_DIM).astype(BF16)
    w_vt = wkv[..., QK_NOPE_DIM:].reshape(KV_LORA, N_HEADS * V_HEAD_DIM).T.astype(BF16)

    blk = _tile(seq, 512)
    xn, cq, ckv, k_pe = _latents(h, g_in, w_lat_t, 0, q_norm_g, kv_norm_g, cos, sin, seq,
                                 _tile(seq, 512))
    gate = _gate(xn, w_z_t, _tile(m, 1024), _tile(w_z_t.shape[0], 1024))
    q = _q_up(cq, w_q, cos, sin, seq, _tile(seq, 512))
    k_nope, v_t = _kv_up(ckv, w_k, w_vt, blk)
    y_attn, w_conv_t, w_o = _attention(q, k_nope, k_pe, v_t, gate, batch, seq, blk,
                                       [(w_in_t, o), (w_out, w_out.shape[0])])
    y_conv = _conv_group(xn, w_conv_t, conv_w, seq, _tile(seq, 1024), _tile(dc, 256))
    return _out_proj(y_conv, y_attn, w_o, h, g_out, final_norm, _tile(m, 256), _tile(d, 512))


def kernel(x, g_in, w_in, conv_w, q_norm_g, w_uq, kv_norm_g, w_ukv, w_out, g_final):
    batch, seq, d = x.shape
    depth = g_in.shape[0]
    cos, sin = _rope_tables(seq)
    h = x.reshape(batch * seq, d)
    for l in range(depth):
        h = _layer(h, g_in[l], w_in[l], conv_w[l], q_norm_g[l], w_uq[l], kv_norm_g[l],
                   w_ukv[l], w_out[l], g_final, l == depth - 1, cos, sin, batch, seq)
    return h.reshape(batch, seq, d)
```

```python
import functools

import jax
import jax.numpy as jnp
import numpy as np
from jax import lax
from jax.experimental import pallas as pl
from jax.experimental.pallas import tpu as pltpu

CHUNK = 64
CONV_WIDTH = 3
N_HEADS = 16
V_HEAD_DIM = 128
QK_NOPE_DIM = 128
ROPE_DIM = 64
ROPE_HALF = ROPE_DIM // 2
Q_LORA = 1024
KV_LORA = 512
ROPE_THETA = 10000.0
NORM_EPS = 1e-6
ATTN_SCALE = (QK_NOPE_DIM + ROPE_DIM) ** -0.5
LOG2_E = float(np.log2(np.e))
Q_SCALE = ATTN_SCALE * LOG2_E

LANES = 128
SUBLANES = 8
QK_PAD_DIM = 2 * LANES
SCORE_LOOKAHEAD = 2
VMEM_LIMIT_BYTES = 56 * 1024 * 1024
BIG_VMEM_LIMIT_BYTES = 60 * 1024 * 1024

BF16 = jnp.bfloat16
F32 = jnp.float32


def _params(*semantics):
    return pltpu.CompilerParams(dimension_semantics=semantics,
                                vmem_limit_bytes=VMEM_LIMIT_BYTES)


def _dot(a, b):
    return jnp.dot(a, b, preferred_element_type=F32)


def _dot_nt(a, b):
    return lax.dot_general(a, b, (((1,), (1,)), ((), ())), preferred_element_type=F32)


def _rmsnorm_rows(x, g):
    ms = jnp.mean(x * x, axis=-1, keepdims=True)
    return (x * lax.rsqrt(ms + NORM_EPS)) * g


def _silu(z):
    return z * jax.nn.sigmoid(z)


def _cast_kernel(w_ref, o_ref):
    o_ref[...] = w_ref[...].astype(o_ref.dtype)


def _cast_rows(w, row0, n_rows, tr, tc, name):
    cols = w.shape[1]
    assert row0 % SUBLANES == 0 and n_rows % tr == 0 and cols % tc == 0
    return pl.pallas_call(
        _cast_kernel,
        grid=(n_rows // tr, cols // tc),
        in_specs=[pl.BlockSpec((pl.Element(tr), pl.Element(tc)),
                               lambda r, c: (pl.multiple_of(row0 + r * tr, SUBLANES),
                                             pl.multiple_of(c * tc, LANES)))],
        out_specs=pl.BlockSpec((tr, tc), lambda r, c: (r, c)),
        out_shape=jax.ShapeDtypeStruct((n_rows, cols), BF16),
        compiler_params=_params("parallel", "parallel"),
        name=name,
    )(w)


def _conv_group_kernel(xn_ref, gb_ref, wc_ref, wh_ref, wz_ref, cw_ref, cast_in_ref,
                       y_ref, cast_out_ref, carry_ref, *, tiles_per_seq):
    i = pl.program_id(0)
    j = pl.program_id(1)
    cast_out_ref[...] = cast_in_ref[...].astype(cast_out_ref.dtype)
    xn = xn_ref[...]
    u = _dot_nt(xn, wc_ref[...]) * _dot_nt(xn, wh_ref[...])
    tm, tc = u.shape

    @pl.when(i % tiles_per_seq == 0)
    def _():
        carry_ref[j] = jnp.zeros((SUBLANES, tc), F32)

    prev = carry_ref[j]
    carry_ref[j] = u[tm - SUBLANES:, :]

    row = lax.broadcasted_iota(jnp.int32, (SUBLANES, tc), 0)

    def shifted(k):
        body = pltpu.roll(u, k, 0)
        head = jnp.where(row < k, pltpu.roll(prev, k, 0), body[:SUBLANES, :])
        return jnp.concatenate([head, body[SUBLANES:, :]], axis=0)

    cw = cw_ref[...]
    conv = cw[0:1, :] * shifted(2) + cw[1:2, :] * shifted(1) + cw[2:3, :] * u
    y = (gb_ref[...].astype(F32) * conv) * _silu(_dot_nt(xn, wz_ref[...]))
    y_ref[...] = y.astype(y_ref.dtype)


def _conv_group(xn, g_b, w_t, conv_w, w_cast, seq, tm, tc):
    m, d = xn.shape
    dc = conv_w.shape[1]
    nj = dc // tc
    cast_rows = w_cast.shape[0] // ((m // tm) * nj)
    assert cast_rows * (m // tm) * nj == w_cast.shape[0] and cast_rows % CAST_PIECE_ROWS == 0

    def w_spec(k):
        return pl.BlockSpec((tc, d), lambda i, j: (j + k * nj, 0))

    cast_spec = pl.BlockSpec((cast_rows, w_cast.shape[1]), lambda i, j: (i * nj + j, 0))
    return pl.pallas_call(
        functools.partial(_conv_group_kernel, tiles_per_seq=seq // tm),
        grid=(m // tm, nj),
        in_specs=[pl.BlockSpec((tm, d), lambda i, j: (i, 0)),
                  pl.BlockSpec((tm, tc), lambda i, j: (i, j)),
                  w_spec(0), w_spec(1), w_spec(2),
                  pl.BlockSpec((CONV_WIDTH, tc), lambda i, j: (0, j)),
                  cast_spec],
        out_specs=[pl.BlockSpec((tm, tc), lambda i, j: (i, j)), cast_spec],
        out_shape=[jax.ShapeDtypeStruct((m, dc), BF16),
                   jax.ShapeDtypeStruct(w_cast.shape, BF16)],
        scratch_shapes=[pltpu.VMEM((nj, SUBLANES, tc), F32)],
        compiler_params=_params("arbitrary", "arbitrary"),
        name="conv_group",
    )(xn, g_b, w_t, w_t, w_t, conv_w, w_cast)


def _rope_pair(t, cos, sin):
    return t * cos + pltpu.roll(t, ROPE_DIM, 1) * sin


def _rope_low_half(t, cos, sin):
    lane = lax.broadcasted_iota(jnp.int32, t.shape, 1)
    partner = jnp.where(lane < ROPE_HALF,
                        -pltpu.roll(t, LANES - ROPE_HALF, 1),
                        pltpu.roll(t, ROPE_HALF, 1))
    return jnp.where(lane < ROPE_DIM, t * cos + partner * sin, 0.0)


def _latent_kernel(x_ref, g_ref, wq_ref, wkv_ref, wkr_ref, gq_ref, gkv_ref, cos_ref, sin_ref,
                   xn_ref, cq_ref, ckv_ref, kpe_ref):
    x = x_ref[...]
    g = g_ref[...]
    inv = lax.rsqrt(jnp.mean(x * x, axis=-1, keepdims=True) + NORM_EPS)
    xn_ref[...] = ((x * inv) * g).astype(xn_ref.dtype)
    lhs = (x * g).astype(BF16)
    cq = _dot_nt(lhs, wq_ref[...]) * inv
    cq_ref[...] = _rmsnorm_rows(cq, gq_ref[...]).astype(cq_ref.dtype)
    ckv = _dot_nt(lhs, wkv_ref[...]) * inv
    ckv_ref[...] = _rmsnorm_rows(ckv, gkv_ref[...]).astype(ckv_ref.dtype)
    kr = _dot_nt(lhs, wkr_ref[...]) * inv
    kpe_ref[...] = _rope_low_half(kr, cos_ref[...], sin_ref[...]).astype(kpe_ref.dtype)


def _latents(x, g_in, w_t, col0, gq, gkv, cos, sin, seq, tm):
    m, d = x.shape
    tps = seq // tm
    assert col0 % Q_LORA == 0 and (col0 + Q_LORA) % KV_LORA == 0
    assert (col0 + Q_LORA + KV_LORA) % LANES == 0

    def w_spec(rows, row0):
        return pl.BlockSpec((rows, d), lambda i: (row0 // rows, 0), pipeline_mode=pl.Buffered(1))

    return pl.pallas_call(
        _latent_kernel,
        grid=(m // tm,),
        in_specs=[pl.BlockSpec((tm, d), lambda i: (i, 0)),
                  pl.BlockSpec((1, d), lambda i: (0, 0)),
                  w_spec(Q_LORA, col0),
                  w_spec(KV_LORA, col0 + Q_LORA),
                  w_spec(LANES, col0 + Q_LORA + KV_LORA),
                  pl.BlockSpec((1, Q_LORA), lambda i: (0, 0)),
                  pl.BlockSpec((1, KV_LORA), lambda i: (0, 0)),
                  pl.BlockSpec((tm, LANES), lambda i: (i % tps, 0)),
                  pl.BlockSpec((tm, LANES), lambda i: (i % tps, 0))],
        out_specs=[pl.BlockSpec((tm, d), lambda i: (i, 0)),
                   pl.BlockSpec((tm, Q_LORA), lambda i: (i, 0)),
                   pl.BlockSpec((tm, KV_LORA), lambda i: (i, 0)),
                   pl.BlockSpec((tm, LANES), lambda i: (i, 0))],
        out_shape=[jax.ShapeDtypeStruct((m, d), BF16),
                   jax.ShapeDtypeStruct((m, Q_LORA), BF16),
                   jax.ShapeDtypeStruct((m, KV_LORA), BF16),
                   jax.ShapeDtypeStruct((m, LANES), BF16)],
        compiler_params=pltpu.CompilerParams(dimension_semantics=("parallel",),
                                             vmem_limit_bytes=BIG_VMEM_LIMIT_BYTES),
        name="latents",
    )(x, g_in.reshape(1, d), w_t, w_t, w_t, gq.reshape(1, -1), gkv.reshape(1, -1), cos, sin)


def _gate_kernel(xn_ref, w_ref, o_ref):
    o_ref[...] = _silu(_dot_nt(xn_ref[...], w_ref[...])).astype(o_ref.dtype)


def _gate(xn, w_z_t, tm, tn):
    m, d = xn.shape
    n = w_z_t.shape[0]
    return pl.pallas_call(
        _gate_kernel,
        grid=(m // tm, n // tn),
        in_specs=[pl.BlockSpec((tm, d), lambda i, j: (i, 0)),
                  pl.BlockSpec((tn, d), lambda i, j: (j, 0))],
        out_specs=pl.BlockSpec((tm, tn), lambda i, j: (i, j)),
        out_shape=jax.ShapeDtypeStruct((m, n), BF16),
        compiler_params=_params("parallel", "parallel"),
        name="attn_gate",
    )(xn, w_z_t)


def _q_up_kernel(cq_ref, w_ref, cos_ref, sin_ref, q_ref):
    r = _dot(cq_ref[...], w_ref[...])
    cos = cos_ref[...]
    sin = sin_ref[...]
    for h in range(N_HEADS):
        lo = h * QK_PAD_DIM
        nope = r[:, lo:lo + LANES] * Q_SCALE
        pe = _rope_pair(r[:, lo + LANES:lo + QK_PAD_DIM], cos, sin) * Q_SCALE
        q_ref[:, lo:lo + LANES] = nope.astype(q_ref.dtype)
        q_ref[:, lo + LANES:lo + QK_PAD_DIM] = pe.astype(q_ref.dtype)


def _q_up(cq, w_q, cos, sin, seq, tm):
    m, k = cq.shape
    n = w_q.shape[1]
    tps = seq // tm
    return pl.pallas_call(
        _q_up_kernel,
        grid=(m // tm,),
        in_specs=[pl.BlockSpec((tm, k), lambda i: (i, 0)),
                  pl.BlockSpec((k, n), lambda i: (0, 0)),
                  pl.BlockSpec((tm, LANES), lambda i: (i % tps, 0)),
                  pl.BlockSpec((tm, LANES), lambda i: (i % tps, 0))],
        out_specs=pl.BlockSpec((tm, n), lambda i: (i, 0)),
        out_shape=jax.ShapeDtypeStruct((m, n), BF16),
        compiler_params=_params("parallel"),
        name="q_up",
    )(cq, w_q, cos, sin)


def _kv_up_kernel(ckv_ref, wk_ref, wvt_ref, k_ref, vt_ref):
    c = ckv_ref[...]
    k_ref[...] = _dot(c, wk_ref[...]).astype(k_ref.dtype)
    vt_ref[...] = _dot_nt(wvt_ref[...], c).astype(vt_ref.dtype)


def _kv_up(ckv, w_k, w_vt, tk):
    m, k = ckv.shape
    n = w_k.shape[1]
    return pl.pallas_call(
        _kv_up_kernel,
        grid=(m // tk,),
        in_specs=[pl.BlockSpec((tk, k), lambda i: (i, 0)),
                  pl.BlockSpec((k, n), lambda i: (0, 0)),
                  pl.BlockSpec((n, k), lambda i: (0, 0))],
        out_specs=[pl.BlockSpec((tk, n), lambda i: (i, 0)),
                   pl.BlockSpec((None, n, tk), lambda i: (i, 0, 0))],
        out_shape=[jax.ShapeDtypeStruct((m, n), BF16),
                   jax.ShapeDtypeStruct((m // tk, n, tk), BF16)],
        compiler_params=_params("parallel"),
        name="kv_up",
    )(ckv, w_k, w_vt)


CAST_PIECE_ROWS = 16
PROJ_CHUNK_K = 512


def _attention_kernel(q_ref, kn_ref, kpe_ref, vt_ref, gate_ref, px_ref, pw_ref, *rest,
                      blk, n_casts):
    cast_in = rest[:n_casts]
    o_ref, proj_ref = rest[n_casts:n_casts + 2]
    cast_out = rest[n_casts + 2:2 * n_casts + 2]
    kfull_ref, pacc_ref = rest[2 * n_casts + 2:]
    proj_chunks = list(range(0, pw_ref.shape[1], PROJ_CHUNK_K))
    cast_pieces = [(src, dst, r) for src, dst in zip(cast_in, cast_out)
                   for r in range(0, src.shape[0], CAST_PIECE_ROWS)]
    seq = q_ref.shape[0]
    nblk = seq // blk
    kfull_ref[:, :LANES] = kn_ref[...]
    kfull_ref[:, LANES:] = kpe_ref[...]

    key_chunk = lax.broadcasted_iota(jnp.int32, (blk, blk), 0) // CHUNK
    qry_chunk = lax.broadcasted_iota(jnp.int32, (blk, blk), 1) // CHUNK
    visible = key_chunk <= qry_chunk

    pairs = [(qi, kj) for qi in range(nblk) for kj in range(qi + 1)]

    def scores(qi, kj):
        s = _dot_nt(kfull_ref[kj * blk:(kj + 1) * blk, :],
                    q_ref[qi * blk:(qi + 1) * blk, :])
        return jnp.where(visible, s, -jnp.inf) if kj == qi else s

    m_run = l_run = acc = None
    ahead = [scores(*pairs[n]) for n in range(min(SCORE_LOOKAHEAD, len(pairs)))]
    for n, (qi, kj) in enumerate(pairs):
        s = ahead.pop(0)
        if n + SCORE_LOOKAHEAD < len(pairs):
            ahead.append(scores(*pairs[n + SCORE_LOOKAHEAD]))
        m_blk = jnp.max(s, axis=0, keepdims=True)
        if kj == 0:
            m_run = m_blk
            p = jnp.exp2(s - m_run)
            l_run = jnp.sum(p, axis=0, keepdims=True)
            acc = _dot(vt_ref[kj], p.astype(BF16))
        else:
            m_new = jnp.maximum(m_run, m_blk)
            alpha = jnp.exp2(m_run - m_new)
            p = jnp.exp2(s - m_new)
            l_run = alpha * l_run + jnp.sum(p, axis=0, keepdims=True)
            acc = alpha * acc + _dot(vt_ref[kj], p.astype(BF16))
            m_run = m_new
        if kj == qi:
            out = (acc / l_run).T
            gate = gate_ref[qi * blk:(qi + 1) * blk, :].astype(F32)
            o_ref[qi * blk:(qi + 1) * blk, :] = (out * gate).astype(o_ref.dtype)
        lo = n * len(cast_pieces) // len(pairs)
        hi = (n + 1) * len(cast_pieces) // len(pairs)
        for src, dst, r in cast_pieces[lo:hi]:
            dst[r:r + CAST_PIECE_ROWS, :] = src[r:r + CAST_PIECE_ROWS, :].astype(dst.dtype)
        lo = n * len(proj_chunks) // len(pairs)
        hi = (n + 1) * len(proj_chunks) // len(pairs)
        for k0 in proj_chunks[lo:hi]:
            part = _dot_nt(px_ref[:, k0:k0 + PROJ_CHUNK_K], pw_ref[:, k0:k0 + PROJ_CHUNK_K])
            if k0 == 0:
                pacc_ref[...] = part
            elif k0 != proj_chunks[-1]:
                pacc_ref[...] += part
            else:
                proj_ref[...] = (pacc_ref[...] + part).astype(proj_ref.dtype)


def _ride_along_tile(n_rows, n_cols, steps):
    for cols in (1024, 512, 256):
        rows = n_rows * n_cols // (steps * cols)
        if n_cols % cols == 0 and rows >= 256 and n_rows % rows == 0 \
                and rows * cols * steps == n_rows * n_cols:
            return rows, cols
    raise ValueError("no ride-along tiling for this shape")


def _attention(q, k_nope, k_pe, v_t, gate, batch, seq, blk, proj, casts):
    m = q.shape[0]
    nblk = seq // blk
    steps = batch * N_HEADS
    px, pw = proj
    pr, pc = _ride_along_tile(px.shape[0], pw.shape[0], steps)
    row_tiles = px.shape[0] // pr

    def step(b, h):
        return b * N_HEADS + h

    cast_in_specs, cast_out_specs, cast_shapes = [], [], []
    for w, row0, n_rows in casts:
        rows = n_rows // steps
        assert rows * steps == n_rows and rows % CAST_PIECE_ROWS == 0 and row0 % SUBLANES == 0
        cast_in_specs.append(pl.BlockSpec(
            (pl.Element(rows), pl.Element(w.shape[1])),
            lambda b, h, row0=row0, rows=rows: (pl.multiple_of(row0 + step(b, h) * rows, SUBLANES), 0)))
        cast_out_specs.append(pl.BlockSpec((rows, w.shape[1]), lambda b, h: (step(b, h), 0)))
        cast_shapes.append(jax.ShapeDtypeStruct((n_rows, w.shape[1]), BF16))
    return pl.pallas_call(
        functools.partial(_attention_kernel, blk=blk, n_casts=len(casts)),
        grid=(batch, N_HEADS),
        in_specs=[pl.BlockSpec((seq, QK_PAD_DIM), lambda b, h: (b, h)),
                  pl.BlockSpec((seq, LANES), lambda b, h: (b, h)),
                  pl.BlockSpec((seq, LANES), lambda b, h: (b, 0)),
                  pl.BlockSpec((nblk, V_HEAD_DIM, blk), lambda b, h: (b, h, 0)),
                  pl.BlockSpec((seq, LANES), lambda b, h: (b, h)),
                  pl.BlockSpec((pr, px.shape[1]), lambda b, h: (step(b, h) % row_tiles, 0)),
                  pl.BlockSpec((pc, pw.shape[1]), lambda b, h: (step(b, h) // row_tiles, 0),
                               pipeline_mode=pl.Buffered(1))]
                 + cast_in_specs,
        out_specs=[pl.BlockSpec((seq, V_HEAD_DIM), lambda b, h: (b, h)),
                   pl.BlockSpec((pr, pc), lambda b, h: (step(b, h) % row_tiles,
                                                        step(b, h) // row_tiles))]
                  + cast_out_specs,
        out_shape=[jax.ShapeDtypeStruct((m, N_HEADS * V_HEAD_DIM), BF16),
                   jax.ShapeDtypeStruct((px.shape[0], pw.shape[0]), BF16)] + cast_shapes,
        scratch_shapes=[pltpu.VMEM((seq, QK_PAD_DIM), BF16), pltpu.VMEM((pr, pc), F32)],
        compiler_params=_params("parallel", "parallel"),
        name="attention",
    )(q, k_nope, k_pe, v_t, gate, px, pw, *[w for w, _, _ in casts])


def _out_proj_kernel(yc_ref, ya_ref, w_ref, x_ref, g_ref, o_ref, *, final_norm, tn):
    kc = yc_ref.shape[1]
    d = o_ref.shape[1]
    yc = yc_ref[...]
    ya = ya_ref[...]
    ss = None
    for t in range(d // tn):
        cols = slice(t * tn, (t + 1) * tn)
        h = x_ref[:, cols] + (_dot(yc, w_ref[:kc, cols]) + _dot(ya, w_ref[kc:, cols]))
        o_ref[:, cols] = h.astype(o_ref.dtype)
        if final_norm:
            part = jnp.sum(h * h, axis=-1, keepdims=True)
            ss = part if ss is None else ss + part
    if final_norm:
        inv = lax.rsqrt(ss / d + NORM_EPS)
        for t in range(d // tn):
            cols = slice(t * tn, (t + 1) * tn)
            o_ref[:, cols] = (o_ref[:, cols] * inv) * g_ref[:, cols]


def _out_proj(y_conv, y_attn, w_o, x, g, final_norm, tm, tn):
    m, d = x.shape
    kc = y_conv.shape[1]
    ka = y_attn.shape[1]
    return pl.pallas_call(
        functools.partial(_out_proj_kernel, final_norm=final_norm, tn=tn),
        grid=(m // tm,),
        in_specs=[pl.BlockSpec((tm, kc), lambda i: (i, 0)),
                  pl.BlockSpec((tm, ka), lambda i: (i, 0)),
                  pl.BlockSpec((kc + ka, d), lambda i: (0, 0), pipeline_mode=pl.Buffered(1)),
                  pl.BlockSpec((tm, d), lambda i: (i, 0)),
                  pl.BlockSpec((1, d), lambda i: (0, 0))],
        out_specs=pl.BlockSpec((tm, d), lambda i: (i, 0)),
        out_shape=jax.ShapeDtypeStruct((m, d), x.dtype),
        compiler_params=pltpu.CompilerParams(dimension_semantics=("parallel",),
                                             vmem_limit_bytes=BIG_VMEM_LIMIT_BYTES),
        name="out_proj",
    )(y_conv, y_attn, w_o, x, g.reshape(1, d))


def _swap_halves_signed(w):
    return jnp.concatenate([-w[..., ROPE_HALF:], w[..., :ROPE_HALF]], axis=-1)


def _rope_tables(seq):
    pos = jnp.arange(seq, dtype=F32)
    inv_freq = 1.0 / (ROPE_THETA ** (jnp.arange(0, ROPE_DIM, 2, dtype=F32) / ROPE_DIM))
    ang = pos[:, None] * inv_freq[None, :]
    zeros = jnp.zeros((seq, LANES - ROPE_DIM), F32)
    cos = jnp.concatenate([jnp.cos(ang), jnp.cos(ang), zeros], axis=-1)
    sin = jnp.concatenate([jnp.sin(ang), jnp.sin(ang), zeros], axis=-1)
    return cos, sin


def _tile(n, pref):
    return pref if n % pref == 0 else n


def _layer(h, g_in, w_in, conv_w, q_norm_g, w_uq, kv_norm_g, w_ukv, w_out, g_out,
           final_norm, cos, sin, batch, seq):
    m, d = h.shape
    dc = conv_w.shape[1]
    o = 4 * dc

    w_in_t = w_in.T
    kr0 = o + Q_LORA + KV_LORA
    n_lat = Q_LORA + KV_LORA + LANES
    w_lat_t = _cast_rows(w_in_t, o, n_lat, n_lat // 2, d // 2, "cast_w_latent")
    w_b_t = _cast_rows(w_in_t, 0, dc, 512, d // 2, "cast_w_gb")
    w_z_t = _cast_rows(w_in_t, kr0 + ROPE_DIM, w_in_t.shape[0] - kr0 - ROPE_DIM, 512, d // 2,
                       "cast_w_gate")
    wq = w_uq.reshape(Q_LORA, N_HEADS, QK_NOPE_DIM + ROPE_DIM)
    wq_pe = wq[..., QK_NOPE_DIM:]
    w_q = jnp.concatenate([wq[..., :QK_NOPE_DIM], wq_pe, _swap_halves_signed(wq_pe)],
                          axis=-1).reshape(Q_LORA, N_HEADS * QK_PAD_DIM).astype(BF16)
    wkv = w_ukv.reshape(KV_LORA, N_HEADS, QK_NOPE_DIM + V_HEAD_DIM)
    w_k = wkv[..., :QK_NOPE_DIM].reshape(KV_LORA, N_HEADS * QK_NOPE_DIM).astype(BF16)
    w_vt = wkv[..., QK_NOPE_DIM:].reshape(KV_LORA, N_HEADS * V_HEAD_DIM).T.astype(BF16)

    blk = _tile(seq, 512)
    xn, cq, ckv, k_pe = _latents(h, g_in, w_lat_t, 0, q_norm_g, kv_norm_g, cos, sin, seq,
                                 _tile(seq, 512))
    gate = _gate(xn, w_z_t, _tile(m, 1024), _tile(w_z_t.shape[0], 1024))
    q = _q_up(cq, w_q, cos, sin, seq, _tile(seq, 512))
    k_nope, v_t = _kv_up(ckv, w_k, w_vt, blk)
    y_attn, g_b, w_conv_t = _attention(q, k_nope, k_pe, v_t, gate, batch, seq, blk,
                                       (xn, w_b_t), [(w_in_t, dc, o - dc)])
    y_conv, w_o = _conv_group(xn, g_b, w_conv_t, conv_w, w_out, seq,
                              _tile(seq, 512), _tile(dc, 512))
    return _out_proj(y_conv, y_attn, w_o, h, g_out, final_norm, _tile(m, 256), _tile(d, 512))


def kernel(x, g_in, w_in, conv_w, q_norm_g, w_uq, kv_norm_g, w_ukv, w_out, g_final):
    batch, seq, d = x.shape
    depth = g_in.shape[0]
    cos, sin = _rope_tables(seq)
    h = x.reshape(batch * seq, d)
    for l in range(depth):
        h = _layer(h, g_in[l], w_in[l], conv_w[l], q_norm_g[l], w_uq[l], kv_norm_g[l],
                   w_ukv[l], w_out[l], g_final, l == depth - 1, cos, sin, batch, seq)
    return h.reshape(batch, seq, d)
```

```python
import functools

import jax
import jax.numpy as jnp
import numpy as np
from jax import lax
from jax.experimental import pallas as pl
from jax.experimental.pallas import tpu as pltpu

CHUNK = 64
CONV_WIDTH = 3
N_HEADS = 16
V_HEAD_DIM = 128
QK_NOPE_DIM = 128
ROPE_DIM = 64
ROPE_HALF = ROPE_DIM // 2
Q_LORA = 1024
KV_LORA = 512
ROPE_THETA = 10000.0
NORM_EPS = 1e-6
ATTN_SCALE = (QK_NOPE_DIM + ROPE_DIM) ** -0.5
LOG2_E = float(np.log2(np.e))
Q_SCALE = ATTN_SCALE * LOG2_E

LANES = 128
SUBLANES = 8
QK_PAD_DIM = 2 * LANES
ATTN_BLOCK = 512
SCORE_LOOKAHEAD = 2
VMEM_LIMIT_BYTES = 56 * 1024 * 1024
BIG_VMEM_LIMIT_BYTES = 60 * 1024 * 1024

BF16 = jnp.bfloat16
F32 = jnp.float32


def _params(*semantics):
    return pltpu.CompilerParams(dimension_semantics=semantics,
                                vmem_limit_bytes=VMEM_LIMIT_BYTES)


def _dot(a, b):
    return jnp.dot(a, b, preferred_element_type=F32)


def _dot_nt(a, b):
    return lax.dot_general(a, b, (((1,), (1,)), ((), ())), preferred_element_type=F32)


def _rmsnorm_rows(x, g):
    ms = jnp.mean(x * x, axis=-1, keepdims=True)
    return (x * lax.rsqrt(ms + NORM_EPS)) * g


def _silu(z):
    return z * (0.5 * (jnp.tanh(0.5 * z) + 1.0))


def _cast_kernel(w_ref, o_ref):
    o_ref[...] = w_ref[...].astype(o_ref.dtype)


def _cast_rows(w, row0, n_rows, tr, tc, name):
    cols = w.shape[1]
    assert row0 % SUBLANES == 0 and n_rows % tr == 0 and cols % tc == 0
    return pl.pallas_call(
        _cast_kernel,
        grid=(n_rows // tr, cols // tc),
        in_specs=[pl.BlockSpec((pl.Element(tr), pl.Element(tc)),
                               lambda r, c: (pl.multiple_of(row0 + r * tr, SUBLANES),
                                             pl.multiple_of(c * tc, LANES)))],
        out_specs=pl.BlockSpec((tr, tc), lambda r, c: (r, c)),
        out_shape=jax.ShapeDtypeStruct((n_rows, cols), BF16),
        compiler_params=_params("parallel", "parallel"),
        name=name,
    )(w)


def _conv_group_kernel(xn_ref, wb_ref, wc_ref, wh_ref, wz_ref, cw_ref, y_ref, carry_ref,
                       *, tiles_per_seq):
    i = pl.program_id(0)
    j = pl.program_id(1)
    xn = xn_ref[...]
    u = _dot_nt(xn, wc_ref[...]) * _dot_nt(xn, wh_ref[...])
    tm, tc = u.shape

    @pl.when(i % tiles_per_seq == 0)
    def _():
        carry_ref[j] = jnp.zeros((SUBLANES, tc), F32)

    prev = carry_ref[j]
    carry_ref[j] = u[tm - SUBLANES:, :]

    row = lax.broadcasted_iota(jnp.int32, (SUBLANES, tc), 0)

    def shifted(k):
        body = pltpu.roll(u, k, 0)
        head = jnp.where(row < k, pltpu.roll(prev, k, 0), body[:SUBLANES, :])
        return jnp.concatenate([head, body[SUBLANES:, :]], axis=0)

    cw = cw_ref[...]
    conv = cw[0:1, :] * shifted(2) + cw[1:2, :] * shifted(1) + cw[2:3, :] * u
    y = (_dot_nt(xn, wb_ref[...]) * conv) * _silu(_dot_nt(xn, wz_ref[...]))
    y_ref[...] = y.astype(y_ref.dtype)


def _conv_group(xn, w_t, conv_w, seq, tm, tc):
    m, d = xn.shape
    dc = conv_w.shape[1]
    nj = dc // tc

    def w_spec(k):
        return pl.BlockSpec((tc, d), lambda i, j: (j + k * nj, 0))

    return pl.pallas_call(
        functools.partial(_conv_group_kernel, tiles_per_seq=seq // tm),
        grid=(m // tm, nj),
        in_specs=[pl.BlockSpec((tm, d), lambda i, j: (i, 0)),
                  w_spec(0), w_spec(1), w_spec(2), w_spec(3),
                  pl.BlockSpec((CONV_WIDTH, tc), lambda i, j: (0, j))],
        out_specs=pl.BlockSpec((tm, tc), lambda i, j: (i, j)),
        out_shape=jax.ShapeDtypeStruct((m, dc), BF16),
        scratch_shapes=[pltpu.VMEM((nj, SUBLANES, tc), F32)],
        compiler_params=_params("arbitrary", "arbitrary"),
        name="conv_group",
    )(xn, w_t, w_t, w_t, w_t, conv_w)


def _rope_pair(t, cos, sin):
    return t * cos + pltpu.roll(t, ROPE_DIM, 1) * sin


def _rope_low_half(t, cos, sin):
    lane = lax.broadcasted_iota(jnp.int32, t.shape, 1)
    partner = jnp.where(lane < ROPE_HALF,
                        -pltpu.roll(t, LANES - ROPE_HALF, 1),
                        pltpu.roll(t, ROPE_HALF, 1))
    return jnp.where(lane < ROPE_DIM, t * cos + partner * sin, 0.0)


def _latent_kernel(x_ref, g_ref, w_ref, gq_ref, gkv_ref, cos_ref, sin_ref, cast_in_ref,
                   xn_ref, cq_ref, ckv_ref, kpe_ref, cast_out_ref):
    cast_out_ref[...] = cast_in_ref[...].astype(cast_out_ref.dtype)
    g = g_ref[...]
    c = _dot_nt((x_ref[...] * g).astype(BF16), w_ref[...])
    x = x_ref[...]
    inv = lax.rsqrt(jnp.mean(x * x, axis=-1, keepdims=True) + NORM_EPS)
    xn_ref[...] = ((x * inv) * g).astype(xn_ref.dtype)
    c = c * inv
    cq_ref[...] = _rmsnorm_rows(c[:, :Q_LORA], gq_ref[...]).astype(cq_ref.dtype)
    ckv_ref[...] = _rmsnorm_rows(c[:, Q_LORA:Q_LORA + KV_LORA], gkv_ref[...]).astype(ckv_ref.dtype)
    kpe_ref[...] = _rope_low_half(c[:, Q_LORA + KV_LORA:], cos_ref[...],
                                  sin_ref[...]).astype(kpe_ref.dtype)


def _latents(x, g_in, w_t, gq, gkv, cos, sin, w_cast, cast_row0, cast_rows, seq, tm):
    m, d = x.shape
    tps = seq // tm
    n = w_t.shape[0]
    assert n == Q_LORA + KV_LORA + LANES
    cr = cast_rows // (m // tm)
    assert cr * (m // tm) == cast_rows and cr % CAST_PIECE_ROWS == 0 and cast_row0 % SUBLANES == 0
    return pl.pallas_call(
        _latent_kernel,
        grid=(m // tm,),
        in_specs=[pl.BlockSpec((tm, d), lambda i: (i, 0)),
                  pl.BlockSpec((1, d), lambda i: (0, 0)),
                  pl.BlockSpec((n, d), lambda i: (0, 0), pipeline_mode=pl.Buffered(1)),
                  pl.BlockSpec((1, Q_LORA), lambda i: (0, 0)),
                  pl.BlockSpec((1, KV_LORA), lambda i: (0, 0)),
                  pl.BlockSpec((tm, LANES), lambda i: (i % tps, 0)),
                  pl.BlockSpec((tm, LANES), lambda i: (i % tps, 0)),
                  pl.BlockSpec((pl.Element(cr), pl.Element(w_cast.shape[1])),
                               lambda i: (pl.multiple_of(cast_row0 + i * cr, SUBLANES), 0))],
        out_specs=[pl.BlockSpec((tm, d), lambda i: (i, 0)),
                   pl.BlockSpec((tm, Q_LORA), lambda i: (i, 0)),
                   pl.BlockSpec((tm, KV_LORA), lambda i: (i, 0)),
                   pl.BlockSpec((tm, LANES), lambda i: (i, 0)),
                   pl.BlockSpec((cr, w_cast.shape[1]), lambda i: (i, 0))],
        out_shape=[jax.ShapeDtypeStruct((m, d), BF16),
                   jax.ShapeDtypeStruct((m, Q_LORA), BF16),
                   jax.ShapeDtypeStruct((m, KV_LORA), BF16),
                   jax.ShapeDtypeStruct((m, LANES), BF16),
                   jax.ShapeDtypeStruct((cast_rows, w_cast.shape[1]), BF16)],
        compiler_params=pltpu.CompilerParams(dimension_semantics=("parallel",),
                                             vmem_limit_bytes=BIG_VMEM_LIMIT_BYTES),
        name="latents",
    )(x, g_in.reshape(1, d), w_t, gq.reshape(1, -1), gkv.reshape(1, -1), cos, sin, w_cast)


def _gate_kernel(xn_ref, w_ref, o_ref):
    o_ref[...] = _silu(_dot_nt(xn_ref[...], w_ref[...])).astype(o_ref.dtype)


def _gate(xn, w_z_t, tm, tn):
    m, d = xn.shape
    n = w_z_t.shape[0]
    return pl.pallas_call(
        _gate_kernel,
        grid=(m // tm, n // tn),
        in_specs=[pl.BlockSpec((tm, d), lambda i, j: (i, 0)),
                  pl.BlockSpec((tn, d), lambda i, j: (j, 0))],
        out_specs=pl.BlockSpec((tm, tn), lambda i, j: (i, j)),
        out_shape=jax.ShapeDtypeStruct((m, n), BF16),
        compiler_params=_params("parallel", "parallel"),
        name="attn_gate",
    )(xn, w_z_t)


def _q_up_kernel(cq_ref, w_ref, cos_ref, sin_ref, q_ref):
    r = _dot(cq_ref[...], w_ref[...])
    cos = cos_ref[...]
    sin = sin_ref[...]
    for h in range(N_HEADS):
        lo = h * QK_PAD_DIM
        nope = r[:, lo:lo + LANES] * Q_SCALE
        pe = _rope_pair(r[:, lo + LANES:lo + QK_PAD_DIM], cos, sin) * Q_SCALE
        q_ref[:, lo:lo + LANES] = nope.astype(q_ref.dtype)
        q_ref[:, lo + LANES:lo + QK_PAD_DIM] = pe.astype(q_ref.dtype)


def _q_up(cq, w_q, cos, sin, seq, tm):
    m, k = cq.shape
    n = w_q.shape[1]
    tps = seq // tm
    return pl.pallas_call(
        _q_up_kernel,
        grid=(m // tm,),
        in_specs=[pl.BlockSpec((tm, k), lambda i: (i, 0)),
                  pl.BlockSpec((k, n), lambda i: (0, 0)),
                  pl.BlockSpec((tm, LANES), lambda i: (i % tps, 0)),
                  pl.BlockSpec((tm, LANES), lambda i: (i % tps, 0))],
        out_specs=pl.BlockSpec((tm, n), lambda i: (i, 0)),
        out_shape=jax.ShapeDtypeStruct((m, n), BF16),
        compiler_params=_params("parallel"),
        name="q_up",
    )(cq, w_q, cos, sin)


def _kv_up_kernel(ckv_ref, wk_ref, wvt_ref, k_ref, vt_ref, *, blk):
    c = ckv_ref[...]
    k_ref[...] = _dot(c, wk_ref[...]).astype(k_ref.dtype)
    for r in range(c.shape[0] // blk):
        vt_ref[r] = _dot_nt(wvt_ref[...], c[r * blk:(r + 1) * blk, :]).astype(vt_ref.dtype)


def _kv_up(ckv, w_k, w_vt, tm, blk):
    m, k = ckv.shape
    n = w_k.shape[1]
    per = tm // blk
    return pl.pallas_call(
        functools.partial(_kv_up_kernel, blk=blk),
        grid=(m // tm,),
        in_specs=[pl.BlockSpec((tm, k), lambda i: (i, 0)),
                  pl.BlockSpec((k, n), lambda i: (0, 0)),
                  pl.BlockSpec((n, k), lambda i: (0, 0))],
        out_specs=[pl.BlockSpec((tm, n), lambda i: (i, 0)),
                   pl.BlockSpec((per, n, blk), lambda i: (i, 0, 0))],
        out_shape=[jax.ShapeDtypeStruct((m, n), BF16),
                   jax.ShapeDtypeStruct((m // blk, n, blk), BF16)],
        compiler_params=_params("parallel"),
        name="kv_up",
    )(ckv, w_k, w_vt)


CAST_PIECE_ROWS = 16


def _attention_kernel(q_ref, kn_ref, kpe_ref, vt_ref, gate_ref, *rest, blk, n_casts):
    cast_in = rest[:n_casts]
    o_ref = rest[n_casts]
    cast_out = rest[n_casts + 1:2 * n_casts + 1]
    kfull_ref = rest[2 * n_casts + 1]
    cast_pieces = [(src, dst, r) for src, dst in zip(cast_in, cast_out)
                   for r in range(0, src.shape[0], CAST_PIECE_ROWS)]
    seq = q_ref.shape[0]
    nblk = seq // blk
    kfull_ref[:, :LANES] = kn_ref[...]
    kfull_ref[:, LANES:] = kpe_ref[...]

    key_chunk = lax.broadcasted_iota(jnp.int32, (blk, blk), 0) // CHUNK
    qry_chunk = lax.broadcasted_iota(jnp.int32, (blk, blk), 1) // CHUNK
    visible = key_chunk <= qry_chunk

    pairs = [(qi, kj) for qi in range(nblk) for kj in range(qi + 1)]

    def scores(qi, kj):
        s = _dot_nt(kfull_ref[kj * blk:(kj + 1) * blk, :],
                    q_ref[qi * blk:(qi + 1) * blk, :])
        return jnp.where(visible, s, -jnp.inf) if kj == qi else s

    m_run = l_run = acc = None
    ahead = [scores(*pairs[n]) for n in range(min(SCORE_LOOKAHEAD, len(pairs)))]
    for n, (qi, kj) in enumerate(pairs):
        s = ahead.pop(0)
        if n + SCORE_LOOKAHEAD < len(pairs):
            ahead.append(scores(*pairs[n + SCORE_LOOKAHEAD]))
        m_blk = jnp.max(s, axis=0, keepdims=True)
        if kj == 0:
            m_run = m_blk
            p = jnp.exp2(s - m_run)
            l_run = jnp.sum(p, axis=0, keepdims=True)
            acc = _dot(vt_ref[kj], p.astype(BF16))
        else:
            m_new = jnp.maximum(m_run, m_blk)
            alpha = jnp.exp2(m_run - m_new)
            p = jnp.exp2(s - m_new)
            l_run = alpha * l_run + jnp.sum(p, axis=0, keepdims=True)
            acc = alpha * acc + _dot(vt_ref[kj], p.astype(BF16))
            m_run = m_new
        if kj == qi:
            out = (acc / l_run).T
            gate = gate_ref[qi * blk:(qi + 1) * blk, :].astype(F32)
            o_ref[qi * blk:(qi + 1) * blk, :] = (out * gate).astype(o_ref.dtype)
        lo = n * len(cast_pieces) // len(pairs)
        hi = (n + 1) * len(cast_pieces) // len(pairs)
        for src, dst, r in cast_pieces[lo:hi]:
            dst[r:r + CAST_PIECE_ROWS, :] = src[r:r + CAST_PIECE_ROWS, :].astype(dst.dtype)


def _attention(q, k_nope, k_pe, v_t, gate, batch, seq, blk, casts):
    m = q.shape[0]
    nblk = seq // blk
    steps = batch * N_HEADS
    cast_specs, cast_shapes = [], []
    for w, n_rows in casts:
        rows = n_rows // steps
        assert rows * steps == n_rows and rows % CAST_PIECE_ROWS == 0
        cast_specs.append(pl.BlockSpec((rows, w.shape[1]), lambda b, h: (b * N_HEADS + h, 0)))
        cast_shapes.append(jax.ShapeDtypeStruct((n_rows, w.shape[1]), BF16))
    return pl.pallas_call(
        functools.partial(_attention_kernel, blk=blk, n_casts=len(casts)),
        grid=(batch, N_HEADS),
        in_specs=[pl.BlockSpec((seq, QK_PAD_DIM), lambda b, h: (b, h)),
                  pl.BlockSpec((seq, LANES), lambda b, h: (b, h)),
                  pl.BlockSpec((seq, LANES), lambda b, h: (b, 0)),
                  pl.BlockSpec((nblk, V_HEAD_DIM, blk), lambda b, h: (b, h, 0)),
                  pl.BlockSpec((seq, LANES), lambda b, h: (b, h))] + cast_specs,
        out_specs=[pl.BlockSpec((seq, V_HEAD_DIM), lambda b, h: (b, h))] + cast_specs,
        out_shape=[jax.ShapeDtypeStruct((m, N_HEADS * V_HEAD_DIM), BF16)] + cast_shapes,
        scratch_shapes=[pltpu.VMEM((seq, QK_PAD_DIM), BF16)],
        compiler_params=_params("parallel", "parallel"),
        name="attention",
    )(q, k_nope, k_pe, v_t, gate, *[w for w, _ in casts])


def _out_proj_kernel(yc_ref, ya_ref, w_ref, x_ref, g_ref, o_ref, *, final_norm, tn):
    kc = yc_ref.shape[1]
    d = o_ref.shape[1]
    yc = yc_ref[...]
    ya = ya_ref[...]
    ss = None
    for t in range(d // tn):
        cols = slice(t * tn, (t + 1) * tn)
        h = x_ref[:, cols] + (_dot(yc, w_ref[:kc, cols]) + _dot(ya, w_ref[kc:, cols]))
        o_ref[:, cols] = h.astype(o_ref.dtype)
        if final_norm:
            part = jnp.sum(h * h, axis=-1, keepdims=True)
            ss = part if ss is None else ss + part
    if final_norm:
        inv = lax.rsqrt(ss / d + NORM_EPS)
        for t in range(d // tn):
            cols = slice(t * tn, (t + 1) * tn)
            o_ref[:, cols] = (o_ref[:, cols] * inv) * g_ref[:, cols]


def _out_proj(y_conv, y_attn, w_o, x, g, final_norm, tm, tn):
    m, d = x.shape
    kc = y_conv.shape[1]
    ka = y_attn.shape[1]
    return pl.pallas_call(
        functools.partial(_out_proj_kernel, final_norm=final_norm, tn=tn),
        grid=(m // tm,),
        in_specs=[pl.BlockSpec((tm, kc), lambda i: (i, 0)),
                  pl.BlockSpec((tm, ka), lambda i: (i, 0)),
                  pl.BlockSpec((kc + ka, d), lambda i: (0, 0), pipeline_mode=pl.Buffered(1)),
                  pl.BlockSpec((tm, d), lambda i: (i, 0)),
                  pl.BlockSpec((1, d), lambda i: (0, 0))],
        out_specs=pl.BlockSpec((tm, d), lambda i: (i, 0)),
        out_shape=jax.ShapeDtypeStruct((m, d), x.dtype),
        compiler_params=pltpu.CompilerParams(dimension_semantics=("parallel",),
                                             vmem_limit_bytes=BIG_VMEM_LIMIT_BYTES),
        name="out_proj",
    )(y_conv, y_attn, w_o, x, g.reshape(1, d))


def _swap_halves_signed(w):
    return jnp.concatenate([-w[..., ROPE_HALF:], w[..., :ROPE_HALF]], axis=-1)


def _rope_tables(seq):
    pos = jnp.arange(seq, dtype=F32)
    inv_freq = 1.0 / (ROPE_THETA ** (jnp.arange(0, ROPE_DIM, 2, dtype=F32) / ROPE_DIM))
    ang = pos[:, None] * inv_freq[None, :]
    zeros = jnp.zeros((seq, LANES - ROPE_DIM), F32)
    cos = jnp.concatenate([jnp.cos(ang), jnp.cos(ang), zeros], axis=-1)
    sin = jnp.concatenate([jnp.sin(ang), jnp.sin(ang), zeros], axis=-1)
    return cos, sin


def _tile(n, pref):
    return pref if n % pref == 0 else n


def _layer(h, g_in, w_in, conv_w, q_norm_g, w_uq, kv_norm_g, w_ukv, w_out, g_out,
           final_norm, cos, sin, batch, seq):
    m, d = h.shape
    dc = conv_w.shape[1]
    o = 4 * dc

    w_in_t = w_in.T
    kr0 = o + Q_LORA + KV_LORA
    n_lat = Q_LORA + KV_LORA + LANES
    w_lat_t = _cast_rows(w_in_t, o, n_lat, n_lat // 2, d // 2, "cast_w_latent")
    wq = w_uq.reshape(Q_LORA, N_HEADS, QK_NOPE_DIM + ROPE_DIM)
    wq_pe = wq[..., QK_NOPE_DIM:]
    w_q = jnp.concatenate([wq[..., :QK_NOPE_DIM], wq_pe, _swap_halves_signed(wq_pe)],
                          axis=-1).reshape(Q_LORA, N_HEADS * QK_PAD_DIM).astype(BF16)
    wkv = w_ukv.reshape(KV_LORA, N_HEADS, QK_NOPE_DIM + V_HEAD_DIM)
    w_k = wkv[..., :QK_NOPE_DIM].reshape(KV_LORA, N_HEADS * QK_NOPE_DIM).astype(BF16)
    w_vt = wkv[..., QK_NOPE_DIM:].reshape(KV_LORA, N_HEADS * V_HEAD_DIM).T.astype(BF16)

    blk = _tile(seq, ATTN_BLOCK)
    xn, cq, ckv, k_pe, w_z_t = _latents(h, g_in, w_lat_t, q_norm_g, kv_norm_g, cos, sin,
                                        w_in_t, kr0 + ROPE_DIM, w_in_t.shape[0] - kr0 - ROPE_DIM,
                                        seq, _tile(seq, 512))
    gate = _gate(xn, w_z_t, _tile(m, 1024), _tile(w_z_t.shape[0], 1024))
    q = _q_up(cq, w_q, cos, sin, seq, _tile(seq, 512))
    k_nope, v_t = _kv_up(ckv, w_k, w_vt, _tile(seq, 512), blk)
    y_attn, w_conv_t, w_o = _attention(q, k_nope, k_pe, v_t, gate, batch, seq, blk,
                                       [(w_in_t, o), (w_out, w_out.shape[0])])
    y_conv = _conv_group(xn, w_conv_t, conv_w, seq, _tile(seq, 1024), _tile(dc, 256))
    return _out_proj(y_conv, y_attn, w_o, h, g_out, final_norm, _tile(m, 256), _tile(d, 512))


def kernel(x, g_in, w_in, conv_w, q_norm_g, w_uq, kv_norm_g, w_ukv, w_out, g_final):
    batch, seq, d = x.shape
    depth = g_in.shape[0]
    cos, sin = _rope_tables(seq)
    h = x.reshape(batch * seq, d)
    for l in range(depth):
        h = _layer(h, g_in[l], w_in[l], conv_w[l], q_norm_g[l], w_uq[l], kv_norm_g[l],
                   w_ukv[l], w_out[l], g_final, l == depth - 1, cos, sin, batch, seq)
    return h.reshape(batch, seq, d)
```

```python
import functools

import jax
import jax.numpy as jnp
import numpy as np
from jax import lax
from jax.experimental import pallas as pl
from jax.experimental.pallas import tpu as pltpu

CHUNK = 64
CONV_WIDTH = 3
N_HEADS = 16
V_HEAD_DIM = 128
QK_NOPE_DIM = 128
ROPE_DIM = 64
ROPE_HALF = ROPE_DIM // 2
Q_LORA = 1024
KV_LORA = 512
ROPE_THETA = 10000.0
NORM_EPS = 1e-6
ATTN_SCALE = (QK_NOPE_DIM + ROPE_DIM) ** -0.5
LOG2_E = float(np.log2(np.e))
Q_SCALE = ATTN_SCALE * LOG2_E

LANES = 128
SUBLANES = 8
QK_PAD_DIM = 2 * LANES
ATTN_BLOCK = 512
SCORE_LOOKAHEAD = 2
VMEM_LIMIT_BYTES = 56 * 1024 * 1024
BIG_VMEM_LIMIT_BYTES = 60 * 1024 * 1024

BF16 = jnp.bfloat16
F32 = jnp.float32


def _params(*semantics):
    return pltpu.CompilerParams(dimension_semantics=semantics,
                                vmem_limit_bytes=VMEM_LIMIT_BYTES)


def _dot(a, b):
    return jnp.dot(a, b, preferred_element_type=F32)


def _dot_nt(a, b):
    return lax.dot_general(a, b, (((1,), (1,)), ((), ())), preferred_element_type=F32)


def _rmsnorm_rows(x, g):
    ms = jnp.mean(x * x, axis=-1, keepdims=True)
    return (x * lax.rsqrt(ms + NORM_EPS)) * g


def _silu(z):
    return z * (0.5 * (jnp.tanh(0.5 * z) + 1.0))


def _cast_kernel(w_ref, o_ref):
    o_ref[...] = w_ref[...].astype(o_ref.dtype)


def _cast_rows(w, row0, n_rows, tr, tc, name):
    cols = w.shape[1]
    assert row0 % SUBLANES == 0 and n_rows % tr == 0 and cols % tc == 0
    return pl.pallas_call(
        _cast_kernel,
        grid=(n_rows // tr, cols // tc),
        in_specs=[pl.BlockSpec((pl.Element(tr), pl.Element(tc)),
                               lambda r, c: (pl.multiple_of(row0 + r * tr, SUBLANES),
                                             pl.multiple_of(c * tc, LANES)))],
        out_specs=pl.BlockSpec((tr, tc), lambda r, c: (r, c)),
        out_shape=jax.ShapeDtypeStruct((n_rows, cols), BF16),
        compiler_params=_params("parallel", "parallel"),
        name=name,
    )(w)


def _conv_group_kernel(xn_ref, wb_ref, wc_ref, wh_ref, wz_ref, cw_ref, y_ref, carry_ref,
                       *, tiles_per_seq):
    i = pl.program_id(0)
    j = pl.program_id(1)
    xn = xn_ref[...]
    u = _dot_nt(xn, wc_ref[...]) * _dot_nt(xn, wh_ref[...])
    tm, tc = u.shape

    @pl.when(i % tiles_per_seq == 0)
    def _():
        carry_ref[j] = jnp.zeros((SUBLANES, tc), F32)

    prev = carry_ref[j]
    carry_ref[j] = u[tm - SUBLANES:, :]

    row = lax.broadcasted_iota(jnp.int32, (SUBLANES, tc), 0)

    def shifted(k):
        body = pltpu.roll(u, k, 0)
        head = jnp.where(row < k, pltpu.roll(prev, k, 0), body[:SUBLANES, :])
        return jnp.concatenate([head, body[SUBLANES:, :]], axis=0)

    cw = cw_ref[...]
    conv = cw[0:1, :] * shifted(2) + cw[1:2, :] * shifted(1) + cw[2:3, :] * u
    y = (_dot_nt(xn, wb_ref[...]) * conv) * _silu(_dot_nt(xn, wz_ref[...]))
    y_ref[...] = y.astype(y_ref.dtype)


def _conv_group(xn, w_t, conv_w, seq, tm, tc):
    m, d = xn.shape
    dc = conv_w.shape[1]
    nj = dc // tc

    def w_spec(k):
        return pl.BlockSpec((tc, d), lambda i, j: (j + k * nj, 0))

    return pl.pallas_call(
        functools.partial(_conv_group_kernel, tiles_per_seq=seq // tm),
        grid=(m // tm, nj),
        in_specs=[pl.BlockSpec((tm, d), lambda i, j: (i, 0)),
                  w_spec(0), w_spec(1), w_spec(2), w_spec(3),
                  pl.BlockSpec((CONV_WIDTH, tc), lambda i, j: (0, j))],
        out_specs=pl.BlockSpec((tm, tc), lambda i, j: (i, j)),
        out_shape=jax.ShapeDtypeStruct((m, dc), BF16),
        scratch_shapes=[pltpu.VMEM((nj, SUBLANES, tc), F32)],
        compiler_params=_params("arbitrary", "arbitrary"),
        name="conv_group",
    )(xn, w_t, w_t, w_t, w_t, conv_w)


def _rope_pair(t, cos, sin):
    return t * cos + pltpu.roll(t, ROPE_DIM, 1) * sin


def _rope_low_half(t, cos, sin):
    lane = lax.broadcasted_iota(jnp.int32, t.shape, 1)
    partner = jnp.where(lane < ROPE_HALF,
                        -pltpu.roll(t, LANES - ROPE_HALF, 1),
                        pltpu.roll(t, ROPE_HALF, 1))
    return jnp.where(lane < ROPE_DIM, t * cos + partner * sin, 0.0)


def _latent_kernel(x_ref, g_ref, w_ref, gq_ref, gkv_ref, cos_ref, sin_ref, cast_in_ref,
                   xn_ref, cq_ref, ckv_ref, kpe_ref, cast_out_ref):
    cast_out_ref[...] = cast_in_ref[...].astype(cast_out_ref.dtype)
    g = g_ref[...]
    c = _dot_nt((x_ref[...] * g).astype(BF16), w_ref[...])
    x = x_ref[...]
    inv = lax.rsqrt(jnp.mean(x * x, axis=-1, keepdims=True) + NORM_EPS)
    xn_ref[...] = ((x * inv) * g).astype(xn_ref.dtype)
    c = c * inv
    cq_ref[...] = _rmsnorm_rows(c[:, :Q_LORA], gq_ref[...]).astype(cq_ref.dtype)
    ckv_ref[...] = _rmsnorm_rows(c[:, Q_LORA:Q_LORA + KV_LORA], gkv_ref[...]).astype(ckv_ref.dtype)
    kpe_ref[...] = _rope_low_half(c[:, Q_LORA + KV_LORA:], cos_ref[...],
                                  sin_ref[...]).astype(kpe_ref.dtype)


def _latents(x, g_in, w_t, gq, gkv, cos, sin, w_cast, cast_row0, cast_rows, seq, tm):
    m, d = x.shape
    tps = seq // tm
    n = w_t.shape[0]
    assert n == Q_LORA + KV_LORA + LANES
    cr = cast_rows // (m // tm)
    assert cr * (m // tm) == cast_rows and cr % CAST_PIECE_ROWS == 0 and cast_row0 % SUBLANES == 0
    return pl.pallas_call(
        _latent_kernel,
        grid=(m // tm,),
        in_specs=[pl.BlockSpec((tm, d), lambda i: (i, 0)),
                  pl.BlockSpec((1, d), lambda i: (0, 0)),
                  pl.BlockSpec((n, d), lambda i: (0, 0), pipeline_mode=pl.Buffered(1)),
                  pl.BlockSpec((1, Q_LORA), lambda i: (0, 0)),
                  pl.BlockSpec((1, KV_LORA), lambda i: (0, 0)),
                  pl.BlockSpec((tm, LANES), lambda i: (i % tps, 0)),
                  pl.BlockSpec((tm, LANES), lambda i: (i % tps, 0)),
                  pl.BlockSpec((pl.Element(cr), pl.Element(w_cast.shape[1])),
                               lambda i: (pl.multiple_of(cast_row0 + i * cr, SUBLANES), 0))],
        out_specs=[pl.BlockSpec((tm, d), lambda i: (i, 0)),
                   pl.BlockSpec((tm, Q_LORA), lambda i: (i, 0)),
                   pl.BlockSpec((tm, KV_LORA), lambda i: (i, 0)),
                   pl.BlockSpec((tm, LANES), lambda i: (i, 0)),
                   pl.BlockSpec((cr, w_cast.shape[1]), lambda i: (i, 0))],
        out_shape=[jax.ShapeDtypeStruct((m, d), BF16),
                   jax.ShapeDtypeStruct((m, Q_LORA), BF16),
                   jax.ShapeDtypeStruct((m, KV_LORA), BF16),
                   jax.ShapeDtypeStruct((m, LANES), BF16),
                   jax.ShapeDtypeStruct((cast_rows, w_cast.shape[1]), BF16)],
        compiler_params=pltpu.CompilerParams(dimension_semantics=("parallel",),
                                             vmem_limit_bytes=BIG_VMEM_LIMIT_BYTES),
        name="latents",
    )(x, g_in.reshape(1, d), w_t, gq.reshape(1, -1), gkv.reshape(1, -1), cos, sin, w_cast)


def _gate_kernel(xn_ref, w_ref, o_ref):
    o_ref[...] = _silu(_dot_nt(xn_ref[...], w_ref[...])).astype(o_ref.dtype)


def _gate(xn, w_z_t, tm, tn):
    m, d = xn.shape
    n = w_z_t.shape[0]
    return pl.pallas_call(
        _gate_kernel,
        grid=(m // tm, n // tn),
        in_specs=[pl.BlockSpec((tm, d), lambda i, j: (i, 0)),
                  pl.BlockSpec((tn, d), lambda i, j: (j, 0))],
        out_specs=pl.BlockSpec((tm, tn), lambda i, j: (i, j)),
        out_shape=jax.ShapeDtypeStruct((m, n), BF16),
        compiler_params=_params("parallel", "parallel"),
        name="attn_gate",
    )(xn, w_z_t)


def _q_weight_kernel(w_ref, o_ref):
    lane = lax.broadcasted_iota(jnp.int32, (w_ref.shape[0], LANES), 1)

    def rope_tile(u):
        swapped = jnp.where(lane < LANES - ROPE_HALF, -pltpu.roll(u, ROPE_HALF, 1),
                            pltpu.roll(u, LANES - ROPE_HALF, 1))
        return jnp.where(lane < ROPE_DIM, u, swapped)

    t0 = w_ref[:, :LANES]
    t1 = w_ref[:, LANES:2 * LANES]
    t2 = w_ref[:, 2 * LANES:]
    t2r = pltpu.roll(t2, ROPE_DIM, 1)
    o_ref[:, :LANES] = t0.astype(o_ref.dtype)
    o_ref[:, LANES:2 * LANES] = rope_tile(t1).astype(o_ref.dtype)
    o_ref[:, 2 * LANES:3 * LANES] = jnp.where(lane < ROPE_DIM, pltpu.roll(t1, ROPE_DIM, 1),
                                              t2r).astype(o_ref.dtype)
    o_ref[:, 3 * LANES:] = rope_tile(t2r).astype(o_ref.dtype)


def _q_weights(w_uq):
    k = w_uq.shape[0]
    pair_in = 2 * (QK_NOPE_DIM + ROPE_DIM)
    return pl.pallas_call(
        _q_weight_kernel,
        grid=(N_HEADS // 2,),
        in_specs=[pl.BlockSpec((k, pair_in), lambda j: (0, j))],
        out_specs=pl.BlockSpec((k, 2 * QK_PAD_DIM), lambda j: (0, j)),
        out_shape=jax.ShapeDtypeStruct((k, N_HEADS * QK_PAD_DIM), BF16),
        compiler_params=_params("parallel"),
        name="q_weights",
    )(w_uq)


def _q_up_kernel(cq_ref, w_ref, cos_ref, sin_ref, q_ref):
    r = _dot(cq_ref[...], w_ref[...])
    cos = cos_ref[...]
    sin = sin_ref[...]
    for h in range(N_HEADS):
        lo = h * QK_PAD_DIM
        nope = r[:, lo:lo + LANES] * Q_SCALE
        pe = _rope_pair(r[:, lo + LANES:lo + QK_PAD_DIM], cos, sin) * Q_SCALE
        q_ref[:, lo:lo + LANES] = nope.astype(q_ref.dtype)
        q_ref[:, lo + LANES:lo + QK_PAD_DIM] = pe.astype(q_ref.dtype)


def _q_up(cq, w_q, cos, sin, seq, tm):
    m, k = cq.shape
    n = w_q.shape[1]
    tps = seq // tm
    return pl.pallas_call(
        _q_up_kernel,
        grid=(m // tm,),
        in_specs=[pl.BlockSpec((tm, k), lambda i: (i, 0)),
                  pl.BlockSpec((k, n), lambda i: (0, 0)),
                  pl.BlockSpec((tm, LANES), lambda i: (i % tps, 0)),
                  pl.BlockSpec((tm, LANES), lambda i: (i % tps, 0))],
        out_specs=pl.BlockSpec((tm, n), lambda i: (i, 0)),
        out_shape=jax.ShapeDtypeStruct((m, n), BF16),
        compiler_params=_params("parallel"),
        name="q_up",
    )(cq, w_q, cos, sin)


def _kv_up_kernel(ckv_ref, w_ref, k_ref, vt_ref, *, blk):
    kv = _dot(ckv_ref[...], w_ref[...])
    for h in range(N_HEADS):
        lo = h * (QK_NOPE_DIM + V_HEAD_DIM)
        k_ref[:, h * QK_NOPE_DIM:(h + 1) * QK_NOPE_DIM] = (
            kv[:, lo:lo + QK_NOPE_DIM].astype(k_ref.dtype))
        v = kv[:, lo + QK_NOPE_DIM:lo + QK_NOPE_DIM + V_HEAD_DIM]
        for r in range(v.shape[0] // blk):
            vt_ref[r, h * V_HEAD_DIM:(h + 1) * V_HEAD_DIM, :] = (
                v[r * blk:(r + 1) * blk, :].T.astype(vt_ref.dtype))


def _kv_up(ckv, w_kv, tm, blk):
    m, k = ckv.shape
    n = N_HEADS * V_HEAD_DIM
    per = tm // blk
    return pl.pallas_call(
        functools.partial(_kv_up_kernel, blk=blk),
        grid=(m // tm,),
        in_specs=[pl.BlockSpec((tm, k), lambda i: (i, 0)),
                  pl.BlockSpec(w_kv.shape, lambda i: (0, 0))],
        out_specs=[pl.BlockSpec((tm, N_HEADS * QK_NOPE_DIM), lambda i: (i, 0)),
                   pl.BlockSpec((per, n, blk), lambda i: (i, 0, 0))],
        out_shape=[jax.ShapeDtypeStruct((m, N_HEADS * QK_NOPE_DIM), BF16),
                   jax.ShapeDtypeStruct((m // blk, n, blk), BF16)],
        compiler_params=_params("parallel"),
        name="kv_up",
    )(ckv, w_kv)


CAST_PIECE_ROWS = 16


def _attention_kernel(q_ref, kn_ref, kpe_ref, vt_ref, gate_ref, *rest, blk, n_casts):
    cast_in = rest[:n_casts]
    o_ref = rest[n_casts]
    cast_out = rest[n_casts + 1:2 * n_casts + 1]
    kfull_ref = rest[2 * n_casts + 1]
    cast_pieces = [(src, dst, r) for src, dst in zip(cast_in, cast_out)
                   for r in range(0, src.shape[0], CAST_PIECE_ROWS)]
    seq = q_ref.shape[0]
    nblk = seq // blk
    kfull_ref[:, :LANES] = kn_ref[...]
    kfull_ref[:, LANES:] = kpe_ref[...]

    key_chunk = lax.broadcasted_iota(jnp.int32, (blk, blk), 0) // CHUNK
    qry_chunk = lax.broadcasted_iota(jnp.int32, (blk, blk), 1) // CHUNK
    visible = key_chunk <= qry_chunk

    pairs = [(qi, kj) for qi in range(nblk) for kj in range(qi + 1)]

    def scores(qi, kj):
        s = _dot_nt(kfull_ref[kj * blk:(kj + 1) * blk, :],
                    q_ref[qi * blk:(qi + 1) * blk, :])
        return jnp.where(visible, s, -jnp.inf) if kj == qi else s

    m_run = l_run = acc = None
    ahead = [scores(*pairs[n]) for n in range(min(SCORE_LOOKAHEAD, len(pairs)))]
    for n, (qi, kj) in enumerate(pairs):
        s = ahead.pop(0)
        if n + SCORE_LOOKAHEAD < len(pairs):
            ahead.append(scores(*pairs[n + SCORE_LOOKAHEAD]))
        m_blk = jnp.max(s, axis=0, keepdims=True)
        if kj == 0:
            m_run = m_blk
            p = jnp.exp2(s - m_run)
            l_run = jnp.sum(p, axis=0, keepdims=True)
            acc = _dot(vt_ref[kj], p.astype(BF16))
        else:
            m_new = jnp.maximum(m_run, m_blk)
            alpha = jnp.exp2(m_run - m_new)
            p = jnp.exp2(s - m_new)
            l_run = alpha * l_run + jnp.sum(p, axis=0, keepdims=True)
            acc = alpha * acc + _dot(vt_ref[kj], p.astype(BF16))
            m_run = m_new
        if kj == qi:
            out = (acc / l_run).T
            gate = gate_ref[qi * blk:(qi + 1) * blk, :].astype(F32)
            o_ref[qi * blk:(qi + 1) * blk, :] = (out * gate).astype(o_ref.dtype)
        lo = n * len(cast_pieces) // len(pairs)
        hi = (n + 1) * len(cast_pieces) // len(pairs)
        for src, dst, r in cast_pieces[lo:hi]:
            dst[r:r + CAST_PIECE_ROWS, :] = src[r:r + CAST_PIECE_ROWS, :].astype(dst.dtype)


def _attention(q, k_nope, k_pe, v_t, gate, batch, seq, blk, casts):
    m = q.shape[0]
    nblk = seq // blk
    steps = batch * N_HEADS
    cast_specs, cast_shapes = [], []
    for w, n_rows in casts:
        rows = n_rows // steps
        assert rows * steps == n_rows and rows % CAST_PIECE_ROWS == 0
        cast_specs.append(pl.BlockSpec((rows, w.shape[1]), lambda b, h: (b * N_HEADS + h, 0)))
        cast_shapes.append(jax.ShapeDtypeStruct((n_rows, w.shape[1]), BF16))
    return pl.pallas_call(
        functools.partial(_attention_kernel, blk=blk, n_casts=len(casts)),
        grid=(batch, N_HEADS),
        in_specs=[pl.BlockSpec((seq, QK_PAD_DIM), lambda b, h: (b, h)),
                  pl.BlockSpec((seq, LANES), lambda b, h: (b, h)),
                  pl.BlockSpec((seq, LANES), lambda b, h: (b, 0)),
                  pl.BlockSpec((nblk, V_HEAD_DIM, blk), lambda b, h: (b, h, 0)),
                  pl.BlockSpec((seq, LANES), lambda b, h: (b, h))] + cast_specs,
        out_specs=[pl.BlockSpec((seq, V_HEAD_DIM), lambda b, h: (b, h))] + cast_specs,
        out_shape=[jax.ShapeDtypeStruct((m, N_HEADS * V_HEAD_DIM), BF16)] + cast_shapes,
        scratch_shapes=[pltpu.VMEM((seq, QK_PAD_DIM), BF16)],
        compiler_params=_params("parallel", "parallel"),
        name="attention",
    )(q, k_nope, k_pe, v_t, gate, *[w for w, _ in casts])


def _out_proj_kernel(yc_ref, ya_ref, w_ref, x_ref, g_ref, o_ref, *, final_norm, tn):
    kc = yc_ref.shape[1]
    d = o_ref.shape[1]
    yc = yc_ref[...]
    ya = ya_ref[...]
    ss = None
    for t in range(d // tn):
        cols = slice(t * tn, (t + 1) * tn)
        h = x_ref[:, cols] + (_dot(yc, w_ref[:kc, cols]) + _dot(ya, w_ref[kc:, cols]))
        o_ref[:, cols] = h.astype(o_ref.dtype)
        if final_norm:
            part = jnp.sum(h * h, axis=-1, keepdims=True)
            ss = part if ss is None else ss + part
    if final_norm:
        inv = lax.rsqrt(ss / d + NORM_EPS)
        for t in range(d // tn):
            cols = slice(t * tn, (t + 1) * tn)
            o_ref[:, cols] = (o_ref[:, cols] * inv) * g_ref[:, cols]


def _out_proj(y_conv, y_attn, w_o, x, g, final_norm, tm, tn):
    m, d = x.shape
    kc = y_conv.shape[1]
    ka = y_attn.shape[1]
    return pl.pallas_call(
        functools.partial(_out_proj_kernel, final_norm=final_norm, tn=tn),
        grid=(m // tm,),
        in_specs=[pl.BlockSpec((tm, kc), lambda i: (i, 0)),
                  pl.BlockSpec((tm, ka), lambda i: (i, 0)),
                  pl.BlockSpec((kc + ka, d), lambda i: (0, 0), pipeline_mode=pl.Buffered(1)),
                  pl.BlockSpec((tm, d), lambda i: (i, 0)),
                  pl.BlockSpec((1, d), lambda i: (0, 0))],
        out_specs=pl.BlockSpec((tm, d), lambda i: (i, 0)),
        out_shape=jax.ShapeDtypeStruct((m, d), x.dtype),
        compiler_params=pltpu.CompilerParams(dimension_semantics=("parallel",),
                                             vmem_limit_bytes=BIG_VMEM_LIMIT_BYTES),
        name="out_proj",
    )(y_conv, y_attn, w_o, x, g.reshape(1, d))


def _rope_tables(seq):
    pos = jnp.arange(seq, dtype=F32)
    inv_freq = 1.0 / (ROPE_THETA ** (jnp.arange(0, ROPE_DIM, 2, dtype=F32) / ROPE_DIM))
    ang = pos[:, None] * inv_freq[None, :]
    zeros = jnp.zeros((seq, LANES - ROPE_DIM), F32)
    cos = jnp.concatenate([jnp.cos(ang), jnp.cos(ang), zeros], axis=-1)
    sin = jnp.concatenate([jnp.sin(ang), jnp.sin(ang), zeros], axis=-1)
    return cos, sin


def _tile(n, pref):
    return pref if n % pref == 0 else n


def _layer(h, g_in, w_in, conv_w, q_norm_g, w_uq, kv_norm_g, w_ukv, w_out, g_out,
           final_norm, cos, sin, batch, seq):
    m, d = h.shape
    dc = conv_w.shape[1]
    o = 4 * dc

    w_in_t = w_in.T
    kr0 = o + Q_LORA + KV_LORA
    n_lat = Q_LORA + KV_LORA + LANES
    w_lat_t = _cast_rows(w_in_t, o, n_lat, n_lat // 2, d // 2, "cast_w_latent")
    w_q = _q_weights(w_uq)
    w_kv = w_ukv.astype(BF16)

    blk = _tile(seq, ATTN_BLOCK)
    xn, cq, ckv, k_pe, w_z_t = _latents(h, g_in, w_lat_t, q_norm_g, kv_norm_g, cos, sin,
                                        w_in_t, kr0 + ROPE_DIM, w_in_t.shape[0] - kr0 - ROPE_DIM,
                                        seq, _tile(seq, 512))
    gate = _gate(xn, w_z_t, _tile(m, 1024), _tile(w_z_t.shape[0], 1024))
    q = _q_up(cq, w_q, cos, sin, seq, _tile(seq, 512))
    k_nope, v_t = _kv_up(ckv, w_kv, _tile(seq, 512), blk)
    y_attn, w_conv_t, w_o = _attention(q, k_nope, k_pe, v_t, gate, batch, seq, blk,
                                       [(w_in_t, o), (w_out, w_out.shape[0])])
    y_conv = _conv_group(xn, w_conv_t, conv_w, seq, _tile(seq, 1024), _tile(dc, 256))
    return _out_proj(y_conv, y_attn, w_o, h, g_out, final_norm, _tile(m, 256), _tile(d, 512))


def kernel(x, g_in, w_in, conv_w, q_norm_g, w_uq, kv_norm_g, w_ukv, w_out, g_final):
    batch, seq, d = x.shape
    depth = g_in.shape[0]
    cos, sin = _rope_tables(seq)
    h = x.reshape(batch * seq, d)
    for l in range(depth):
        h = _layer(h, g_in[l], w_in[l], conv_w[l], q_norm_g[l], w_uq[l], kv_norm_g[l],
                   w_ukv[l], w_out[l], g_final, l == depth - 1, cos, sin, batch, seq)
    return h.reshape(batch, seq, d)
```

```python
import functools

import jax
import jax.numpy as jnp
import numpy as np
from jax import lax
from jax.experimental import pallas as pl
from jax.experimental.pallas import tpu as pltpu

CHUNK = 64
CONV_WIDTH = 3
N_HEADS = 16
V_HEAD_DIM = 128
QK_NOPE_DIM = 128
ROPE_DIM = 64
ROPE_HALF = ROPE_DIM // 2
Q_LORA = 1024
KV_LORA = 512
ROPE_THETA = 10000.0
NORM_EPS = 1e-6
ATTN_SCALE = (QK_NOPE_DIM + ROPE_DIM) ** -0.5
LOG2_E = float(np.log2(np.e))
Q_SCALE = ATTN_SCALE * LOG2_E

LANES = 128
SUBLANES = 8
QK_PAD_DIM = 2 * LANES
ATTN_BLOCK = 512
SCORE_LOOKAHEAD = 2
VMEM_LIMIT_BYTES = 56 * 1024 * 1024
BIG_VMEM_LIMIT_BYTES = 60 * 1024 * 1024

BF16 = jnp.bfloat16
F32 = jnp.float32


def _params(*semantics):
    return pltpu.CompilerParams(dimension_semantics=semantics,
                                vmem_limit_bytes=VMEM_LIMIT_BYTES)


def _dot(a, b):
    return jnp.dot(a, b, preferred_element_type=F32)


def _dot_nt(a, b):
    return lax.dot_general(a, b, (((1,), (1,)), ((), ())), preferred_element_type=F32)


def _rmsnorm_rows(x, g):
    ms = jnp.mean(x * x, axis=-1, keepdims=True)
    return (x * lax.rsqrt(ms + NORM_EPS)) * g


def _silu(z):
    return z * (0.5 * (jnp.tanh(0.5 * z) + 1.0))


def _cast_kernel(w_ref, o_ref):
    o_ref[...] = w_ref[...].astype(o_ref.dtype)


def _cast_rows(w, row0, n_rows, tr, tc, name):
    cols = w.shape[1]
    assert row0 % SUBLANES == 0 and n_rows % tr == 0 and cols % tc == 0
    return pl.pallas_call(
        _cast_kernel,
        grid=(n_rows // tr, cols // tc),
        in_specs=[pl.BlockSpec((pl.Element(tr), pl.Element(tc)),
                               lambda r, c: (pl.multiple_of(row0 + r * tr, SUBLANES),
                                             pl.multiple_of(c * tc, LANES)))],
        out_specs=pl.BlockSpec((tr, tc), lambda r, c: (r, c)),
        out_shape=jax.ShapeDtypeStruct((n_rows, cols), BF16),
        compiler_params=_params("parallel", "parallel"),
        name=name,
    )(w)


def _conv_group_kernel(xn_ref, wb_ref, wc_ref, wh_ref, wz_ref, cw_ref, y_ref, carry_ref,
                       *, tiles_per_seq):
    i = pl.program_id(0)
    j = pl.program_id(1)
    xn = xn_ref[...]
    u = _dot_nt(xn, wc_ref[...]) * _dot_nt(xn, wh_ref[...])
    tm, tc = u.shape

    @pl.when(i % tiles_per_seq == 0)
    def _():
        carry_ref[j] = jnp.zeros((SUBLANES, tc), F32)

    prev = carry_ref[j]
    carry_ref[j] = u[tm - SUBLANES:, :]

    row = lax.broadcasted_iota(jnp.int32, (SUBLANES, tc), 0)

    def shifted(k):
        body = pltpu.roll(u, k, 0)
        head = jnp.where(row < k, pltpu.roll(prev, k, 0), body[:SUBLANES, :])
        return jnp.concatenate([head, body[SUBLANES:, :]], axis=0)

    cw = cw_ref[...]
    conv = cw[0:1, :] * shifted(2) + cw[1:2, :] * shifted(1) + cw[2:3, :] * u
    y = (_dot_nt(xn, wb_ref[...]) * conv) * _silu(_dot_nt(xn, wz_ref[...]))
    y_ref[...] = y.astype(y_ref.dtype)


def _conv_group(xn, w_t, conv_w, seq, tm, tc):
    m, d = xn.shape
    dc = conv_w.shape[1]
    nj = dc // tc

    def w_spec(k):
        return pl.BlockSpec((tc, d), lambda i, j: (j + k * nj, 0))

    return pl.pallas_call(
        functools.partial(_conv_group_kernel, tiles_per_seq=seq // tm),
        grid=(m // tm, nj),
        in_specs=[pl.BlockSpec((tm, d), lambda i, j: (i, 0)),
                  w_spec(0), w_spec(1), w_spec(2), w_spec(3),
                  pl.BlockSpec((CONV_WIDTH, tc), lambda i, j: (0, j))],
        out_specs=pl.BlockSpec((tm, tc), lambda i, j: (i, j)),
        out_shape=jax.ShapeDtypeStruct((m, dc), BF16),
        scratch_shapes=[pltpu.VMEM((nj, SUBLANES, tc), F32)],
        compiler_params=_params("arbitrary", "arbitrary"),
        name="conv_group",
    )(xn, w_t, w_t, w_t, w_t, conv_w)


def _rope_pair(t, cos, sin):
    return t * cos + pltpu.roll(t, ROPE_DIM, 1) * sin


def _rope_low_half(t, cos, sin):
    lane = lax.broadcasted_iota(jnp.int32, t.shape, 1)
    partner = jnp.where(lane < ROPE_HALF,
                        -pltpu.roll(t, LANES - ROPE_HALF, 1),
                        pltpu.roll(t, ROPE_HALF, 1))
    return jnp.where(lane < ROPE_DIM, t * cos + partner * sin, 0.0)


def _latent_kernel(x_ref, g_ref, w_ref, gq_ref, gkv_ref, cos_ref, sin_ref, cast_in_ref,
                   xn_ref, cq_ref, ckv_ref, kpe_ref, cast_out_ref):
    cast_out_ref[...] = cast_in_ref[...].astype(cast_out_ref.dtype)
    g = g_ref[...]
    c = _dot_nt((x_ref[...] * g).astype(BF16), w_ref[...])
    x = x_ref[...]
    inv = lax.rsqrt(jnp.mean(x * x, axis=-1, keepdims=True) + NORM_EPS)
    xn_ref[...] = ((x * inv) * g).astype(xn_ref.dtype)
    c = c * inv
    cq_ref[...] = _rmsnorm_rows(c[:, :Q_LORA], gq_ref[...]).astype(cq_ref.dtype)
    ckv_ref[...] = _rmsnorm_rows(c[:, Q_LORA:Q_LORA + KV_LORA], gkv_ref[...]).astype(ckv_ref.dtype)
    kpe_ref[...] = _rope_low_half(c[:, Q_LORA + KV_LORA:], cos_ref[...],
                                  sin_ref[...]).astype(kpe_ref.dtype)


def _latents(x, g_in, w_t, gq, gkv, cos, sin, w_cast, cast_row0, cast_rows, seq, tm):
    m, d = x.shape
    tps = seq // tm
    n = w_t.shape[0]
    assert n == Q_LORA + KV_LORA + LANES
    cr = cast_rows // (m // tm)
    assert cr * (m // tm) == cast_rows and cr % CAST_PIECE_ROWS == 0 and cast_row0 % SUBLANES == 0
    return pl.pallas_call(
        _latent_kernel,
        grid=(m // tm,),
        in_specs=[pl.BlockSpec((tm, d), lambda i: (i, 0)),
                  pl.BlockSpec((1, d), lambda i: (0, 0)),
                  pl.BlockSpec((n, d), lambda i: (0, 0), pipeline_mode=pl.Buffered(1)),
                  pl.BlockSpec((1, Q_LORA), lambda i: (0, 0)),
                  pl.BlockSpec((1, KV_LORA), lambda i: (0, 0)),
                  pl.BlockSpec((tm, LANES), lambda i: (i % tps, 0)),
                  pl.BlockSpec((tm, LANES), lambda i: (i % tps, 0)),
                  pl.BlockSpec((pl.Element(cr), pl.Element(w_cast.shape[1])),
                               lambda i: (pl.multiple_of(cast_row0 + i * cr, SUBLANES), 0))],
        out_specs=[pl.BlockSpec((tm, d), lambda i: (i, 0)),
                   pl.BlockSpec((tm, Q_LORA), lambda i: (i, 0)),
                   pl.BlockSpec((tm, KV_LORA), lambda i: (i, 0)),
                   pl.BlockSpec((tm, LANES), lambda i: (i, 0)),
                   pl.BlockSpec((cr, w_cast.shape[1]), lambda i: (i, 0))],
        out_shape=[jax.ShapeDtypeStruct((m, d), BF16),
                   jax.ShapeDtypeStruct((m, Q_LORA), BF16),
                   jax.ShapeDtypeStruct((m, KV_LORA), BF16),
                   jax.ShapeDtypeStruct((m, LANES), BF16),
                   jax.ShapeDtypeStruct((cast_rows, w_cast.shape[1]), BF16)],
        compiler_params=pltpu.CompilerParams(dimension_semantics=("parallel",),
                                             vmem_limit_bytes=BIG_VMEM_LIMIT_BYTES),
        name="latents",
    )(x, g_in.reshape(1, d), w_t, gq.reshape(1, -1), gkv.reshape(1, -1), cos, sin, w_cast)


def _gate_kernel(xn_ref, w_ref, o_ref):
    o_ref[...] = _silu(_dot_nt(xn_ref[...], w_ref[...])).astype(o_ref.dtype)


def _gate(xn, w_z_t, tm, tn):
    m, d = xn.shape
    n = w_z_t.shape[0]
    return pl.pallas_call(
        _gate_kernel,
        grid=(m // tm, n // tn),
        in_specs=[pl.BlockSpec((tm, d), lambda i, j: (i, 0)),
                  pl.BlockSpec((tn, d), lambda i, j: (j, 0))],
        out_specs=pl.BlockSpec((tm, tn), lambda i, j: (i, j)),
        out_shape=jax.ShapeDtypeStruct((m, n), BF16),
        compiler_params=_params("parallel", "parallel"),
        name="attn_gate",
    )(xn, w_z_t)


def _q_weight_kernel(w_ref, o_ref):
    lane = lax.broadcasted_iota(jnp.int32, (w_ref.shape[0], LANES), 1)

    def rope_tile(u):
        swapped = jnp.where(lane < LANES - ROPE_HALF, -pltpu.roll(u, ROPE_HALF, 1),
                            pltpu.roll(u, LANES - ROPE_HALF, 1))
        return jnp.where(lane < ROPE_DIM, u, swapped)

    t0 = w_ref[:, :LANES]
    t1 = w_ref[:, LANES:2 * LANES]
    t2 = w_ref[:, 2 * LANES:]
    t2r = pltpu.roll(t2, ROPE_DIM, 1)
    o_ref[:, :LANES] = t0.astype(o_ref.dtype)
    o_ref[:, LANES:2 * LANES] = rope_tile(t1).astype(o_ref.dtype)
    o_ref[:, 2 * LANES:3 * LANES] = jnp.where(lane < ROPE_DIM, pltpu.roll(t1, ROPE_DIM, 1),
                                              t2r).astype(o_ref.dtype)
    o_ref[:, 3 * LANES:] = rope_tile(t2r).astype(o_ref.dtype)


def _q_weights(w_uq):
    k = w_uq.shape[0]
    pair_in = 2 * (QK_NOPE_DIM + ROPE_DIM)
    return pl.pallas_call(
        _q_weight_kernel,
        grid=(N_HEADS // 2,),
        in_specs=[pl.BlockSpec((k, pair_in), lambda j: (0, j))],
        out_specs=pl.BlockSpec((k, 2 * QK_PAD_DIM), lambda j: (0, j)),
        out_shape=jax.ShapeDtypeStruct((k, N_HEADS * QK_PAD_DIM), BF16),
        compiler_params=_params("parallel"),
        name="q_weights",
    )(w_uq)


def _q_up_kernel(cq_ref, w_ref, cos_ref, sin_ref, q_ref):
    r = _dot(cq_ref[...], w_ref[...])
    cos = cos_ref[...]
    sin = sin_ref[...]
    for h in range(N_HEADS):
        lo = h * QK_PAD_DIM
        nope = r[:, lo:lo + LANES] * Q_SCALE
        pe = _rope_pair(r[:, lo + LANES:lo + QK_PAD_DIM], cos, sin) * Q_SCALE
        q_ref[:, lo:lo + LANES] = nope.astype(q_ref.dtype)
        q_ref[:, lo + LANES:lo + QK_PAD_DIM] = pe.astype(q_ref.dtype)


def _q_up(cq, w_q, cos, sin, seq, tm):
    m, k = cq.shape
    n = w_q.shape[1]
    tps = seq // tm
    return pl.pallas_call(
        _q_up_kernel,
        grid=(m // tm,),
        in_specs=[pl.BlockSpec((tm, k), lambda i: (i, 0)),
                  pl.BlockSpec((k, n), lambda i: (0, 0)),
                  pl.BlockSpec((tm, LANES), lambda i: (i % tps, 0)),
                  pl.BlockSpec((tm, LANES), lambda i: (i % tps, 0))],
        out_specs=pl.BlockSpec((tm, n), lambda i: (i, 0)),
        out_shape=jax.ShapeDtypeStruct((m, n), BF16),
        compiler_params=_params("parallel"),
        name="q_up",
    )(cq, w_q, cos, sin)


def _kv_up_kernel(ckv_ref, w_ref, k_ref, vt_ref, *, blk):
    kv = _dot(ckv_ref[...], w_ref[...])
    for h in range(N_HEADS):
        lo = h * (QK_NOPE_DIM + V_HEAD_DIM)
        k_ref[:, h * QK_NOPE_DIM:(h + 1) * QK_NOPE_DIM] = (
            kv[:, lo:lo + QK_NOPE_DIM].astype(k_ref.dtype))
        v = kv[:, lo + QK_NOPE_DIM:lo + QK_NOPE_DIM + V_HEAD_DIM]
        for r in range(v.shape[0] // blk):
            vt_ref[r, h * V_HEAD_DIM:(h + 1) * V_HEAD_DIM, :] = (
                v[r * blk:(r + 1) * blk, :].T.astype(vt_ref.dtype))


def _kv_up(ckv, w_kv, tm, blk):
    m, k = ckv.shape
    n = N_HEADS * V_HEAD_DIM
    per = tm // blk
    return pl.pallas_call(
        functools.partial(_kv_up_kernel, blk=blk),
        grid=(m // tm,),
        in_specs=[pl.BlockSpec((tm, k), lambda i: (i, 0)),
                  pl.BlockSpec(w_kv.shape, lambda i: (0, 0))],
        out_specs=[pl.BlockSpec((tm, N_HEADS * QK_NOPE_DIM), lambda i: (i, 0)),
                   pl.BlockSpec((per, n, blk), lambda i: (i, 0, 0))],
        out_shape=[jax.ShapeDtypeStruct((m, N_HEADS * QK_NOPE_DIM), BF16),
                   jax.ShapeDtypeStruct((m // blk, n, blk), BF16)],
        compiler_params=_params("parallel"),
        name="kv_up",
    )(ckv, w_kv)


CAST_PIECE_ROWS = 16


def _attention_kernel(q_ref, kn_ref, kpe_ref, vt_ref, gate_ref, *rest, blk, n_casts):
    cast_in = rest[:n_casts]
    o_ref = rest[n_casts]
    cast_out = rest[n_casts + 1:2 * n_casts + 1]
    kfull_ref = rest[2 * n_casts + 1]
    cast_pieces = [(src, dst, r) for src, dst in zip(cast_in, cast_out)
                   for r in range(0, src.shape[0], CAST_PIECE_ROWS)]
    seq = q_ref.shape[0]
    nblk = seq // blk
    kfull_ref[:, :LANES] = kn_ref[...]
    kfull_ref[:, LANES:] = kpe_ref[...]

    key_chunk = lax.broadcasted_iota(jnp.int32, (blk, blk), 0) // CHUNK
    qry_chunk = lax.broadcasted_iota(jnp.int32, (blk, blk), 1) // CHUNK
    visible = key_chunk <= qry_chunk

    pairs = [(qi, kj) for qi in range(nblk) for kj in range(qi + 1)]
    half = blk // 2

    def scores(qi, kj):
        k0, q0 = kj * blk, qi * blk
        if kj != qi:
            return _dot_nt(kfull_ref[k0:k0 + blk, :], q_ref[q0:q0 + blk, :])
        s_lo = _dot_nt(kfull_ref[k0:k0 + half, :], q_ref[q0:q0 + half, :])
        s_hi = _dot_nt(kfull_ref[k0:k0 + blk, :], q_ref[q0 + half:q0 + blk, :])
        return (jnp.where(visible[:half, :half], s_lo, -jnp.inf),
                jnp.where(visible[:, half:], s_hi, -jnp.inf))

    def softmax_step(state, s, vt):
        m_blk = jnp.max(s, axis=0, keepdims=True)
        if state is None:
            p = jnp.exp2(s - m_blk)
            return m_blk, jnp.sum(p, axis=0, keepdims=True), _dot(vt, p.astype(BF16))
        m_run, l_run, acc = state
        m_new = jnp.maximum(m_run, m_blk)
        alpha = jnp.exp2(m_run - m_new)
        p = jnp.exp2(s - m_new)
        return (m_new, alpha * l_run + jnp.sum(p, axis=0, keepdims=True),
                alpha * acc + _dot(vt, p.astype(BF16)))

    state = None
    ahead = [scores(*pairs[n]) for n in range(min(SCORE_LOOKAHEAD, len(pairs)))]
    for n, (qi, kj) in enumerate(pairs):
        s = ahead.pop(0)
        if n + SCORE_LOOKAHEAD < len(pairs):
            ahead.append(scores(*pairs[n + SCORE_LOOKAHEAD]))
        if kj != qi:
            state = softmax_step(state, s, vt_ref[kj])
        else:
            lo_state = None if state is None else tuple(t[:, :half] for t in state)
            hi_state = None if state is None else tuple(t[:, half:] for t in state)
            _, l_lo, acc_lo = softmax_step(lo_state, s[0], vt_ref[kj, :, :half])
            _, l_hi, acc_hi = softmax_step(hi_state, s[1], vt_ref[kj])
            out = jnp.concatenate([acc_lo / l_lo, acc_hi / l_hi], axis=1).T
            gate = gate_ref[qi * blk:(qi + 1) * blk, :].astype(F32)
            o_ref[qi * blk:(qi + 1) * blk, :] = (out * gate).astype(o_ref.dtype)
            state = None
        lo = n * len(cast_pieces) // len(pairs)
        hi = (n + 1) * len(cast_pieces) // len(pairs)
        for src, dst, r in cast_pieces[lo:hi]:
            dst[r:r + CAST_PIECE_ROWS, :] = src[r:r + CAST_PIECE_ROWS, :].astype(dst.dtype)


def _attention(q, k_nope, k_pe, v_t, gate, batch, seq, blk, casts):
    m = q.shape[0]
    nblk = seq // blk
    steps = batch * N_HEADS
    cast_specs, cast_shapes = [], []
    for w, n_rows in casts:
        rows = n_rows // steps
        assert rows * steps == n_rows and rows % CAST_PIECE_ROWS == 0
        cast_specs.append(pl.BlockSpec((rows, w.shape[1]), lambda b, h: (b * N_HEADS + h, 0)))
        cast_shapes.append(jax.ShapeDtypeStruct((n_rows, w.shape[1]), BF16))
    return pl.pallas_call(
        functools.partial(_attention_kernel, blk=blk, n_casts=len(casts)),
        grid=(batch, N_HEADS),
        in_specs=[pl.BlockSpec((seq, QK_PAD_DIM), lambda b, h: (b, h)),
                  pl.BlockSpec((seq, LANES), lambda b, h: (b, h)),
                  pl.BlockSpec((seq, LANES), lambda b, h: (b, 0)),
                  pl.BlockSpec((nblk, V_HEAD_DIM, blk), lambda b, h: (b, h, 0)),
                  pl.BlockSpec((seq, LANES), lambda b, h: (b, h))] + cast_specs,
        out_specs=[pl.BlockSpec((seq, V_HEAD_DIM), lambda b, h: (b, h))] + cast_specs,
        out_shape=[jax.ShapeDtypeStruct((m, N_HEADS * V_HEAD_DIM), BF16)] + cast_shapes,
        scratch_shapes=[pltpu.VMEM((seq, QK_PAD_DIM), BF16)],
        compiler_params=_params("parallel", "parallel"),
        name="attention",
    )(q, k_nope, k_pe, v_t, gate, *[w for w, _ in casts])


def _out_proj_kernel(yc_ref, ya_ref, w_ref, x_ref, g_ref, o_ref, *, final_norm, tn):
    kc = yc_ref.shape[1]
    d = o_ref.shape[1]
    yc = yc_ref[...]
    ya = ya_ref[...]
    ss = None
    for t in range(d // tn):
        cols = slice(t * tn, (t + 1) * tn)
        h = x_ref[:, cols] + (_dot(yc, w_ref[:kc, cols]) + _dot(ya, w_ref[kc:, cols]))
        o_ref[:, cols] = h.astype(o_ref.dtype)
        if final_norm:
            part = jnp.sum(h * h, axis=-1, keepdims=True)
            ss = part if ss is None else ss + part
    if final_norm:
        inv = lax.rsqrt(ss / d + NORM_EPS)
        for t in range(d // tn):
            cols = slice(t * tn, (t + 1) * tn)
            o_ref[:, cols] = (o_ref[:, cols] * inv) * g_ref[:, cols]


def _out_proj(y_conv, y_attn, w_o, x, g, final_norm, tm, tn):
    m, d = x.shape
    kc = y_conv.shape[1]
    ka = y_attn.shape[1]
    return pl.pallas_call(
        functools.partial(_out_proj_kernel, final_norm=final_norm, tn=tn),
        grid=(m // tm,),
        in_specs=[pl.BlockSpec((tm, kc), lambda i: (i, 0)),
                  pl.BlockSpec((tm, ka), lambda i: (i, 0)),
                  pl.BlockSpec((kc + ka, d), lambda i: (0, 0), pipeline_mode=pl.Buffered(1)),
                  pl.BlockSpec((tm, d), lambda i: (i, 0)),
                  pl.BlockSpec((1, d), lambda i: (0, 0))],
        out_specs=pl.BlockSpec((tm, d), lambda i: (i, 0)),
        out_shape=jax.ShapeDtypeStruct((m, d), x.dtype),
        compiler_params=pltpu.CompilerParams(dimension_semantics=("parallel",),
                                             vmem_limit_bytes=BIG_VMEM_LIMIT_BYTES),
        name="out_proj",
    )(y_conv, y_attn, w_o, x, g.reshape(1, d))


def _rope_tables(seq):
    pos = jnp.arange(seq, dtype=F32)
    inv_freq = 1.0 / (ROPE_THETA ** (jnp.arange(0, ROPE_DIM, 2, dtype=F32) / ROPE_DIM))
    ang = pos[:, None] * inv_freq[None, :]
    zeros = jnp.zeros((seq, LANES - ROPE_DIM), F32)
    cos = jnp.concatenate([jnp.cos(ang), jnp.cos(ang), zeros], axis=-1)
    sin = jnp.concatenate([jnp.sin(ang), jnp.sin(ang), zeros], axis=-1)
    return cos, sin


def _tile(n, pref):
    return pref if n % pref == 0 else n


def _layer(h, g_in, w_in, conv_w, q_norm_g, w_uq, kv_norm_g, w_ukv, w_out, g_out,
           final_norm, cos, sin, batch, seq):
    m, d = h.shape
    dc = conv_w.shape[1]
    o = 4 * dc

    w_in_t = w_in.T
    kr0 = o + Q_LORA + KV_LORA
    n_lat = Q_LORA + KV_LORA + LANES
    w_lat_t = _cast_rows(w_in_t, o, n_lat, n_lat // 2, d // 2, "cast_w_latent")
    w_q = _q_weights(w_uq)
    w_kv = w_ukv.astype(BF16)

    blk = _tile(seq, ATTN_BLOCK)
    xn, cq, ckv, k_pe, w_z_t = _latents(h, g_in, w_lat_t, q_norm_g, kv_norm_g, cos, sin,
                                        w_in_t, kr0 + ROPE_DIM, w_in_t.shape[0] - kr0 - ROPE_DIM,
                                        seq, _tile(seq, 512))
    gate = _gate(xn, w_z_t, _tile(m, 1024), _tile(w_z_t.shape[0], 1024))
    q = _q_up(cq, w_q, cos, sin, seq, _tile(seq, 512))
    k_nope, v_t = _kv_up(ckv, w_kv, _tile(seq, 512), blk)
    y_attn, w_conv_t, w_o = _attention(q, k_nope, k_pe, v_t, gate, batch, seq, blk,
                                       [(w_in_t, o), (w_out, w_out.shape[0])])
    y_conv = _conv_group(xn, w_conv_t, conv_w, seq, _tile(seq, 1024), _tile(dc, 256))
    return _out_proj(y_conv, y_attn, w_o, h, g_out, final_norm, _tile(m, 256), _tile(d, 512))


def kernel(x, g_in, w_in, conv_w, q_norm_g, w_uq, kv_norm_g, w_ukv, w_out, g_final):
    batch, seq, d = x.shape
    depth = g_in.shape[0]
    cos, sin = _rope_tables(seq)
    h = x.reshape(batch * seq, d)
    for l in range(depth):
        h = _layer(h, g_in[l], w_in[l], conv_w[l], q_norm_g[l], w_uq[l], kv_norm_g[l],
                   w_ukv[l], w_out[l], g_final, l == depth - 1, cos, sin, batch, seq)
    return h.reshape(batch, seq, d)
```

```python
import functools

import jax
import jax.numpy as jnp
import numpy as np
from jax import lax
from jax.experimental import pallas as pl
from jax.experimental.pallas import tpu as pltpu

CHUNK = 64
CONV_WIDTH = 3
N_HEADS = 16
V_HEAD_DIM = 128
QK_NOPE_DIM = 128
ROPE_DIM = 64
ROPE_HALF = ROPE_DIM // 2
Q_LORA = 1024
KV_LORA = 512
ROPE_THETA = 10000.0
NORM_EPS = 1e-6
ATTN_SCALE = (QK_NOPE_DIM + ROPE_DIM) ** -0.5
LOG2_E = float(np.log2(np.e))
Q_SCALE = ATTN_SCALE * LOG2_E

LANES = 128
SUBLANES = 8
QK_PAD_DIM = 2 * LANES
V_AUG_DIM = V_HEAD_DIM + 16
ATTN_BLOCK = 512
SCORE_LOOKAHEAD = 2
VMEM_LIMIT_BYTES = 56 * 1024 * 1024
BIG_VMEM_LIMIT_BYTES = 60 * 1024 * 1024

BF16 = jnp.bfloat16
F32 = jnp.float32


def _params(*semantics):
    return pltpu.CompilerParams(dimension_semantics=semantics,
                                vmem_limit_bytes=VMEM_LIMIT_BYTES)


def _dot(a, b):
    return jnp.dot(a, b, preferred_element_type=F32)


def _dot_nt(a, b):
    return lax.dot_general(a, b, (((1,), (1,)), ((), ())), preferred_element_type=F32)


def _rmsnorm_rows(x, g):
    ms = jnp.mean(x * x, axis=-1, keepdims=True)
    return (x * lax.rsqrt(ms + NORM_EPS)) * g


def _silu(z):
    return z * (0.5 * (jnp.tanh(0.5 * z) + 1.0))


def _cast_kernel(w_ref, o_ref):
    o_ref[...] = w_ref[...].astype(o_ref.dtype)


def _cast_rows(w, row0, n_rows, tr, tc, name):
    cols = w.shape[1]
    assert row0 % SUBLANES == 0 and n_rows % tr == 0 and cols % tc == 0
    return pl.pallas_call(
        _cast_kernel,
        grid=(n_rows // tr, cols // tc),
        in_specs=[pl.BlockSpec((pl.Element(tr), pl.Element(tc)),
                               lambda r, c: (pl.multiple_of(row0 + r * tr, SUBLANES),
                                             pl.multiple_of(c * tc, LANES)))],
        out_specs=pl.BlockSpec((tr, tc), lambda r, c: (r, c)),
        out_shape=jax.ShapeDtypeStruct((n_rows, cols), BF16),
        compiler_params=_params("parallel", "parallel"),
        name=name,
    )(w)


def _conv_group_kernel(xn_ref, wb_ref, wc_ref, wh_ref, wz_ref, cw_ref, y_ref, carry_ref,
                       *, tiles_per_seq):
    i = pl.program_id(0)
    j = pl.program_id(1)
    xn = xn_ref[...]
    u = _dot_nt(xn, wc_ref[...]) * _dot_nt(xn, wh_ref[...])
    tm, tc = u.shape

    @pl.when(i % tiles_per_seq == 0)
    def _():
        carry_ref[j] = jnp.zeros((SUBLANES, tc), F32)

    prev = carry_ref[j]
    carry_ref[j] = u[tm - SUBLANES:, :]

    row = lax.broadcasted_iota(jnp.int32, (SUBLANES, tc), 0)

    def shifted(k):
        body = pltpu.roll(u, k, 0)
        head = jnp.where(row < k, pltpu.roll(prev, k, 0), body[:SUBLANES, :])
        return jnp.concatenate([head, body[SUBLANES:, :]], axis=0)

    cw = cw_ref[...]
    conv = cw[0:1, :] * shifted(2) + cw[1:2, :] * shifted(1) + cw[2:3, :] * u
    y = (_dot_nt(xn, wb_ref[...]) * conv) * _silu(_dot_nt(xn, wz_ref[...]))
    y_ref[...] = y.astype(y_ref.dtype)


def _conv_group(xn, w_t, conv_w, seq, tm, tc):
    m, d = xn.shape
    dc = conv_w.shape[1]
    nj = dc // tc

    def w_spec(k):
        return pl.BlockSpec((tc, d), lambda i, j: (j + k * nj, 0))

    return pl.pallas_call(
        functools.partial(_conv_group_kernel, tiles_per_seq=seq // tm),
        grid=(m // tm, nj),
        in_specs=[pl.BlockSpec((tm, d), lambda i, j: (i, 0)),
                  w_spec(0), w_spec(1), w_spec(2), w_spec(3),
                  pl.BlockSpec((CONV_WIDTH, tc), lambda i, j: (0, j))],
        out_specs=pl.BlockSpec((tm, tc), lambda i, j: (i, j)),
        out_shape=jax.ShapeDtypeStruct((m, dc), BF16),
        scratch_shapes=[pltpu.VMEM((nj, SUBLANES, tc), F32)],
        compiler_params=_params("arbitrary", "arbitrary"),
        name="conv_group",
    )(xn, w_t, w_t, w_t, w_t, conv_w)


def _rope_pair(t, cos, sin):
    return t * cos + pltpu.roll(t, ROPE_DIM, 1) * sin


def _rope_low_half(t, cos, sin):
    lane = lax.broadcasted_iota(jnp.int32, t.shape, 1)
    partner = jnp.where(lane < ROPE_HALF,
                        -pltpu.roll(t, LANES - ROPE_HALF, 1),
                        pltpu.roll(t, ROPE_HALF, 1))
    return jnp.where(lane < ROPE_DIM, t * cos + partner * sin, 0.0)


def _latent_kernel(x_ref, g_ref, w_ref, gq_ref, gkv_ref, cos_ref, sin_ref, cast_in_ref,
                   xn_ref, cq_ref, ckv_ref, kpe_ref, cast_out_ref):
    cast_out_ref[...] = cast_in_ref[...].astype(cast_out_ref.dtype)
    g = g_ref[...]
    c = _dot_nt((x_ref[...] * g).astype(BF16), w_ref[...])
    x = x_ref[...]
    inv = lax.rsqrt(jnp.mean(x * x, axis=-1, keepdims=True) + NORM_EPS)
    xn_ref[...] = ((x * inv) * g).astype(xn_ref.dtype)
    c = c * inv
    cq_ref[...] = _rmsnorm_rows(c[:, :Q_LORA], gq_ref[...]).astype(cq_ref.dtype)
    ckv_ref[...] = _rmsnorm_rows(c[:, Q_LORA:Q_LORA + KV_LORA], gkv_ref[...]).astype(ckv_ref.dtype)
    kpe_ref[...] = _rope_low_half(c[:, Q_LORA + KV_LORA:], cos_ref[...],
                                  sin_ref[...]).astype(kpe_ref.dtype)


def _latents(x, g_in, w_t, gq, gkv, cos, sin, w_cast, cast_row0, cast_rows, seq, tm):
    m, d = x.shape
    tps = seq // tm
    n = w_t.shape[0]
    assert n == Q_LORA + KV_LORA + LANES
    cr = cast_rows // (m // tm)
    assert cr * (m // tm) == cast_rows and cr % CAST_PIECE_ROWS == 0 and cast_row0 % SUBLANES == 0
    return pl.pallas_call(
        _latent_kernel,
        grid=(m // tm,),
        in_specs=[pl.BlockSpec((tm, d), lambda i: (i, 0)),
                  pl.BlockSpec((1, d), lambda i: (0, 0)),
                  pl.BlockSpec((n, d), lambda i: (0, 0), pipeline_mode=pl.Buffered(1)),
                  pl.BlockSpec((1, Q_LORA), lambda i: (0, 0)),
                  pl.BlockSpec((1, KV_LORA), lambda i: (0, 0)),
                  pl.BlockSpec((tm, LANES), lambda i: (i % tps, 0)),
                  pl.BlockSpec((tm, LANES), lambda i: (i % tps, 0)),
                  pl.BlockSpec((pl.Element(cr), pl.Element(w_cast.shape[1])),
                               lambda i: (pl.multiple_of(cast_row0 + i * cr, SUBLANES), 0))],
        out_specs=[pl.BlockSpec((tm, d), lambda i: (i, 0)),
                   pl.BlockSpec((tm, Q_LORA), lambda i: (i, 0)),
                   pl.BlockSpec((tm, KV_LORA), lambda i: (i, 0)),
                   pl.BlockSpec((tm, LANES), lambda i: (i, 0)),
                   pl.BlockSpec((cr, w_cast.shape[1]), lambda i: (i, 0))],
        out_shape=[jax.ShapeDtypeStruct((m, d), BF16),
                   jax.ShapeDtypeStruct((m, Q_LORA), BF16),
                   jax.ShapeDtypeStruct((m, KV_LORA), BF16),
                   jax.ShapeDtypeStruct((m, LANES), BF16),
                   jax.ShapeDtypeStruct((cast_rows, w_cast.shape[1]), BF16)],
        compiler_params=pltpu.CompilerParams(dimension_semantics=("parallel",),
                                             vmem_limit_bytes=BIG_VMEM_LIMIT_BYTES),
        name="latents",
    )(x, g_in.reshape(1, d), w_t, gq.reshape(1, -1), gkv.reshape(1, -1), cos, sin, w_cast)


def _gate_kernel(xn_ref, w_ref, o_ref):
    o_ref[...] = _silu(_dot_nt(xn_ref[...], w_ref[...])).astype(o_ref.dtype)


def _gate(xn, w_z_t, tm, tn):
    m, d = xn.shape
    n = w_z_t.shape[0]
    return pl.pallas_call(
        _gate_kernel,
        grid=(m // tm, n // tn),
        in_specs=[pl.BlockSpec((tm, d), lambda i, j: (i, 0)),
                  pl.BlockSpec((tn, d), lambda i, j: (j, 0))],
        out_specs=pl.BlockSpec((tm, tn), lambda i, j: (i, j)),
        out_shape=jax.ShapeDtypeStruct((m, n), BF16),
        compiler_params=_params("parallel", "parallel"),
        name="attn_gate",
    )(xn, w_z_t)


def _q_weight_kernel(w_ref, o_ref):
    lane = lax.broadcasted_iota(jnp.int32, (w_ref.shape[0], LANES), 1)

    def rope_tile(u):
        swapped = jnp.where(lane < LANES - ROPE_HALF, -pltpu.roll(u, ROPE_HALF, 1),
                            pltpu.roll(u, LANES - ROPE_HALF, 1))
        return jnp.where(lane < ROPE_DIM, u, swapped)

    t0 = w_ref[:, :LANES]
    t1 = w_ref[:, LANES:2 * LANES]
    t2 = w_ref[:, 2 * LANES:]
    t2r = pltpu.roll(t2, ROPE_DIM, 1)
    o_ref[:, :LANES] = t0.astype(o_ref.dtype)
    o_ref[:, LANES:2 * LANES] = rope_tile(t1).astype(o_ref.dtype)
    o_ref[:, 2 * LANES:3 * LANES] = jnp.where(lane < ROPE_DIM, pltpu.roll(t1, ROPE_DIM, 1),
                                              t2r).astype(o_ref.dtype)
    o_ref[:, 3 * LANES:] = rope_tile(t2r).astype(o_ref.dtype)


def _q_weights(w_uq):
    k = w_uq.shape[0]
    pair_in = 2 * (QK_NOPE_DIM + ROPE_DIM)
    return pl.pallas_call(
        _q_weight_kernel,
        grid=(N_HEADS // 2,),
        in_specs=[pl.BlockSpec((k, pair_in), lambda j: (0, j))],
        out_specs=pl.BlockSpec((k, 2 * QK_PAD_DIM), lambda j: (0, j)),
        out_shape=jax.ShapeDtypeStruct((k, N_HEADS * QK_PAD_DIM), BF16),
        compiler_params=_params("parallel"),
        name="q_weights",
    )(w_uq)


def _q_up_kernel(cq_ref, w_ref, cos_ref, sin_ref, q_ref):
    r = _dot(cq_ref[...], w_ref[...])
    cos = cos_ref[...]
    sin = sin_ref[...]
    for h in range(N_HEADS):
        lo = h * QK_PAD_DIM
        nope = r[:, lo:lo + LANES] * Q_SCALE
        pe = _rope_pair(r[:, lo + LANES:lo + QK_PAD_DIM], cos, sin) * Q_SCALE
        q_ref[:, lo:lo + LANES] = nope.astype(q_ref.dtype)
        q_ref[:, lo + LANES:lo + QK_PAD_DIM] = pe.astype(q_ref.dtype)


def _q_up(cq, w_q, cos, sin, seq, tm):
    m, k = cq.shape
    n = w_q.shape[1]
    tps = seq // tm
    return pl.pallas_call(
        _q_up_kernel,
        grid=(m // tm,),
        in_specs=[pl.BlockSpec((tm, k), lambda i: (i, 0)),
                  pl.BlockSpec((k, n), lambda i: (0, 0)),
                  pl.BlockSpec((tm, LANES), lambda i: (i % tps, 0)),
                  pl.BlockSpec((tm, LANES), lambda i: (i % tps, 0))],
        out_specs=pl.BlockSpec((tm, n), lambda i: (i, 0)),
        out_shape=jax.ShapeDtypeStruct((m, n), BF16),
        compiler_params=_params("parallel"),
        name="q_up",
    )(cq, w_q, cos, sin)


def _kv_up_kernel(ckv_ref, w_ref, k_ref, vt_ref, *, blk):
    kv = _dot(ckv_ref[...], w_ref[...])
    ones = jnp.ones((V_AUG_DIM - V_HEAD_DIM, blk), vt_ref.dtype)
    for h in range(N_HEADS):
        lo = h * (QK_NOPE_DIM + V_HEAD_DIM)
        k_ref[:, h * QK_NOPE_DIM:(h + 1) * QK_NOPE_DIM] = (
            kv[:, lo:lo + QK_NOPE_DIM].astype(k_ref.dtype))
        v = kv[:, lo + QK_NOPE_DIM:lo + QK_NOPE_DIM + V_HEAD_DIM]
        for r in range(v.shape[0] // blk):
            vt_ref[r, h * V_AUG_DIM:h * V_AUG_DIM + V_HEAD_DIM, :] = (
                v[r * blk:(r + 1) * blk, :].T.astype(vt_ref.dtype))
            vt_ref[r, h * V_AUG_DIM + V_HEAD_DIM:(h + 1) * V_AUG_DIM, :] = ones


def _kv_up(ckv, w_kv, tm, blk):
    m, k = ckv.shape
    n = N_HEADS * V_AUG_DIM
    per = tm // blk
    return pl.pallas_call(
        functools.partial(_kv_up_kernel, blk=blk),
        grid=(m // tm,),
        in_specs=[pl.BlockSpec((tm, k), lambda i: (i, 0)),
                  pl.BlockSpec(w_kv.shape, lambda i: (0, 0))],
        out_specs=[pl.BlockSpec((tm, N_HEADS * QK_NOPE_DIM), lambda i: (i, 0)),
                   pl.BlockSpec((per, n, blk), lambda i: (i, 0, 0))],
        out_shape=[jax.ShapeDtypeStruct((m, N_HEADS * QK_NOPE_DIM), BF16),
                   jax.ShapeDtypeStruct((m // blk, n, blk), BF16)],
        compiler_params=_params("parallel"),
        name="kv_up",
    )(ckv, w_kv)


CAST_PIECE_ROWS = 16


def _attention_kernel(q_ref, kn_ref, kpe_ref, vt_ref, gate_ref, *rest, blk, n_casts):
    cast_in = rest[:n_casts]
    o_ref = rest[n_casts]
    cast_out = rest[n_casts + 1:2 * n_casts + 1]
    kfull_ref = rest[2 * n_casts + 1]
    cast_pieces = [(src, dst, r) for src, dst in zip(cast_in, cast_out)
                   for r in range(0, src.shape[0], CAST_PIECE_ROWS)]
    seq = q_ref.shape[0]
    nblk = seq // blk
    kfull_ref[:, :LANES] = kn_ref[...]
    kfull_ref[:, LANES:] = kpe_ref[...]

    key_chunk = lax.broadcasted_iota(jnp.int32, (blk, blk), 0) // CHUNK
    qry_chunk = lax.broadcasted_iota(jnp.int32, (blk, blk), 1) // CHUNK
    visible = key_chunk <= qry_chunk

    pairs = [(qi, kj) for qi in range(nblk) for kj in range(qi + 1)]
    half = blk // 2

    def scores(qi, kj):
        k0, q0 = kj * blk, qi * blk
        q_lo = q_ref[q0:q0 + half, :]
        q_hi = q_ref[q0 + half:q0 + blk, :]
        if kj != qi:
            kb = kfull_ref[k0:k0 + blk, :]
            return _dot_nt(kb, q_lo), _dot_nt(kb, q_hi)
        s_lo = _dot_nt(kfull_ref[k0:k0 + half, :], q_lo)
        s_hi = _dot_nt(kfull_ref[k0:k0 + blk, :], q_hi)
        return (jnp.where(visible[:half, :half], s_lo, -jnp.inf),
                jnp.where(visible[:, half:], s_hi, -jnp.inf))

    def softmax_step(state, s, vt):
        m_blk = jnp.max(s, axis=0, keepdims=True)
        if state is None:
            return m_blk, _dot(vt, jnp.exp2((s - m_blk).astype(BF16)))
        m_run, acc = state
        m_new = jnp.maximum(m_run, m_blk)
        alpha = jnp.exp2(m_run - m_new)
        return m_new, alpha * acc + _dot(vt, jnp.exp2((s - m_new).astype(BF16)))

    lo_state = hi_state = None
    ahead = [scores(*pairs[n]) for n in range(min(SCORE_LOOKAHEAD, len(pairs)))]
    for n, (qi, kj) in enumerate(pairs):
        s_lo, s_hi = ahead.pop(0)
        if n + SCORE_LOOKAHEAD < len(pairs):
            ahead.append(scores(*pairs[n + SCORE_LOOKAHEAD]))
        vt_lo = vt_ref[kj, :, :half] if kj == qi else vt_ref[kj]
        lo_state = softmax_step(lo_state, s_lo, vt_lo)
        hi_state = softmax_step(hi_state, s_hi, vt_ref[kj])
        if kj == qi:
            acc = jnp.concatenate([lo_state[1], hi_state[1]], axis=1)
            out = (acc[:V_HEAD_DIM] / acc[V_HEAD_DIM:V_HEAD_DIM + 1]).T
            gate = gate_ref[qi * blk:(qi + 1) * blk, :].astype(F32)
            o_ref[qi * blk:(qi + 1) * blk, :] = (out * gate).astype(o_ref.dtype)
            lo_state = hi_state = None
        lo = n * len(cast_pieces) // len(pairs)
        hi = (n + 1) * len(cast_pieces) // len(pairs)
        for src, dst, r in cast_pieces[lo:hi]:
            dst[r:r + CAST_PIECE_ROWS, :] = src[r:r + CAST_PIECE_ROWS, :].astype(dst.dtype)


def _attention(q, k_nope, k_pe, v_t, gate, batch, seq, blk, casts):
    m = q.shape[0]
    nblk = seq // blk
    steps = batch * N_HEADS
    cast_specs, cast_shapes = [], []
    for w, n_rows in casts:
        rows = n_rows // steps
        assert rows * steps == n_rows and rows % CAST_PIECE_ROWS == 0
        cast_specs.append(pl.BlockSpec((rows, w.shape[1]), lambda b, h: (b * N_HEADS + h, 0)))
        cast_shapes.append(jax.ShapeDtypeStruct((n_rows, w.shape[1]), BF16))
    return pl.pallas_call(
        functools.partial(_attention_kernel, blk=blk, n_casts=len(casts)),
        grid=(batch, N_HEADS),
        in_specs=[pl.BlockSpec((seq, QK_PAD_DIM), lambda b, h: (b, h)),
                  pl.BlockSpec((seq, LANES), lambda b, h: (b, h)),
                  pl.BlockSpec((seq, LANES), lambda b, h: (b, 0)),
                  pl.BlockSpec((nblk, V_AUG_DIM, blk), lambda b, h: (b, h, 0)),
                  pl.BlockSpec((seq, LANES), lambda b, h: (b, h))] + cast_specs,
        out_specs=[pl.BlockSpec((seq, V_HEAD_DIM), lambda b, h: (b, h))] + cast_specs,
        out_shape=[jax.ShapeDtypeStruct((m, N_HEADS * V_HEAD_DIM), BF16)] + cast_shapes,
        scratch_shapes=[pltpu.VMEM((seq, QK_PAD_DIM), BF16)],
        compiler_params=_params("parallel", "parallel"),
        name="attention",
    )(q, k_nope, k_pe, v_t, gate, *[w for w, _ in casts])


def _out_proj_kernel(yc_ref, ya_ref, w_ref, x_ref, g_ref, o_ref, *, final_norm, tn):
    kc = yc_ref.shape[1]
    d = o_ref.shape[1]
    yc = yc_ref[...]
    ya = ya_ref[...]
    ss = None
    for t in range(d // tn):
        cols = slice(t * tn, (t + 1) * tn)
        h = x_ref[:, cols] + (_dot(yc, w_ref[:kc, cols]) + _dot(ya, w_ref[kc:, cols]))
        o_ref[:, cols] = h.astype(o_ref.dtype)
        if final_norm:
            part = jnp.sum(h * h, axis=-1, keepdims=True)
            ss = part if ss is None else ss + part
    if final_norm:
        inv = lax.rsqrt(ss / d + NORM_EPS)
        for t in range(d // tn):
            cols = slice(t * tn, (t + 1) * tn)
            o_ref[:, cols] = (o_ref[:, cols] * inv) * g_ref[:, cols]


def _out_proj(y_conv, y_attn, w_o, x, g, final_norm, tm, tn):
    m, d = x.shape
    kc = y_conv.shape[1]
    ka = y_attn.shape[1]
    return pl.pallas_call(
        functools.partial(_out_proj_kernel, final_norm=final_norm, tn=tn),
        grid=(m // tm,),
        in_specs=[pl.BlockSpec((tm, kc), lambda i: (i, 0)),
                  pl.BlockSpec((tm, ka), lambda i: (i, 0)),
                  pl.BlockSpec((kc + ka, d), lambda i: (0, 0), pipeline_mode=pl.Buffered(1)),
                  pl.BlockSpec((tm, d), lambda i: (i, 0)),
                  pl.BlockSpec((1, d), lambda i: (0, 0))],
        out_specs=pl.BlockSpec((tm, d), lambda i: (i, 0)),
        out_shape=jax.ShapeDtypeStruct((m, d), x.dtype),
        compiler_params=pltpu.CompilerParams(dimension_semantics=("parallel",),
                                             vmem_limit_bytes=BIG_VMEM_LIMIT_BYTES),
        name="out_proj",
    )(y_conv, y_attn, w_o, x, g.reshape(1, d))


def _rope_tables(seq):
    pos = jnp.arange(seq, dtype=F32)
    inv_freq = 1.0 / (ROPE_THETA ** (jnp.arange(0, ROPE_DIM, 2, dtype=F32) / ROPE_DIM))
    ang = pos[:, None] * inv_freq[None, :]
    zeros = jnp.zeros((seq, LANES - ROPE_DIM), F32)
    cos = jnp.concatenate([jnp.cos(ang), jnp.cos(ang), zeros], axis=-1)
    sin = jnp.concatenate([jnp.sin(ang), jnp.sin(ang), zeros], axis=-1)
    return cos, sin


def _tile(n, pref):
    return pref if n % pref == 0 else n


def _layer(h, g_in, w_in, conv_w, q_norm_g, w_uq, kv_norm_g, w_ukv, w_out, g_out,
           final_norm, cos, sin, batch, seq):
    m, d = h.shape
    dc = conv_w.shape[1]
    o = 4 * dc

    w_in_t = w_in.T
    kr0 = o + Q_LORA + KV_LORA
    n_lat = Q_LORA + KV_LORA + LANES
    w_lat_t = _cast_rows(w_in_t, o, n_lat, n_lat // 2, d // 2, "cast_w_latent")
    w_q = _q_weights(w_uq)
    w_kv = w_ukv.astype(BF16)

    blk = _tile(seq, ATTN_BLOCK)
    xn, cq, ckv, k_pe, w_z_t = _latents(h, g_in, w_lat_t, q_norm_g, kv_norm_g, cos, sin,
                                        w_in_t, kr0 + ROPE_DIM, w_in_t.shape[0] - kr0 - ROPE_DIM,
                                        seq, _tile(seq, 512))
    gate = _gate(xn, w_z_t, _tile(m, 1024), _tile(w_z_t.shape[0], 1024))
    q = _q_up(cq, w_q, cos, sin, seq, _tile(seq, 512))
    k_nope, v_t = _kv_up(ckv, w_kv, _tile(seq, 512), blk)
    y_attn, w_conv_t, w_o = _attention(q, k_nope, k_pe, v_t, gate, batch, seq, blk,
                                       [(w_in_t, o), (w_out, w_out.shape[0])])
    y_conv = _conv_group(xn, w_conv_t, conv_w, seq, _tile(seq, 1024), _tile(dc, 256))
    return _out_proj(y_conv, y_attn, w_o, h, g_out, final_norm, _tile(m, 256), _tile(d, 512))


def kernel(x, g_in, w_in, conv_w, q_norm_g, w_uq, kv_norm_g, w_ukv, w_out, g_final):
    batch, seq, d = x.shape
    depth = g_in.shape[0]
    cos, sin = _rope_tables(seq)
    h = x.reshape(batch * seq, d)
    for l in range(depth):
        h = _layer(h, g_in[l], w_in[l], conv_w[l], q_norm_g[l], w_uq[l], kv_norm_g[l],
                   w_ukv[l], w_out[l], g_final, l == depth - 1, cos, sin, batch, seq)
    return h.reshape(batch, seq, d)
```

```python
import functools

import jax
import jax.numpy as jnp
import numpy as np
from jax import lax
from jax.experimental import pallas as pl
from jax.experimental.pallas import tpu as pltpu

CHUNK = 64
CONV_WIDTH = 3
N_HEADS = 16
V_HEAD_DIM = 128
QK_NOPE_DIM = 128
ROPE_DIM = 64
ROPE_HALF = ROPE_DIM // 2
Q_LORA = 1024
KV_LORA = 512
ROPE_THETA = 10000.0
NORM_EPS = 1e-6
ATTN_SCALE = (QK_NOPE_DIM + ROPE_DIM) ** -0.5
LOG2_E = float(np.log2(np.e))
Q_SCALE = ATTN_SCALE * LOG2_E

LANES = 128
SUBLANES = 8
QK_PAD_DIM = 2 * LANES
V_AUG_DIM = V_HEAD_DIM + 16
ATTN_BLOCK = 512
SCORE_LOOKAHEAD = 2
CAST_PIECE_ROWS = 16
VMEM_LIMIT_BYTES = 56 * 1024 * 1024
BIG_VMEM_LIMIT_BYTES = 60 * 1024 * 1024

BF16 = jnp.bfloat16
F32 = jnp.float32


def _params(*semantics):
    return pltpu.CompilerParams(dimension_semantics=semantics,
                                vmem_limit_bytes=VMEM_LIMIT_BYTES)


def _dot(a, b):
    return jnp.dot(a, b, preferred_element_type=F32)


def _dot_nt(a, b):
    return lax.dot_general(a, b, (((1,), (1,)), ((), ())), preferred_element_type=F32)


def _rmsnorm_rows(x, g):
    ms = jnp.mean(x * x, axis=-1, keepdims=True)
    return (x * lax.rsqrt(ms + NORM_EPS)) * g


def _silu(z):
    return z * (0.5 * (jnp.tanh(0.5 * z) + 1.0))


def _conv_group_kernel(xn_ref, wb_ref, wc_ref, wh_ref, wz_ref, cw_ref, y_ref, carry_ref,
                       *, tiles_per_seq):
    i = pl.program_id(0)
    j = pl.program_id(1)
    xn = xn_ref[...]
    u = _dot_nt(xn, wc_ref[...]) * _dot_nt(xn, wh_ref[...])
    tm, tc = u.shape

    @pl.when(i % tiles_per_seq == 0)
    def _():
        carry_ref[j] = jnp.zeros((SUBLANES, tc), F32)

    prev = carry_ref[j]
    carry_ref[j] = u[tm - SUBLANES:, :]

    row = lax.broadcasted_iota(jnp.int32, (SUBLANES, tc), 0)

    def shifted(k):
        body = pltpu.roll(u, k, 0)
        head = jnp.where(row < k, pltpu.roll(prev, k, 0), body[:SUBLANES, :])
        return jnp.concatenate([head, body[SUBLANES:, :]], axis=0)

    cw = cw_ref[...]
    conv = cw[0:1, :] * shifted(2) + cw[1:2, :] * shifted(1) + cw[2:3, :] * u
    y = (_dot_nt(xn, wb_ref[...]) * conv) * _silu(_dot_nt(xn, wz_ref[...]))
    y_ref[...] = y.astype(y_ref.dtype)


def _conv_group(xn, w_t, conv_w, seq, tm, tc):
    m, d = xn.shape
    dc = conv_w.shape[1]
    nj = dc // tc

    def w_spec(k):
        return pl.BlockSpec((tc, d), lambda i, j: (j + k * nj, 0))

    return pl.pallas_call(
        functools.partial(_conv_group_kernel, tiles_per_seq=seq // tm),
        grid=(m // tm, nj),
        in_specs=[pl.BlockSpec((tm, d), lambda i, j: (i, 0)),
                  w_spec(0), w_spec(1), w_spec(2), w_spec(3),
                  pl.BlockSpec((CONV_WIDTH, tc), lambda i, j: (0, j))],
        out_specs=pl.BlockSpec((tm, tc), lambda i, j: (i, j)),
        out_shape=jax.ShapeDtypeStruct((m, dc), BF16),
        scratch_shapes=[pltpu.VMEM((nj, SUBLANES, tc), F32)],
        compiler_params=_params("arbitrary", "arbitrary"),
        name="conv_group",
    )(xn, w_t, w_t, w_t, w_t, conv_w)


def _rope_pair(t, cos, sin):
    return t * cos + pltpu.roll(t, ROPE_DIM, 1) * sin


def _rope_low_half(t, cos, sin):
    lane = lax.broadcasted_iota(jnp.int32, t.shape, 1)
    partner = jnp.where(lane < ROPE_HALF,
                        -pltpu.roll(t, LANES - ROPE_HALF, 1),
                        pltpu.roll(t, ROPE_HALF, 1))
    return jnp.where(lane < ROPE_DIM, t * cos + partner * sin, 0.0)


def _latent_kernel(x_ref, g_ref, w_ref, gq_ref, gkv_ref, cos_ref, sin_ref, cast_in_ref,
                   xn_ref, cq_ref, ckv_ref, kpe_ref, cast_out_ref):
    cast_out_ref[...] = cast_in_ref[...].astype(cast_out_ref.dtype)
    g = g_ref[...]
    c = _dot_nt((x_ref[...] * g).astype(BF16), w_ref[...])
    x = x_ref[...]
    inv = lax.rsqrt(jnp.mean(x * x, axis=-1, keepdims=True) + NORM_EPS)
    xn_ref[...] = ((x * inv) * g).astype(xn_ref.dtype)
    c = c * inv
    cq_ref[...] = _rmsnorm_rows(c[:, :Q_LORA], gq_ref[...]).astype(cq_ref.dtype)
    ckv_ref[...] = _rmsnorm_rows(c[:, Q_LORA:Q_LORA + KV_LORA], gkv_ref[...]).astype(ckv_ref.dtype)
    kpe_ref[...] = _rope_low_half(c[:, Q_LORA + KV_LORA:], cos_ref[...],
                                  sin_ref[...]).astype(kpe_ref.dtype)


def _latents(x, g_in, w_t, gq, gkv, cos, sin, w_cast, cast_row0, cast_rows, seq, tm):
    m, d = x.shape
    tps = seq // tm
    n = w_t.shape[0]
    assert n == Q_LORA + KV_LORA + LANES
    cr = cast_rows // (m // tm)
    assert cr * (m // tm) == cast_rows and cr % CAST_PIECE_ROWS == 0 and cast_row0 % SUBLANES == 0
    return pl.pallas_call(
        _latent_kernel,
        grid=(m // tm,),
        in_specs=[pl.BlockSpec((tm, d), lambda i: (i, 0)),
                  pl.BlockSpec((1, d), lambda i: (0, 0)),
                  pl.BlockSpec((n, d), lambda i: (0, 0), pipeline_mode=pl.Buffered(1)),
                  pl.BlockSpec((1, Q_LORA), lambda i: (0, 0)),
                  pl.BlockSpec((1, KV_LORA), lambda i: (0, 0)),
                  pl.BlockSpec((tm, LANES), lambda i: (i % tps, 0)),
                  pl.BlockSpec((tm, LANES), lambda i: (i % tps, 0)),
                  pl.BlockSpec((pl.Element(cr), pl.Element(w_cast.shape[1])),
                               lambda i: (pl.multiple_of(cast_row0 + i * cr, SUBLANES), 0))],
        out_specs=[pl.BlockSpec((tm, d), lambda i: (i, 0)),
                   pl.BlockSpec((tm, Q_LORA), lambda i: (i, 0)),
                   pl.BlockSpec((tm, KV_LORA), lambda i: (i, 0)),
                   pl.BlockSpec((tm, LANES), lambda i: (i, 0)),
                   pl.BlockSpec((cr, w_cast.shape[1]), lambda i: (i, 0))],
        out_shape=[jax.ShapeDtypeStruct((m, d), BF16),
                   jax.ShapeDtypeStruct((m, Q_LORA), BF16),
                   jax.ShapeDtypeStruct((m, KV_LORA), BF16),
                   jax.ShapeDtypeStruct((m, LANES), BF16),
                   jax.ShapeDtypeStruct((cast_rows, w_cast.shape[1]), BF16)],
        compiler_params=pltpu.CompilerParams(dimension_semantics=("parallel",),
                                             vmem_limit_bytes=BIG_VMEM_LIMIT_BYTES),
        name="latents",
    )(x, g_in.reshape(1, d), w_t, gq.reshape(1, -1), gkv.reshape(1, -1), cos, sin, w_cast)


def _gate_kernel(xn_ref, w_ref, o_ref):
    o_ref[...] = _silu(_dot_nt(xn_ref[...], w_ref[...])).astype(o_ref.dtype)


def _gate(xn, w_z_t, tm, tn):
    m, d = xn.shape
    n = w_z_t.shape[0]
    return pl.pallas_call(
        _gate_kernel,
        grid=(m // tm, n // tn),
        in_specs=[pl.BlockSpec((tm, d), lambda i, j: (i, 0)),
                  pl.BlockSpec((tn, d), lambda i, j: (j, 0))],
        out_specs=pl.BlockSpec((tm, tn), lambda i, j: (i, j)),
        out_shape=jax.ShapeDtypeStruct((m, n), BF16),
        compiler_params=_params("parallel", "parallel"),
        name="attn_gate",
    )(xn, w_z_t)


def _q_weight_kernel(w_ref, cast_in_ref, o_ref, cast_out_ref):
    cast_out_ref[...] = cast_in_ref[...].astype(cast_out_ref.dtype)
    lane = lax.broadcasted_iota(jnp.int32, (w_ref.shape[0], LANES), 1)

    def rope_tile(u):
        swapped = jnp.where(lane < LANES - ROPE_HALF, -pltpu.roll(u, ROPE_HALF, 1),
                            pltpu.roll(u, LANES - ROPE_HALF, 1))
        return jnp.where(lane < ROPE_DIM, u, swapped)

    t0 = w_ref[:, :LANES]
    t1 = w_ref[:, LANES:2 * LANES]
    t2 = w_ref[:, 2 * LANES:]
    t2r = pltpu.roll(t2, ROPE_DIM, 1)
    o_ref[:, :LANES] = t0.astype(o_ref.dtype)
    o_ref[:, LANES:2 * LANES] = rope_tile(t1).astype(o_ref.dtype)
    o_ref[:, 2 * LANES:3 * LANES] = jnp.where(lane < ROPE_DIM, pltpu.roll(t1, ROPE_DIM, 1),
                                              t2r).astype(o_ref.dtype)
    o_ref[:, 3 * LANES:] = rope_tile(t2r).astype(o_ref.dtype)


def _q_weights(w_uq, w_cast, cast_row0, cast_rows):
    k = w_uq.shape[0]
    pair_in = 2 * (QK_NOPE_DIM + ROPE_DIM)
    steps = N_HEADS // 2
    cr = cast_rows // steps
    assert cr * steps == cast_rows and cr % CAST_PIECE_ROWS == 0 and cast_row0 % SUBLANES == 0
    return pl.pallas_call(
        _q_weight_kernel,
        grid=(steps,),
        in_specs=[pl.BlockSpec((k, pair_in), lambda j: (0, j)),
                  pl.BlockSpec((pl.Element(cr), pl.Element(w_cast.shape[1])),
                               lambda j: (pl.multiple_of(cast_row0 + j * cr, SUBLANES), 0))],
        out_specs=[pl.BlockSpec((k, 2 * QK_PAD_DIM), lambda j: (0, j)),
                   pl.BlockSpec((cr, w_cast.shape[1]), lambda j: (j, 0))],
        out_shape=[jax.ShapeDtypeStruct((k, N_HEADS * QK_PAD_DIM), BF16),
                   jax.ShapeDtypeStruct((cast_rows, w_cast.shape[1]), BF16)],
        compiler_params=_params("parallel"),
        name="q_weights",
    )(w_uq, w_cast)


def _qkv_up_kernel(cq_ref, ckv_ref, wq_ref, wkv_ref, cos_ref, sin_ref, q_ref, k_ref, vt_ref, *, blk):
    r = _dot(cq_ref[...], wq_ref[...])
    kv = _dot(ckv_ref[...], wkv_ref[...])
    cos = cos_ref[...]
    sin = sin_ref[...]
    for h in range(N_HEADS):
        lo = h * QK_PAD_DIM
        nope = r[:, lo:lo + LANES] * Q_SCALE
        pe = _rope_pair(r[:, lo + LANES:lo + QK_PAD_DIM], cos, sin) * Q_SCALE
        q_ref[:, lo:lo + LANES] = nope.astype(q_ref.dtype)
        q_ref[:, lo + LANES:lo + QK_PAD_DIM] = pe.astype(q_ref.dtype)
    ones = jnp.ones((V_AUG_DIM - V_HEAD_DIM, blk), vt_ref.dtype)
    for h in range(N_HEADS):
        lo = h * (QK_NOPE_DIM + V_HEAD_DIM)
        k_ref[:, h * QK_NOPE_DIM:(h + 1) * QK_NOPE_DIM] = (
            kv[:, lo:lo + QK_NOPE_DIM].astype(k_ref.dtype))
        v = kv[:, lo + QK_NOPE_DIM:lo + QK_NOPE_DIM + V_HEAD_DIM]
        for t in range(v.shape[0] // blk):
            vt_ref[t, h * V_AUG_DIM:h * V_AUG_DIM + V_HEAD_DIM, :] = (
                v[t * blk:(t + 1) * blk, :].T.astype(vt_ref.dtype))
            vt_ref[t, h * V_AUG_DIM + V_HEAD_DIM:(h + 1) * V_AUG_DIM, :] = ones


def _qkv_up(cq, ckv, w_q, w_kv, cos, sin, seq, tm, blk):
    m = cq.shape[0]
    tps = seq // tm
    per = tm // blk
    nq = w_q.shape[1]
    nk = N_HEADS * QK_NOPE_DIM
    nv = N_HEADS * V_AUG_DIM
    return pl.pallas_call(
        functools.partial(_qkv_up_kernel, blk=blk),
        grid=(m // tm,),
        in_specs=[pl.BlockSpec((tm, cq.shape[1]), lambda i: (i, 0)),
                  pl.BlockSpec((tm, ckv.shape[1]), lambda i: (i, 0)),
                  pl.BlockSpec(w_q.shape, lambda i: (0, 0)),
                  pl.BlockSpec(w_kv.shape, lambda i: (0, 0)),
                  pl.BlockSpec((tm, LANES), lambda i: (i % tps, 0)),
                  pl.BlockSpec((tm, LANES), lambda i: (i % tps, 0))],
        out_specs=[pl.BlockSpec((tm, nq), lambda i: (i, 0)),
                   pl.BlockSpec((tm, nk), lambda i: (i, 0)),
                   pl.BlockSpec((per, nv, blk), lambda i: (i, 0, 0))],
        out_shape=[jax.ShapeDtypeStruct((m, nq), BF16),
                   jax.ShapeDtypeStruct((m, nk), BF16),
                   jax.ShapeDtypeStruct((m // blk, nv, blk), BF16)],
        compiler_params=_params("parallel"),
        name="qkv_up",
    )(cq, ckv, w_q, w_kv, cos, sin)


def _attention_kernel(q_ref, kn_ref, kpe_ref, vt_ref, gate_ref, *rest, blk, n_casts):
    cast_in = rest[:n_casts]
    o_ref = rest[n_casts]
    cast_out = rest[n_casts + 1:2 * n_casts + 1]
    kfull_ref = rest[2 * n_casts + 1]
    cast_pieces = [(src, dst, r) for src, dst in zip(cast_in, cast_out)
                   for r in range(0, src.shape[0], CAST_PIECE_ROWS)]
    seq = q_ref.shape[0]
    nblk = seq // blk
    kfull_ref[:, :LANES] = kn_ref[...]
    kfull_ref[:, LANES:] = kpe_ref[...]

    key_chunk = lax.broadcasted_iota(jnp.int32, (blk, blk), 0) // CHUNK
    qry_chunk = lax.broadcasted_iota(jnp.int32, (blk, blk), 1) // CHUNK
    visible = key_chunk <= qry_chunk

    pairs = [(qi, kj) for qi in range(nblk) for kj in range(qi + 1)]
    half = blk // 2

    def scores(qi, kj):
        k0, q0 = kj * blk, qi * blk
        q_lo = q_ref[q0:q0 + half, :]
        q_hi = q_ref[q0 + half:q0 + blk, :]
        if kj != qi:
            kb = kfull_ref[k0:k0 + blk, :]
            return _dot_nt(kb, q_lo), _dot_nt(kb, q_hi)
        s_lo = _dot_nt(kfull_ref[k0:k0 + half, :], q_lo)
        s_hi = _dot_nt(kfull_ref[k0:k0 + blk, :], q_hi)
        return (jnp.where(visible[:half, :half], s_lo, -jnp.inf),
                jnp.where(visible[:, half:], s_hi, -jnp.inf))

    def softmax_step(state, s, vt):
        m_blk = jnp.max(s, axis=0, keepdims=True)
        if state is None:
            return m_blk, _dot(vt, jnp.exp2((s - m_blk).astype(BF16)))
        m_run, acc = state
        m_new = jnp.maximum(m_run, m_blk)
        alpha = jnp.exp2(m_run - m_new)
        return m_new, alpha * acc + _dot(vt, jnp.exp2((s - m_new).astype(BF16)))

    lo_state = hi_state = None
    ahead = [scores(*pairs[n]) for n in range(min(SCORE_LOOKAHEAD, len(pairs)))]
    for n, (qi, kj) in enumerate(pairs):
        s_lo, s_hi = ahead.pop(0)
        if n + SCORE_LOOKAHEAD < len(pairs):
            ahead.append(scores(*pairs[n + SCORE_LOOKAHEAD]))
        vt_lo = vt_ref[kj, :, :half] if kj == qi else vt_ref[kj]
        lo_state = softmax_step(lo_state, s_lo, vt_lo)
        hi_state = softmax_step(hi_state, s_hi, vt_ref[kj])
        if kj == qi:
            acc = jnp.concatenate([lo_state[1], hi_state[1]], axis=1)
            out = (acc[:V_HEAD_DIM] / acc[V_HEAD_DIM:V_HEAD_DIM + 1]).T
            gate = gate_ref[qi * blk:(qi + 1) * blk, :].astype(F32)
            o_ref[qi * blk:(qi + 1) * blk, :] = (out * gate).astype(o_ref.dtype)
            lo_state = hi_state = None
        lo = n * len(cast_pieces) // len(pairs)
        hi = (n + 1) * len(cast_pieces) // len(pairs)
        for src, dst, r in cast_pieces[lo:hi]:
            dst[r:r + CAST_PIECE_ROWS, :] = src[r:r + CAST_PIECE_ROWS, :].astype(dst.dtype)


def _attention(q, k_nope, k_pe, v_t, gate, batch, seq, blk, casts):
    m = q.shape[0]
    nblk = seq // blk
    steps = batch * N_HEADS
    cast_specs, cast_shapes = [], []
    for w, n_rows in casts:
        rows = n_rows // steps
        assert rows * steps == n_rows and rows % CAST_PIECE_ROWS == 0
        cast_specs.append(pl.BlockSpec((rows, w.shape[1]), lambda b, h: (b * N_HEADS + h, 0)))
        cast_shapes.append(jax.ShapeDtypeStruct((n_rows, w.shape[1]), BF16))
    return pl.pallas_call(
        functools.partial(_attention_kernel, blk=blk, n_casts=len(casts)),
        grid=(batch, N_HEADS),
        in_specs=[pl.BlockSpec((seq, QK_PAD_DIM), lambda b, h: (b, h)),
                  pl.BlockSpec((seq, LANES), lambda b, h: (b, h)),
                  pl.BlockSpec((seq, LANES), lambda b, h: (b, 0)),
                  pl.BlockSpec((nblk, V_AUG_DIM, blk), lambda b, h: (b, h, 0)),
                  pl.BlockSpec((seq, LANES), lambda b, h: (b, h))] + cast_specs,
        out_specs=[pl.BlockSpec((seq, V_HEAD_DIM), lambda b, h: (b, h))] + cast_specs,
        out_shape=[jax.ShapeDtypeStruct((m, N_HEADS * V_HEAD_DIM), BF16)] + cast_shapes,
        scratch_shapes=[pltpu.VMEM((seq, QK_PAD_DIM), BF16)],
        compiler_params=_params("parallel", "parallel"),
        name="attention",
    )(q, k_nope, k_pe, v_t, gate, *[w for w, _ in casts])


def _out_proj_kernel(yc_ref, ya_ref, w_ref, x_ref, g_ref, o_ref, *, final_norm, tn):
    kc = yc_ref.shape[1]
    d = o_ref.shape[1]
    yc = yc_ref[...]
    ya = ya_ref[...]
    ss = None
    for t in range(d // tn):
        cols = slice(t * tn, (t + 1) * tn)
        h = x_ref[:, cols] + (_dot(yc, w_ref[:kc, cols]) + _dot(ya, w_ref[kc:, cols]))
        o_ref[:, cols] = h.astype(o_ref.dtype)
        if final_norm:
            part = jnp.sum(h * h, axis=-1, keepdims=True)
            ss = part if ss is None else ss + part
    if final_norm:
        inv = lax.rsqrt(ss / d + NORM_EPS)
        for t in range(d // tn):
            cols = slice(t * tn, (t + 1) * tn)
            o_ref[:, cols] = (o_ref[:, cols] * inv) * g_ref[:, cols]


def _out_proj(y_conv, y_attn, w_o, x, g, final_norm, tm, tn):
    m, d = x.shape
    kc = y_conv.shape[1]
    ka = y_attn.shape[1]
    return pl.pallas_call(
        functools.partial(_out_proj_kernel, final_norm=final_norm, tn=tn),
        grid=(m // tm,),
        in_specs=[pl.BlockSpec((tm, kc), lambda i: (i, 0)),
                  pl.BlockSpec((tm, ka), lambda i: (i, 0)),
                  pl.BlockSpec((kc + ka, d), lambda i: (0, 0), pipeline_mode=pl.Buffered(1)),
                  pl.BlockSpec((tm, d), lambda i: (i, 0)),
                  pl.BlockSpec((1, d), lambda i: (0, 0))],
        out_specs=pl.BlockSpec((tm, d), lambda i: (i, 0)),
        out_shape=jax.ShapeDtypeStruct((m, d), x.dtype),
        compiler_params=pltpu.CompilerParams(dimension_semantics=("parallel",),
                                             vmem_limit_bytes=BIG_VMEM_LIMIT_BYTES),
        name="out_proj",
    )(y_conv, y_attn, w_o, x, g.reshape(1, d))


def _rope_tables(seq):
    pos = jnp.arange(seq, dtype=F32)
    inv_freq = 1.0 / (ROPE_THETA ** (jnp.arange(0, ROPE_DIM, 2, dtype=F32) / ROPE_DIM))
    ang = pos[:, None] * inv_freq[None, :]
    zeros = jnp.zeros((seq, LANES - ROPE_DIM), F32)
    cos = jnp.concatenate([jnp.cos(ang), jnp.cos(ang), zeros], axis=-1)
    sin = jnp.concatenate([jnp.sin(ang), jnp.sin(ang), zeros], axis=-1)
    return cos, sin


def _tile(n, pref):
    return pref if n % pref == 0 else n


def _layer(h, g_in, w_in, conv_w, q_norm_g, w_uq, kv_norm_g, w_ukv, w_out, g_out,
           final_norm, cos, sin, batch, seq):
    m, d = h.shape
    dc = conv_w.shape[1]
    o = 4 * dc

    w_in_t = w_in.T
    kr0 = o + Q_LORA + KV_LORA
    n_lat = Q_LORA + KV_LORA + LANES
    w_q, w_lat_t = _q_weights(w_uq, w_in_t, o, n_lat)
    w_kv = w_ukv.astype(BF16)

    blk = _tile(seq, ATTN_BLOCK)
    xn, cq, ckv, k_pe, w_z_t = _latents(h, g_in, w_lat_t, q_norm_g, kv_norm_g, cos, sin,
                                        w_in_t, kr0 + ROPE_DIM, w_in_t.shape[0] - kr0 - ROPE_DIM,
                                        seq, _tile(seq, 512))
    gate = _gate(xn, w_z_t, _tile(m, 1024), _tile(w_z_t.shape[0], 1024))
    q, k_nope, v_t = _qkv_up(cq, ckv, w_q, w_kv, cos, sin, seq, _tile(seq, 512), blk)
    y_attn, w_conv_t, w_o = _attention(q, k_nope, k_pe, v_t, gate, batch, seq, blk,
                                       [(w_in_t, o), (w_out, w_out.shape[0])])
    y_conv = _conv_group(xn, w_conv_t, conv_w, seq, _tile(seq, 1024), _tile(dc, 256))
    return _out_proj(y_conv, y_attn, w_o, h, g_out, final_norm, _tile(m, 256), _tile(d, 512))


def kernel(x, g_in, w_in, conv_w, q_norm_g, w_uq, kv_norm_g, w_ukv, w_out, g_final):
    batch, seq, d = x.shape
    depth = g_in.shape[0]
    cos, sin = _rope_tables(seq)
    h = x.reshape(batch * seq, d)
    for l in range(depth):
        h = _layer(h, g_in[l], w_in[l], conv_w[l], q_norm_g[l], w_uq[l], kv_norm_g[l],
                   w_ukv[l], w_out[l], g_final, l == depth - 1, cos, sin, batch, seq)
    return h.reshape(batch, seq, d)
```

```python
import functools

import jax
import jax.numpy as jnp
import numpy as np
from jax import lax
from jax.experimental import pallas as pl
from jax.experimental.pallas import tpu as pltpu

CHUNK = 64
CONV_WIDTH = 3
N_HEADS = 16
V_HEAD_DIM = 128
QK_NOPE_DIM = 128
ROPE_DIM = 64
ROPE_HALF = ROPE_DIM // 2
Q_LORA = 1024
KV_LORA = 512
ROPE_THETA = 10000.0
NORM_EPS = 1e-6
ATTN_SCALE = (QK_NOPE_DIM + ROPE_DIM) ** -0.5
LOG2_E = float(np.log2(np.e))
Q_SCALE = ATTN_SCALE * LOG2_E

LANES = 128
SUBLANES = 8
QK_PAD_DIM = 2 * LANES
V_AUG_DIM = V_HEAD_DIM + 16
ATTN_BLOCK = 512
SCORE_LOOKAHEAD = 1
CAST_PIECE_ROWS = 16
VMEM_LIMIT_BYTES = 56 * 1024 * 1024
BIG_VMEM_LIMIT_BYTES = 60 * 1024 * 1024

BF16 = jnp.bfloat16
F32 = jnp.float32


def _params(*semantics):
    return pltpu.CompilerParams(dimension_semantics=semantics,
                                vmem_limit_bytes=VMEM_LIMIT_BYTES)


def _dot(a, b):
    return jnp.dot(a, b, preferred_element_type=F32)


def _dot_nt(a, b):
    return lax.dot_general(a, b, (((1,), (1,)), ((), ())), preferred_element_type=F32)


def _rmsnorm_rows(x, g):
    ms = jnp.mean(x * x, axis=-1, keepdims=True)
    return (x * lax.rsqrt(ms + NORM_EPS)) * g


def _silu(z):
    return z * (0.5 * (jnp.tanh(0.5 * z) + 1.0))


def _conv_group_kernel(xn_ref, wb_ref, wc_ref, wh_ref, wz_ref, cw_ref, y_ref, carry_ref,
                       *, tiles_per_seq):
    i = pl.program_id(0)
    j = pl.program_id(1)
    xn = xn_ref[...]
    u = _dot_nt(xn, wc_ref[...]) * _dot_nt(xn, wh_ref[...])
    tm, tc = u.shape

    @pl.when(i % tiles_per_seq == 0)
    def _():
        carry_ref[j] = jnp.zeros((SUBLANES, tc), F32)

    prev = carry_ref[j]
    carry_ref[j] = u[tm - SUBLANES:, :]

    row = lax.broadcasted_iota(jnp.int32, (SUBLANES, tc), 0)

    def shifted(k):
        body = pltpu.roll(u, k, 0)
        head = jnp.where(row < k, pltpu.roll(prev, k, 0), body[:SUBLANES, :])
        return jnp.concatenate([head, body[SUBLANES:, :]], axis=0)

    cw = cw_ref[...]
    conv = cw[0:1, :] * shifted(2) + cw[1:2, :] * shifted(1) + cw[2:3, :] * u
    y = (_dot_nt(xn, wb_ref[...]) * conv) * _silu(_dot_nt(xn, wz_ref[...]))
    y_ref[...] = y.astype(y_ref.dtype)


def _conv_group(xn, w_t, conv_w, seq, tm, tc):
    m, d = xn.shape
    dc = conv_w.shape[1]
    nj = dc // tc

    def w_spec(k):
        return pl.BlockSpec((tc, d), lambda i, j: (j + k * nj, 0))

    return pl.pallas_call(
        functools.partial(_conv_group_kernel, tiles_per_seq=seq // tm),
        grid=(m // tm, nj),
        in_specs=[pl.BlockSpec((tm, d), lambda i, j: (i, 0)),
                  w_spec(0), w_spec(1), w_spec(2), w_spec(3),
                  pl.BlockSpec((CONV_WIDTH, tc), lambda i, j: (0, j))],
        out_specs=pl.BlockSpec((tm, tc), lambda i, j: (i, j)),
        out_shape=jax.ShapeDtypeStruct((m, dc), BF16),
        scratch_shapes=[pltpu.VMEM((nj, SUBLANES, tc), F32)],
        compiler_params=_params("arbitrary", "arbitrary"),
        name="conv_group",
    )(xn, w_t, w_t, w_t, w_t, conv_w)


def _rope_pair(t, cos, sin):
    return t * cos + pltpu.roll(t, ROPE_DIM, 1) * sin


def _rope_low_half(t, cos, sin):
    lane = lax.broadcasted_iota(jnp.int32, t.shape, 1)
    partner = jnp.where(lane < ROPE_HALF,
                        -pltpu.roll(t, LANES - ROPE_HALF, 1),
                        pltpu.roll(t, ROPE_HALF, 1))
    return jnp.where(lane < ROPE_DIM, t * cos + partner * sin, 0.0)


def _latent_kernel(x_ref, g_ref, w_ref, gq_ref, gkv_ref, cos_ref, sin_ref, cast_in_ref,
                   xn_ref, cq_ref, ckv_ref, kpe_ref, cast_out_ref):
    cast_out_ref[...] = cast_in_ref[...].astype(cast_out_ref.dtype)
    x = x_ref[...]
    xg = x * g_ref[...]
    c = _dot_nt(xg.astype(BF16), w_ref[...])
    inv = lax.rsqrt(jnp.mean(x * x, axis=-1, keepdims=True) + NORM_EPS)
    xn_ref[...] = (xg * inv).astype(xn_ref.dtype)
    c = c * inv
    cq_ref[...] = _rmsnorm_rows(c[:, :Q_LORA], gq_ref[...]).astype(cq_ref.dtype)
    ckv_ref[...] = _rmsnorm_rows(c[:, Q_LORA:Q_LORA + KV_LORA], gkv_ref[...]).astype(ckv_ref.dtype)
    kpe_ref[...] = _rope_low_half(c[:, Q_LORA + KV_LORA:], cos_ref[...],
                                  sin_ref[...]).astype(kpe_ref.dtype)


def _latents(x, g_in, w_t, gq, gkv, cos, sin, w_cast, cast_row0, cast_rows, seq, tm):
    m, d = x.shape
    tps = seq // tm
    n = w_t.shape[0]
    assert n == Q_LORA + KV_LORA + LANES
    cr = cast_rows // (m // tm)
    assert cr * (m // tm) == cast_rows and cr % CAST_PIECE_ROWS == 0 and cast_row0 % SUBLANES == 0
    return pl.pallas_call(
        _latent_kernel,
        grid=(m // tm,),
        in_specs=[pl.BlockSpec((tm, d), lambda i: (i, 0)),
                  pl.BlockSpec((1, d), lambda i: (0, 0)),
                  pl.BlockSpec((n, d), lambda i: (0, 0), pipeline_mode=pl.Buffered(1)),
                  pl.BlockSpec((1, Q_LORA), lambda i: (0, 0)),
                  pl.BlockSpec((1, KV_LORA), lambda i: (0, 0)),
                  pl.BlockSpec((tm, LANES), lambda i: (i % tps, 0)),
                  pl.BlockSpec((tm, LANES), lambda i: (i % tps, 0)),
                  pl.BlockSpec((pl.Element(cr), pl.Element(w_cast.shape[1])),
                               lambda i: (pl.multiple_of(cast_row0 + i * cr, SUBLANES), 0))],
        out_specs=[pl.BlockSpec((tm, d), lambda i: (i, 0)),
                   pl.BlockSpec((tm, Q_LORA), lambda i: (i, 0)),
                   pl.BlockSpec((tm, KV_LORA), lambda i: (i, 0)),
                   pl.BlockSpec((tm, LANES), lambda i: (i, 0)),
                   pl.BlockSpec((cr, w_cast.shape[1]), lambda i: (i, 0))],
        out_shape=[jax.ShapeDtypeStruct((m, d), BF16),
                   jax.ShapeDtypeStruct((m, Q_LORA), BF16),
                   jax.ShapeDtypeStruct((m, KV_LORA), BF16),
                   jax.ShapeDtypeStruct((m, LANES), BF16),
                   jax.ShapeDtypeStruct((cast_rows, w_cast.shape[1]), BF16)],
        compiler_params=pltpu.CompilerParams(dimension_semantics=("parallel",),
                                             vmem_limit_bytes=BIG_VMEM_LIMIT_BYTES),
        name="latents",
    )(x, g_in.reshape(1, d), w_t, gq.reshape(1, -1), gkv.reshape(1, -1), cos, sin, w_cast)


def _gate_kernel(xn_ref, w_ref, o_ref, *, tn):
    xn = xn_ref[...]
    for t in range(o_ref.shape[1] // tn):
        cols = slice(t * tn, (t + 1) * tn)
        o_ref[:, cols] = _silu(_dot_nt(xn, w_ref[cols, :])).astype(o_ref.dtype)


def _gate(xn, w_z_t, tm, tn):
    m, d = xn.shape
    n = w_z_t.shape[0]
    return pl.pallas_call(
        functools.partial(_gate_kernel, tn=tn),
        grid=(m // tm,),
        in_specs=[pl.BlockSpec((tm, d), lambda i: (i, 0)),
                  pl.BlockSpec((n, d), lambda i: (0, 0), pipeline_mode=pl.Buffered(1))],
        out_specs=pl.BlockSpec((tm, n), lambda i: (i, 0)),
        out_shape=jax.ShapeDtypeStruct((m, n), BF16),
        compiler_params=_params("parallel"),
        name="attn_gate",
    )(xn, w_z_t)


def _q_weight_kernel(w_ref, *rest):
    n_casts = (len(rest) - 1) // 2
    o_ref = rest[n_casts]
    for src, dst in zip(rest[:n_casts], rest[n_casts + 1:]):
        dst[...] = src[...].astype(dst.dtype)
    lane = lax.broadcasted_iota(jnp.int32, (w_ref.shape[0], LANES), 1)

    def rope_tile(u):
        swapped = jnp.where(lane < LANES - ROPE_HALF, -pltpu.roll(u, ROPE_HALF, 1),
                            pltpu.roll(u, LANES - ROPE_HALF, 1))
        return jnp.where(lane < ROPE_DIM, u, swapped)

    t0 = w_ref[:, :LANES]
    t1 = w_ref[:, LANES:2 * LANES]
    t2 = w_ref[:, 2 * LANES:]
    t2r = pltpu.roll(t2, ROPE_DIM, 1)
    o_ref[:, :LANES] = t0.astype(o_ref.dtype)
    o_ref[:, LANES:2 * LANES] = rope_tile(t1).astype(o_ref.dtype)
    o_ref[:, 2 * LANES:3 * LANES] = jnp.where(lane < ROPE_DIM, pltpu.roll(t1, ROPE_DIM, 1),
                                              t2r).astype(o_ref.dtype)
    o_ref[:, 3 * LANES:] = rope_tile(t2r).astype(o_ref.dtype)


def _q_weights(w_uq, casts):
    k = w_uq.shape[0]
    pair_in = 2 * (QK_NOPE_DIM + ROPE_DIM)
    steps = N_HEADS // 2
    cast_in_specs, cast_out_specs, cast_shapes = [], [], []
    for w, row0, n_rows in casts:
        cr = n_rows // steps
        assert cr * steps == n_rows and cr % CAST_PIECE_ROWS == 0 and row0 % SUBLANES == 0
        cast_in_specs.append(pl.BlockSpec(
            (pl.Element(cr), pl.Element(w.shape[1])),
            lambda j, row0=row0, cr=cr: (pl.multiple_of(row0 + j * cr, SUBLANES), 0)))
        cast_out_specs.append(pl.BlockSpec((cr, w.shape[1]), lambda j: (j, 0)))
        cast_shapes.append(jax.ShapeDtypeStruct((n_rows, w.shape[1]), BF16))
    return pl.pallas_call(
        _q_weight_kernel,
        grid=(steps,),
        in_specs=[pl.BlockSpec((k, pair_in), lambda j: (0, j))] + cast_in_specs,
        out_specs=[pl.BlockSpec((k, 2 * QK_PAD_DIM), lambda j: (0, j))] + cast_out_specs,
        out_shape=[jax.ShapeDtypeStruct((k, N_HEADS * QK_PAD_DIM), BF16)] + cast_shapes,
        compiler_params=_params("parallel"),
        name="q_weights",
    )(w_uq, *[w for w, _, _ in casts])


def _qkv_up_kernel(cq_ref, ckv_ref, wq_ref, wkv_ref, cos_ref, sin_ref, q_ref, k_ref, vt_ref, *, blk):
    r = _dot(cq_ref[...], wq_ref[...])
    kv = _dot(ckv_ref[...], wkv_ref[...])
    cos = cos_ref[...]
    sin = sin_ref[...]
    for h in range(N_HEADS):
        lo = h * QK_PAD_DIM
        nope = r[:, lo:lo + LANES] * Q_SCALE
        pe = _rope_pair(r[:, lo + LANES:lo + QK_PAD_DIM], cos, sin) * Q_SCALE
        q_ref[:, lo:lo + LANES] = nope.astype(q_ref.dtype)
        q_ref[:, lo + LANES:lo + QK_PAD_DIM] = pe.astype(q_ref.dtype)
    ones = jnp.ones((V_AUG_DIM - V_HEAD_DIM, blk), vt_ref.dtype)
    for h in range(N_HEADS):
        lo = h * (QK_NOPE_DIM + V_HEAD_DIM)
        k_ref[:, h * QK_NOPE_DIM:(h + 1) * QK_NOPE_DIM] = (
            kv[:, lo:lo + QK_NOPE_DIM].astype(k_ref.dtype))
        v = kv[:, lo + QK_NOPE_DIM:lo + QK_NOPE_DIM + V_HEAD_DIM]
        for t in range(v.shape[0] // blk):
            vt_ref[t, h * V_AUG_DIM:h * V_AUG_DIM + V_HEAD_DIM, :] = (
                v[t * blk:(t + 1) * blk, :].T.astype(vt_ref.dtype))
            vt_ref[t, h * V_AUG_DIM + V_HEAD_DIM:(h + 1) * V_AUG_DIM, :] = ones


def _qkv_up(cq, ckv, w_q, w_kv, cos, sin, seq, tm, blk):
    m = cq.shape[0]
    tps = seq // tm
    per = tm // blk
    nq = w_q.shape[1]
    nk = N_HEADS * QK_NOPE_DIM
    nv = N_HEADS * V_AUG_DIM
    return pl.pallas_call(
        functools.partial(_qkv_up_kernel, blk=blk),
        grid=(m // tm,),
        in_specs=[pl.BlockSpec((tm, cq.shape[1]), lambda i: (i, 0)),
                  pl.BlockSpec((tm, ckv.shape[1]), lambda i: (i, 0)),
                  pl.BlockSpec(w_q.shape, lambda i: (0, 0)),
                  pl.BlockSpec(w_kv.shape, lambda i: (0, 0)),
                  pl.BlockSpec((tm, LANES), lambda i: (i % tps, 0)),
                  pl.BlockSpec((tm, LANES), lambda i: (i % tps, 0))],
        out_specs=[pl.BlockSpec((tm, nq), lambda i: (i, 0)),
                   pl.BlockSpec((tm, nk), lambda i: (i, 0)),
                   pl.BlockSpec((per, nv, blk), lambda i: (i, 0, 0))],
        out_shape=[jax.ShapeDtypeStruct((m, nq), BF16),
                   jax.ShapeDtypeStruct((m, nk), BF16),
                   jax.ShapeDtypeStruct((m // blk, nv, blk), BF16)],
        compiler_params=_params("parallel"),
        name="qkv_up",
    )(cq, ckv, w_q, w_kv, cos, sin)


def _attention_kernel(q_ref, kn_ref, kpe_ref, vt_ref, gate_ref, *rest, blk, n_casts):
    cast_in = rest[:n_casts]
    o_ref = rest[n_casts]
    cast_out = rest[n_casts + 1:2 * n_casts + 1]
    kfull_ref = rest[2 * n_casts + 1]
    cast_pieces = [(src, dst, r) for src, dst in zip(cast_in, cast_out)
                   for r in range(0, src.shape[0], CAST_PIECE_ROWS)]
    seq = q_ref.shape[0]
    nblk = seq // blk
    kfull_ref[:, :LANES] = kn_ref[...]
    kfull_ref[:, LANES:] = kpe_ref[...]

    key_chunk = lax.broadcasted_iota(jnp.int32, (blk, blk), 0) // CHUNK
    qry_chunk = lax.broadcasted_iota(jnp.int32, (blk, blk), 1) // CHUNK
    visible = key_chunk <= qry_chunk

    pairs = [(qi, kj) for qi in range(nblk) for kj in range(qi + 1)]
    half = blk // 2

    def scores(qi, kj):
        k0, q0 = kj * blk, qi * blk
        q_lo = q_ref[q0:q0 + half, :]
        q_hi = q_ref[q0 + half:q0 + blk, :]
        if kj != qi:
            kb = kfull_ref[k0:k0 + blk, :]
            return _dot_nt(kb, q_lo), _dot_nt(kb, q_hi)
        s_lo = _dot_nt(kfull_ref[k0:k0 + half, :], q_lo)
        s_hi = _dot_nt(kfull_ref[k0:k0 + blk, :], q_hi)
        return (jnp.where(visible[:half, :half], s_lo, -jnp.inf),
                jnp.where(visible[:, half:], s_hi, -jnp.inf))

    def softmax_step(state, s, vt):
        m_blk = jnp.max(s, axis=0, keepdims=True)
        if state is None:
            return m_blk, _dot(vt, jnp.exp2((s - m_blk).astype(BF16)))
        m_run, acc = state
        m_new = jnp.maximum(m_run, m_blk)
        alpha = jnp.exp2(m_run - m_new)
        return m_new, alpha * acc + _dot(vt, jnp.exp2((s - m_new).astype(BF16)))

    lo_state = hi_state = None
    ahead = [scores(*pairs[n]) for n in range(min(SCORE_LOOKAHEAD, len(pairs)))]
    for n, (qi, kj) in enumerate(pairs):
        s_lo, s_hi = ahead.pop(0)
        if n + SCORE_LOOKAHEAD < len(pairs):
            ahead.append(scores(*pairs[n + SCORE_LOOKAHEAD]))
        vt_lo = vt_ref[kj, :, :half] if kj == qi else vt_ref[kj]
        lo_state = softmax_step(lo_state, s_lo, vt_lo)
        hi_state = softmax_step(hi_state, s_hi, vt_ref[kj])
        if kj == qi:
            acc = jnp.concatenate([lo_state[1], hi_state[1]], axis=1)
            out = (acc[:V_HEAD_DIM] / acc[V_HEAD_DIM:V_HEAD_DIM + 1]).T
            gate = gate_ref[qi * blk:(qi + 1) * blk, :].astype(F32)
            o_ref[qi * blk:(qi + 1) * blk, :] = (out * gate).astype(o_ref.dtype)
            lo_state = hi_state = None
        lo = n * len(cast_pieces) // len(pairs)
        hi = (n + 1) * len(cast_pieces) // len(pairs)
        for src, dst, r in cast_pieces[lo:hi]:
            dst[r:r + CAST_PIECE_ROWS, :] = src[r:r + CAST_PIECE_ROWS, :].astype(dst.dtype)


def _attention(q, k_nope, k_pe, v_t, gate, batch, seq, blk, casts):
    m = q.shape[0]
    nblk = seq // blk
    steps = batch * N_HEADS
    cast_specs, cast_shapes = [], []
    for w, n_rows in casts:
        rows = n_rows // steps
        assert rows * steps == n_rows and rows % CAST_PIECE_ROWS == 0
        cast_specs.append(pl.BlockSpec((rows, w.shape[1]), lambda b, h: (b * N_HEADS + h, 0)))
        cast_shapes.append(jax.ShapeDtypeStruct((n_rows, w.shape[1]), BF16))
    return pl.pallas_call(
        functools.partial(_attention_kernel, blk=blk, n_casts=len(casts)),
        grid=(batch, N_HEADS),
        in_specs=[pl.BlockSpec((seq, QK_PAD_DIM), lambda b, h: (b, h)),
                  pl.BlockSpec((seq, LANES), lambda b, h: (b, h)),
                  pl.BlockSpec((seq, LANES), lambda b, h: (b, 0)),
                  pl.BlockSpec((nblk, V_AUG_DIM, blk), lambda b, h: (b, h, 0)),
                  pl.BlockSpec((seq, LANES), lambda b, h: (b, h))] + cast_specs,
        out_specs=[pl.BlockSpec((seq, V_HEAD_DIM), lambda b, h: (b, h))] + cast_specs,
        out_shape=[jax.ShapeDtypeStruct((m, N_HEADS * V_HEAD_DIM), BF16)] + cast_shapes,
        scratch_shapes=[pltpu.VMEM((seq, QK_PAD_DIM), BF16)],
        compiler_params=_params("parallel", "parallel"),
        name="attention",
    )(q, k_nope, k_pe, v_t, gate, *[w for w, _ in casts])


def _out_proj_kernel(yc_ref, ya_ref, w_ref, x_ref, g_ref, o_ref, *, final_norm, tn):
    kc = yc_ref.shape[1]
    d = o_ref.shape[1]
    yc = yc_ref[...]
    ya = ya_ref[...]
    ss = None
    for t in range(d // tn):
        cols = slice(t * tn, (t + 1) * tn)
        h = x_ref[:, cols] + (_dot(yc, w_ref[:kc, cols]) + _dot(ya, w_ref[kc:, cols]))
        o_ref[:, cols] = h.astype(o_ref.dtype)
        if final_norm:
            part = jnp.sum(h * h, axis=-1, keepdims=True)
            ss = part if ss is None else ss + part
    if final_norm:
        inv = lax.rsqrt(ss / d + NORM_EPS)
        for t in range(d // tn):
            cols = slice(t * tn, (t + 1) * tn)
            o_ref[:, cols] = (o_ref[:, cols] * inv) * g_ref[:, cols]


def _out_proj(y_conv, y_attn, w_o, x, g, final_norm, tm, tn):
    m, d = x.shape
    kc = y_conv.shape[1]
    ka = y_attn.shape[1]
    return pl.pallas_call(
        functools.partial(_out_proj_kernel, final_norm=final_norm, tn=tn),
        grid=(m // tm,),
        in_specs=[pl.BlockSpec((tm, kc), lambda i: (i, 0)),
                  pl.BlockSpec((tm, ka), lambda i: (i, 0)),
                  pl.BlockSpec((kc + ka, d), lambda i: (0, 0), pipeline_mode=pl.Buffered(1)),
                  pl.BlockSpec((tm, d), lambda i: (i, 0)),
                  pl.BlockSpec((1, d), lambda i: (0, 0))],
        out_specs=pl.BlockSpec((tm, d), lambda i: (i, 0)),
        out_shape=jax.ShapeDtypeStruct((m, d), x.dtype),
        compiler_params=pltpu.CompilerParams(dimension_semantics=("parallel",),
                                             vmem_limit_bytes=BIG_VMEM_LIMIT_BYTES),
        name="out_proj",
    )(y_conv, y_attn, w_o, x, g.reshape(1, d))


def _rope_tables(seq):
    pos = jnp.arange(seq, dtype=F32)
    inv_freq = 1.0 / (ROPE_THETA ** (jnp.arange(0, ROPE_DIM, 2, dtype=F32) / ROPE_DIM))
    ang = pos[:, None] * inv_freq[None, :]
    zeros = jnp.zeros((seq, LANES - ROPE_DIM), F32)
    cos = jnp.concatenate([jnp.cos(ang), jnp.cos(ang), zeros], axis=-1)
    sin = jnp.concatenate([jnp.sin(ang), jnp.sin(ang), zeros], axis=-1)
    return cos, sin


def _tile(n, pref):
    return pref if n % pref == 0 else n


def _layer(h, g_in, w_in, conv_w, q_norm_g, w_uq, kv_norm_g, w_ukv, w_out, g_out,
           final_norm, cos, sin, batch, seq):
    m, d = h.shape
    dc = conv_w.shape[1]
    o = 4 * dc

    w_in_t = w_in.T
    kr0 = o + Q_LORA + KV_LORA
    n_lat = Q_LORA + KV_LORA + LANES
    w_q, w_lat_t, w_kv = _q_weights(w_uq, [(w_in_t, o, n_lat), (w_ukv, 0, w_ukv.shape[0])])

    blk = _tile(seq, ATTN_BLOCK)
    xn, cq, ckv, k_pe, w_z_t = _latents(h, g_in, w_lat_t, q_norm_g, kv_norm_g, cos, sin,
                                        w_in_t, kr0 + ROPE_DIM, w_in_t.shape[0] - kr0 - ROPE_DIM,
                                        seq, _tile(seq, 512))
    gate = _gate(xn, w_z_t, _tile(m, 1024), _tile(w_z_t.shape[0], 512))
    q, k_nope, v_t = _qkv_up(cq, ckv, w_q, w_kv, cos, sin, seq, _tile(seq, 512), blk)
    y_attn, w_conv_t, w_o = _attention(q, k_nope, k_pe, v_t, gate, batch, seq, blk,
                                       [(w_in_t, o), (w_out, w_out.shape[0])])
    y_conv = _conv_group(xn, w_conv_t, conv_w, seq, _tile(seq, 1024), _tile(dc, 256))
    return _out_proj(y_conv, y_attn, w_o, h, g_out, final_norm, _tile(m, 256), _tile(d, 512))


def kernel(x, g_in, w_in, conv_w, q_norm_g, w_uq, kv_norm_g, w_ukv, w_out, g_final):
    batch, seq, d = x.shape
    depth = g_in.shape[0]
    cos, sin = _rope_tables(seq)
    h = x.reshape(batch * seq, d)
    for l in range(depth):
        h = _layer(h, g_in[l], w_in[l], conv_w[l], q_norm_g[l], w_uq[l], kv_norm_g[l],
                   w_ukv[l], w_out[l], g_final, l == depth - 1, cos, sin, batch, seq)
    return h.reshape(batch, seq, d)
```

```python
import functools

import jax
import jax.numpy as jnp
import numpy as np
from jax import lax
from jax.experimental import pallas as pl
from jax.experimental.pallas import tpu as pltpu

CHUNK = 64
CONV_WIDTH = 3
N_HEADS = 16
V_HEAD_DIM = 128
QK_NOPE_DIM = 128
ROPE_DIM = 64
ROPE_HALF = ROPE_DIM // 2
Q_LORA = 1024
KV_LORA = 512
ROPE_THETA = 10000.0
NORM_EPS = 1e-6
ATTN_SCALE = (QK_NOPE_DIM + ROPE_DIM) ** -0.5
LOG2_E = float(np.log2(np.e))
Q_SCALE = ATTN_SCALE * LOG2_E

LANES = 128
SUBLANES = 8
QK_PAD_DIM = 2 * LANES
V_AUG_DIM = V_HEAD_DIM + 16
ATTN_BLOCK = 512
SCORE_LOOKAHEAD = 2
CAST_PIECE_ROWS = 16
VMEM_LIMIT_BYTES = 56 * 1024 * 1024
BIG_VMEM_LIMIT_BYTES = 60 * 1024 * 1024

BF16 = jnp.bfloat16
F32 = jnp.float32


def _params(*semantics):
    return pltpu.CompilerParams(dimension_semantics=semantics,
                                vmem_limit_bytes=VMEM_LIMIT_BYTES)


def _dot(a, b):
    return jnp.dot(a, b, preferred_element_type=F32)


def _dot_nt(a, b):
    return lax.dot_general(a, b, (((1,), (1,)), ((), ())), preferred_element_type=F32)


def _rmsnorm_rows(x, g):
    ms = jnp.mean(x * x, axis=-1, keepdims=True)
    return (x * lax.rsqrt(ms + NORM_EPS)) * g


def _silu(z):
    return z * (0.5 * (jnp.tanh(0.5 * z) + 1.0))


def _conv_group_kernel(xn_ref, wb_ref, wc_ref, wh_ref, wz_ref, cw_ref, y_ref, carry_ref,
                       *, tiles_per_seq):
    i = pl.program_id(0)
    j = pl.program_id(1)
    xn = xn_ref[...]
    u = _dot_nt(xn, wc_ref[...]) * _dot_nt(xn, wh_ref[...])
    tm, tc = u.shape

    @pl.when(i % tiles_per_seq == 0)
    def _():
        carry_ref[j] = jnp.zeros((SUBLANES, tc), F32)

    prev = carry_ref[j]
    carry_ref[j] = u[tm - SUBLANES:, :]

    row = lax.broadcasted_iota(jnp.int32, (SUBLANES, tc), 0)

    def shifted(k):
        body = pltpu.roll(u, k, 0)
        head = jnp.where(row < k, pltpu.roll(prev, k, 0), body[:SUBLANES, :])
        return jnp.concatenate([head, body[SUBLANES:, :]], axis=0)

    cw = cw_ref[...]
    conv = cw[0:1, :] * shifted(2) + cw[1:2, :] * shifted(1) + cw[2:3, :] * u
    y = (_dot_nt(xn, wb_ref[...]) * conv) * _silu(_dot_nt(xn, wz_ref[...]))
    y_ref[...] = y.astype(y_ref.dtype)


def _conv_group(xn, w_t, conv_w, seq, tm, tc):
    m, d = xn.shape
    dc = conv_w.shape[1]
    nj = dc // tc

    def w_spec(k):
        return pl.BlockSpec((tc, d), lambda i, j: (j + k * nj, 0))

    return pl.pallas_call(
        functools.partial(_conv_group_kernel, tiles_per_seq=seq // tm),
        grid=(m // tm, nj),
        in_specs=[pl.BlockSpec((tm, d), lambda i, j: (i, 0)),
                  w_spec(0), w_spec(1), w_spec(2), w_spec(3),
                  pl.BlockSpec((CONV_WIDTH, tc), lambda i, j: (0, j))],
        out_specs=pl.BlockSpec((tm, tc), lambda i, j: (i, j)),
        out_shape=jax.ShapeDtypeStruct((m, dc), BF16),
        scratch_shapes=[pltpu.VMEM((nj, SUBLANES, tc), F32)],
        compiler_params=_params("arbitrary", "arbitrary"),
        name="conv_group",
    )(xn, w_t, w_t, w_t, w_t, conv_w)


def _rope_low_half(t, cos, sin):
    lane = lax.broadcasted_iota(jnp.int32, t.shape, 1)
    partner = jnp.where(lane < ROPE_HALF,
                        -pltpu.roll(t, LANES - ROPE_HALF, 1),
                        pltpu.roll(t, ROPE_HALF, 1))
    return jnp.where(lane < ROPE_DIM, t * cos + partner * sin, 0.0)


def _latent_kernel(x_ref, g_ref, w_ref, gq_ref, gkv_ref, cos_ref, sin_ref, cast_in_ref,
                   xn_ref, cq_ref, ckv_ref, kpe_ref, cast_out_ref):
    cast_out_ref[...] = cast_in_ref[...].astype(cast_out_ref.dtype)
    g = g_ref[...]
    c = _dot_nt((x_ref[...] * g).astype(BF16), w_ref[...])
    x = x_ref[...]
    inv = lax.rsqrt(jnp.mean(x * x, axis=-1, keepdims=True) + NORM_EPS)
    xn_ref[...] = ((x * inv) * g).astype(xn_ref.dtype)
    c = c * inv
    cq_ref[...] = _rmsnorm_rows(c[:, :Q_LORA], gq_ref[...]).astype(cq_ref.dtype)
    ckv_ref[...] = _rmsnorm_rows(c[:, Q_LORA:Q_LORA + KV_LORA], gkv_ref[...]).astype(ckv_ref.dtype)
    kpe_ref[...] = _rope_low_half(c[:, Q_LORA + KV_LORA:], cos_ref[...],
                                  sin_ref[...]).astype(kpe_ref.dtype)


def _latents(x, g_in, w_t, gq, gkv, cos, sin, w_cast, cast_row0, cast_rows, seq, tm):
    m, d = x.shape
    tps = seq // tm
    n = w_t.shape[0]
    assert n == Q_LORA + KV_LORA + LANES
    cr = cast_rows // (m // tm)
    assert cr * (m // tm) == cast_rows and cr % CAST_PIECE_ROWS == 0 and cast_row0 % SUBLANES == 0
    return pl.pallas_call(
        _latent_kernel,
        grid=(m // tm,),
        in_specs=[pl.BlockSpec((tm, d), lambda i: (i, 0)),
                  pl.BlockSpec((1, d), lambda i: (0, 0)),
                  pl.BlockSpec((n, d), lambda i: (0, 0), pipeline_mode=pl.Buffered(1)),
                  pl.BlockSpec((1, Q_LORA), lambda i: (0, 0)),
                  pl.BlockSpec((1, KV_LORA), lambda i: (0, 0)),
                  pl.BlockSpec((tm, LANES), lambda i: (i % tps, 0)),
                  pl.BlockSpec((tm, LANES), lambda i: (i % tps, 0)),
                  pl.BlockSpec((pl.Element(cr), pl.Element(w_cast.shape[1])),
                               lambda i: (pl.multiple_of(cast_row0 + i * cr, SUBLANES), 0))],
        out_specs=[pl.BlockSpec((tm, d), lambda i: (i, 0)),
                   pl.BlockSpec((tm, Q_LORA), lambda i: (i, 0)),
                   pl.BlockSpec((tm, KV_LORA), lambda i: (i, 0)),
                   pl.BlockSpec((tm, LANES), lambda i: (i, 0)),
                   pl.BlockSpec((cr, w_cast.shape[1]), lambda i: (i, 0))],
        out_shape=[jax.ShapeDtypeStruct((m, d), BF16),
                   jax.ShapeDtypeStruct((m, Q_LORA), BF16),
                   jax.ShapeDtypeStruct((m, KV_LORA), BF16),
                   jax.ShapeDtypeStruct((m, LANES), BF16),
                   jax.ShapeDtypeStruct((cast_rows, w_cast.shape[1]), BF16)],
        compiler_params=pltpu.CompilerParams(dimension_semantics=("parallel",),
                                             vmem_limit_bytes=BIG_VMEM_LIMIT_BYTES),
        name="latents",
    )(x, g_in.reshape(1, d), w_t, gq.reshape(1, -1), gkv.reshape(1, -1), cos, sin, w_cast)


def _gate_kernel(xn_ref, w_ref, o_ref):
    o_ref[...] = _silu(_dot_nt(xn_ref[...], w_ref[...])).astype(o_ref.dtype)


def _gate(xn, w_z_t, tm, tn):
    m, d = xn.shape
    n = w_z_t.shape[0]
    return pl.pallas_call(
        _gate_kernel,
        grid=(m // tm, n // tn),
        in_specs=[pl.BlockSpec((tm, d), lambda i, j: (i, 0)),
                  pl.BlockSpec((tn, d), lambda i, j: (j, 0))],
        out_specs=pl.BlockSpec((tm, tn), lambda i, j: (i, j)),
        out_shape=jax.ShapeDtypeStruct((m, n), BF16),
        compiler_params=_params("parallel", "parallel"),
        name="attn_gate",
    )(xn, w_z_t)


def _q_weight_kernel(w_ref, cast_in_ref, nope_ref, rope_ref, cast_out_ref):
    cast_out_ref[...] = cast_in_ref[...].astype(cast_out_ref.dtype)
    lane = lax.broadcasted_iota(jnp.int32, (w_ref.shape[0], LANES), 1)
    t0 = w_ref[:, :LANES]
    t1 = w_ref[:, LANES:2 * LANES]
    t2 = w_ref[:, 2 * LANES:]
    nope_ref[:, :LANES] = t0.astype(nope_ref.dtype)
    nope_ref[:, LANES:] = jnp.where(lane < ROPE_DIM, pltpu.roll(t1, ROPE_DIM, 1),
                                    pltpu.roll(t2, ROPE_DIM, 1)).astype(nope_ref.dtype)
    rope_ref[...] = jnp.where(lane < ROPE_DIM, t1, t2).astype(rope_ref.dtype)


def _q_weights(w_uq, w_cast, cast_row0, cast_rows):
    k = w_uq.shape[0]
    pair_in = 2 * (QK_NOPE_DIM + ROPE_DIM)
    steps = N_HEADS // 2
    cr = cast_rows // steps
    assert cr * steps == cast_rows and cr % CAST_PIECE_ROWS == 0 and cast_row0 % SUBLANES == 0
    return pl.pallas_call(
        _q_weight_kernel,
        grid=(steps,),
        in_specs=[pl.BlockSpec((k, pair_in), lambda j: (0, j)),
                  pl.BlockSpec((pl.Element(cr), pl.Element(w_cast.shape[1])),
                               lambda j: (pl.multiple_of(cast_row0 + j * cr, SUBLANES), 0))],
        out_specs=[pl.BlockSpec((k, 2 * QK_NOPE_DIM), lambda j: (0, j)),
                   pl.BlockSpec((k, 2 * ROPE_DIM), lambda j: (0, j)),
                   pl.BlockSpec((cr, w_cast.shape[1]), lambda j: (j, 0))],
        out_shape=[jax.ShapeDtypeStruct((k, N_HEADS * QK_NOPE_DIM), BF16),
                   jax.ShapeDtypeStruct((k, N_HEADS * ROPE_DIM), BF16),
                   jax.ShapeDtypeStruct((cast_rows, w_cast.shape[1]), BF16)],
        compiler_params=_params("parallel"),
        name="q_weights",
    )(w_uq, w_cast)


def _qkv_up_kernel(cq_ref, ckv_ref, wqn_ref, wqr_ref, wkv_ref, cos_ref, sin_ref,
                   q_ref, k_ref, vt_ref, *, blk):
    cq = cq_ref[...]
    nope = _dot(cq, wqn_ref[...])
    rope = _dot(cq, wqr_ref[...])
    kv = _dot(ckv_ref[...], wkv_ref[...])
    cos = cos_ref[...]
    sin = sin_ref[...]
    for h in range(N_HEADS):
        lo = h * QK_PAD_DIM
        pair = rope[:, (h // 2) * LANES:(h // 2 + 1) * LANES]
        if h % 2:
            pair = pltpu.roll(pair, ROPE_DIM, 1)
        q_ref[:, lo:lo + LANES] = (
            nope[:, h * QK_NOPE_DIM:(h + 1) * QK_NOPE_DIM] * Q_SCALE).astype(q_ref.dtype)
        q_ref[:, lo + LANES:lo + QK_PAD_DIM] = (
            _rope_low_half(pair, cos, sin) * Q_SCALE).astype(q_ref.dtype)
    ones = jnp.ones((V_AUG_DIM - V_HEAD_DIM, blk), vt_ref.dtype)
    for h in range(N_HEADS):
        lo = h * (QK_NOPE_DIM + V_HEAD_DIM)
        k_ref[:, h * QK_NOPE_DIM:(h + 1) * QK_NOPE_DIM] = (
            kv[:, lo:lo + QK_NOPE_DIM].astype(k_ref.dtype))
        v = kv[:, lo + QK_NOPE_DIM:lo + QK_NOPE_DIM + V_HEAD_DIM]
        for t in range(v.shape[0] // blk):
            vt_ref[t, h * V_AUG_DIM:h * V_AUG_DIM + V_HEAD_DIM, :] = (
                v[t * blk:(t + 1) * blk, :].T.astype(vt_ref.dtype))
            vt_ref[t, h * V_AUG_DIM + V_HEAD_DIM:(h + 1) * V_AUG_DIM, :] = ones


def _qkv_up(cq, ckv, w_qn, w_qr, w_kv, cos, sin, seq, tm, blk):
    m = cq.shape[0]
    tps = seq // tm
    per = tm // blk
    nq = N_HEADS * QK_PAD_DIM
    nk = N_HEADS * QK_NOPE_DIM
    nv = N_HEADS * V_AUG_DIM
    return pl.pallas_call(
        functools.partial(_qkv_up_kernel, blk=blk),
        grid=(m // tm,),
        in_specs=[pl.BlockSpec((tm, cq.shape[1]), lambda i: (i, 0)),
                  pl.BlockSpec((tm, ckv.shape[1]), lambda i: (i, 0)),
                  pl.BlockSpec(w_qn.shape, lambda i: (0, 0)),
                  pl.BlockSpec(w_qr.shape, lambda i: (0, 0)),
                  pl.BlockSpec(w_kv.shape, lambda i: (0, 0)),
                  pl.BlockSpec((tm, LANES), lambda i: (i % tps, 0)),
                  pl.BlockSpec((tm, LANES), lambda i: (i % tps, 0))],
        out_specs=[pl.BlockSpec((tm, nq), lambda i: (i, 0)),
                   pl.BlockSpec((tm, nk), lambda i: (i, 0)),
                   pl.BlockSpec((per, nv, blk), lambda i: (i, 0, 0))],
        out_shape=[jax.ShapeDtypeStruct((m, nq), BF16),
                   jax.ShapeDtypeStruct((m, nk), BF16),
                   jax.ShapeDtypeStruct((m // blk, nv, blk), BF16)],
        compiler_params=_params("parallel"),
        name="qkv_up",
    )(cq, ckv, w_qn, w_qr, w_kv, cos, sin)


def _attention_kernel(q_ref, kn_ref, kpe_ref, vt_ref, gate_ref, *rest, blk, n_casts):
    cast_in = rest[:n_casts]
    o_ref = rest[n_casts]
    cast_out = rest[n_casts + 1:2 * n_casts + 1]
    kfull_ref = rest[2 * n_casts + 1]
    cast_pieces = [(src, dst, r) for src, dst in zip(cast_in, cast_out)
                   for r in range(0, src.shape[0], CAST_PIECE_ROWS)]
    seq = q_ref.shape[0]
    nblk = seq // blk
    kfull_ref[:, :LANES] = kn_ref[...]
    kfull_ref[:, LANES:] = kpe_ref[...]

    key_chunk = lax.broadcasted_iota(jnp.int32, (blk, blk), 0) // CHUNK
    qry_chunk = lax.broadcasted_iota(jnp.int32, (blk, blk), 1) // CHUNK
    visible = key_chunk <= qry_chunk

    pairs = [(qi, kj) for qi in range(nblk) for kj in range(qi + 1)]
    half = blk // 2

    def scores(qi, kj):
        k0, q0 = kj * blk, qi * blk
        q_lo = q_ref[q0:q0 + half, :]
        q_hi = q_ref[q0 + half:q0 + blk, :]
        if kj != qi:
            kb = kfull_ref[k0:k0 + blk, :]
            return _dot_nt(kb, q_lo), _dot_nt(kb, q_hi)
        s_lo = _dot_nt(kfull_ref[k0:k0 + half, :], q_lo)
        s_hi = _dot_nt(kfull_ref[k0:k0 + blk, :], q_hi)
        return (jnp.where(visible[:half, :half], s_lo, -jnp.inf),
                jnp.where(visible[:, half:], s_hi, -jnp.inf))

    def softmax_step(state, s, vt):
        m_blk = jnp.max(s, axis=0, keepdims=True)
        if state is None:
            return m_blk, _dot(vt, jnp.exp2((s - m_blk).astype(BF16)))
        m_run, acc = state
        m_new = jnp.maximum(m_run, m_blk)
        alpha = jnp.exp2(m_run - m_new)
        return m_new, alpha * acc + _dot(vt, jnp.exp2((s - m_new).astype(BF16)))

    lo_state = hi_state = None
    ahead = [scores(*pairs[n]) for n in range(min(SCORE_LOOKAHEAD, len(pairs)))]
    for n, (qi, kj) in enumerate(pairs):
        s_lo, s_hi = ahead.pop(0)
        if n + SCORE_LOOKAHEAD < len(pairs):
            ahead.append(scores(*pairs[n + SCORE_LOOKAHEAD]))
        vt_lo = vt_ref[kj, :, :half] if kj == qi else vt_ref[kj]
        lo_state = softmax_step(lo_state, s_lo, vt_lo)
        hi_state = softmax_step(hi_state, s_hi, vt_ref[kj])
        if kj == qi:
            acc = jnp.concatenate([lo_state[1], hi_state[1]], axis=1)
            out = (acc[:V_HEAD_DIM] / acc[V_HEAD_DIM:V_HEAD_DIM + 1]).T
            gate = gate_ref[qi * blk:(qi + 1) * blk, :].astype(F32)
            o_ref[qi * blk:(qi + 1) * blk, :] = (out * gate).astype(o_ref.dtype)
            lo_state = hi_state = None
        lo = n * len(cast_pieces) // len(pairs)
        hi = (n + 1) * len(cast_pieces) // len(pairs)
        for src, dst, r in cast_pieces[lo:hi]:
            dst[r:r + CAST_PIECE_ROWS, :] = src[r:r + CAST_PIECE_ROWS, :].astype(dst.dtype)


def _attention(q, k_nope, k_pe, v_t, gate, batch, seq, blk, casts):
    m = q.shape[0]
    nblk = seq // blk
    steps = batch * N_HEADS
    cast_specs, cast_shapes = [], []
    for w, n_rows in casts:
        rows = n_rows // steps
        assert rows * steps == n_rows and rows % CAST_PIECE_ROWS == 0
        cast_specs.append(pl.BlockSpec((rows, w.shape[1]), lambda b, h: (b * N_HEADS + h, 0)))
        cast_shapes.append(jax.ShapeDtypeStruct((n_rows, w.shape[1]), BF16))
    return pl.pallas_call(
        functools.partial(_attention_kernel, blk=blk, n_casts=len(casts)),
        grid=(batch, N_HEADS),
        in_specs=[pl.BlockSpec((seq, QK_PAD_DIM), lambda b, h: (b, h)),
                  pl.BlockSpec((seq, LANES), lambda b, h: (b, h)),
                  pl.BlockSpec((seq, LANES), lambda b, h: (b, 0)),
                  pl.BlockSpec((nblk, V_AUG_DIM, blk), lambda b, h: (b, h, 0)),
                  pl.BlockSpec((seq, LANES), lambda b, h: (b, h))] + cast_specs,
        out_specs=[pl.BlockSpec((seq, V_HEAD_DIM), lambda b, h: (b, h))] + cast_specs,
        out_shape=[jax.ShapeDtypeStruct((m, N_HEADS * V_HEAD_DIM), BF16)] + cast_shapes,
        scratch_shapes=[pltpu.VMEM((seq, QK_PAD_DIM), BF16)],
        compiler_params=_params("parallel", "parallel"),
        name="attention",
    )(q, k_nope, k_pe, v_t, gate, *[w for w, _ in casts])


def _out_proj_kernel(yc_ref, ya_ref, w_ref, x_ref, g_ref, o_ref, *, final_norm, tn):
    kc = yc_ref.shape[1]
    d = o_ref.shape[1]
    yc = yc_ref[...]
    ya = ya_ref[...]
    ss = None
    for t in range(d // tn):
        cols = slice(t * tn, (t + 1) * tn)
        h = x_ref[:, cols] + (_dot(yc, w_ref[:kc, cols]) + _dot(ya, w_ref[kc:, cols]))
        o_ref[:, cols] = h.astype(o_ref.dtype)
        if final_norm:
            part = jnp.sum(h * h, axis=-1, keepdims=True)
            ss = part if ss is None else ss + part
    if final_norm:
        inv = lax.rsqrt(ss / d + NORM_EPS)
        for t in range(d // tn):
            cols = slice(t * tn, (t + 1) * tn)
            o_ref[:, cols] = (o_ref[:, cols] * inv) * g_ref[:, cols]


def _out_proj(y_conv, y_attn, w_o, x, g, final_norm, tm, tn):
    m, d = x.shape
    kc = y_conv.shape[1]
    ka = y_attn.shape[1]
    return pl.pallas_call(
        functools.partial(_out_proj_kernel, final_norm=final_norm, tn=tn),
        grid=(m // tm,),
        in_specs=[pl.BlockSpec((tm, kc), lambda i: (i, 0)),
                  pl.BlockSpec((tm, ka), lambda i: (i, 0)),
                  pl.BlockSpec((kc + ka, d), lambda i: (0, 0), pipeline_mode=pl.Buffered(1)),
                  pl.BlockSpec((tm, d), lambda i: (i, 0)),
                  pl.BlockSpec((1, d), lambda i: (0, 0))],
        out_specs=pl.BlockSpec((tm, d), lambda i: (i, 0)),
        out_shape=jax.ShapeDtypeStruct((m, d), x.dtype),
        compiler_params=pltpu.CompilerParams(dimension_semantics=("parallel",),
                                             vmem_limit_bytes=BIG_VMEM_LIMIT_BYTES),
        name="out_proj",
    )(y_conv, y_attn, w_o, x, g.reshape(1, d))


def _rope_tables(seq):
    pos = jnp.arange(seq, dtype=F32)
    inv_freq = 1.0 / (ROPE_THETA ** (jnp.arange(0, ROPE_DIM, 2, dtype=F32) / ROPE_DIM))
    ang = pos[:, None] * inv_freq[None, :]
    zeros = jnp.zeros((seq, LANES - ROPE_DIM), F32)
    cos = jnp.concatenate([jnp.cos(ang), jnp.cos(ang), zeros], axis=-1)
    sin = jnp.concatenate([jnp.sin(ang), jnp.sin(ang), zeros], axis=-1)
    return cos, sin


def _tile(n, pref):
    return pref if n % pref == 0 else n


def _layer(h, g_in, w_in, conv_w, q_norm_g, w_uq, kv_norm_g, w_ukv, w_out, g_out,
           final_norm, cos, sin, batch, seq):
    m, d = h.shape
    dc = conv_w.shape[1]
    o = 4 * dc

    w_in_t = w_in.T
    kr0 = o + Q_LORA + KV_LORA
    n_lat = Q_LORA + KV_LORA + LANES
    w_qn, w_qr, w_lat_t = _q_weights(w_uq, w_in_t, o, n_lat)
    w_kv = w_ukv.astype(BF16)

    blk = _tile(seq, ATTN_BLOCK)
    xn, cq, ckv, k_pe, w_z_t = _latents(h, g_in, w_lat_t, q_norm_g, kv_norm_g, cos, sin,
                                        w_in_t, kr0 + ROPE_DIM, w_in_t.shape[0] - kr0 - ROPE_DIM,
                                        seq, _tile(seq, 512))
    gate = _gate(xn, w_z_t, _tile(m, 1024), _tile(w_z_t.shape[0], 1024))
    q, k_nope, v_t = _qkv_up(cq, ckv, w_qn, w_qr, w_kv, cos, sin, seq, _tile(seq, 512), blk)
    y_attn, w_conv_t, w_o = _attention(q, k_nope, k_pe, v_t, gate, batch, seq, blk,
                                       [(w_in_t, o), (w_out, w_out.shape[0])])
    y_conv = _conv_group(xn, w_conv_t, conv_w, seq, _tile(seq, 1024), _tile(dc, 256))
    return _out_proj(y_conv, y_attn, w_o, h, g_out, final_norm, _tile(m, 256), _tile(d, 512))


def kernel(x, g_in, w_in, conv_w, q_norm_g, w_uq, kv_norm_g, w_ukv, w_out, g_final):
    batch, seq, d = x.shape
    depth = g_in.shape[0]
    cos, sin = _rope_tables(seq)
    h = x.reshape(batch * seq, d)
    for l in range(depth):
        h = _layer(h, g_in[l], w_in[l], conv_w[l], q_norm_g[l], w_uq[l], kv_norm_g[l],
                   w_ukv[l], w_out[l], g_final, l == depth - 1, cos, sin, batch, seq)
    return h.reshape(batch, seq, d)
```

```python
import functools

import jax
import jax.numpy as jnp
import numpy as np
from jax import lax
from jax.experimental import pallas as pl
from jax.experimental.pallas import tpu as pltpu

CHUNK = 64
CONV_WIDTH = 3
N_HEADS = 16
V_HEAD_DIM = 128
QK_NOPE_DIM = 128
ROPE_DIM = 64
ROPE_HALF = ROPE_DIM // 2
Q_LORA = 1024
KV_LORA = 512
ROPE_THETA = 10000.0
NORM_EPS = 1e-6
ATTN_SCALE = (QK_NOPE_DIM + ROPE_DIM) ** -0.5
LOG2_E = float(np.log2(np.e))
Q_SCALE = ATTN_SCALE * LOG2_E

LANES = 128
SUBLANES = 8
QK_PAD_DIM = 2 * LANES
V_AUG_DIM = V_HEAD_DIM + 16
ATTN_BLOCK = 512
SCORE_LOOKAHEAD = 2
CONV_ROW_PARTS = 2
CAST_PIECE_ROWS = 16
VMEM_LIMIT_BYTES = 56 * 1024 * 1024
BIG_VMEM_LIMIT_BYTES = 60 * 1024 * 1024

BF16 = jnp.bfloat16
F32 = jnp.float32


def _params(*semantics):
    return pltpu.CompilerParams(dimension_semantics=semantics,
                                vmem_limit_bytes=VMEM_LIMIT_BYTES)


def _dot(a, b):
    return jnp.dot(a, b, preferred_element_type=F32)


def _dot_nt(a, b):
    return lax.dot_general(a, b, (((1,), (1,)), ((), ())), preferred_element_type=F32)


def _rmsnorm_rows(x, g):
    ms = jnp.mean(x * x, axis=-1, keepdims=True)
    return (x * lax.rsqrt(ms + NORM_EPS)) * g


def _silu(z):
    return z * (0.5 * (jnp.tanh(0.5 * z) + 1.0))


def _conv_group_kernel(xn_ref, wb_ref, wc_ref, wh_ref, wz_ref, cw_ref, y_ref, carry_ref,
                       *, tiles_per_seq, row_parts):
    i = pl.program_id(0)
    j = pl.program_id(1)
    tc = y_ref.shape[1]
    part = xn_ref.shape[0] // row_parts

    @pl.when(i % tiles_per_seq == 0)
    def _():
        carry_ref[j] = jnp.zeros((SUBLANES, tc), F32)

    prev = carry_ref[j]
    row = lax.broadcasted_iota(jnp.int32, (SUBLANES, tc), 0)
    cw = cw_ref[...]
    for p in range(row_parts):
        rows = slice(p * part, (p + 1) * part)
        xn = xn_ref[rows, :]
        u = _dot_nt(xn, wc_ref[...]) * _dot_nt(xn, wh_ref[...])

        def shifted(k):
            body = pltpu.roll(u, k, 0)
            head = jnp.where(row < k, pltpu.roll(prev, k, 0), body[:SUBLANES, :])
            return jnp.concatenate([head, body[SUBLANES:, :]], axis=0)

        conv = cw[0:1, :] * shifted(2) + cw[1:2, :] * shifted(1) + cw[2:3, :] * u
        y = (_dot_nt(xn, wb_ref[...]) * conv) * _silu(_dot_nt(xn, wz_ref[...]))
        y_ref[rows, :] = y.astype(y_ref.dtype)
        prev = u[part - SUBLANES:, :]
    carry_ref[j] = prev


def _conv_group(xn, w_t, conv_w, seq, tm, tc):
    m, d = xn.shape
    dc = conv_w.shape[1]
    nj = dc // tc

    def w_spec(k):
        return pl.BlockSpec((tc, d), lambda i, j: (j + k * nj, 0))

    return pl.pallas_call(
        functools.partial(_conv_group_kernel, tiles_per_seq=seq // tm, row_parts=CONV_ROW_PARTS),
        grid=(m // tm, nj),
        in_specs=[pl.BlockSpec((tm, d), lambda i, j: (i, 0)),
                  w_spec(0), w_spec(1), w_spec(2), w_spec(3),
                  pl.BlockSpec((CONV_WIDTH, tc), lambda i, j: (0, j))],
        out_specs=pl.BlockSpec((tm, tc), lambda i, j: (i, j)),
        out_shape=jax.ShapeDtypeStruct((m, dc), BF16),
        scratch_shapes=[pltpu.VMEM((nj, SUBLANES, tc), F32)],
        compiler_params=_params("arbitrary", "arbitrary"),
        name="conv_group",
    )(xn, w_t, w_t, w_t, w_t, conv_w)


def _rope_low_half(t, cos, sin):
    lane = lax.broadcasted_iota(jnp.int32, t.shape, 1)
    partner = jnp.where(lane < ROPE_HALF,
                        -pltpu.roll(t, LANES - ROPE_HALF, 1),
                        pltpu.roll(t, ROPE_HALF, 1))
    return jnp.where(lane < ROPE_DIM, t * cos + partner * sin, 0.0)


def _latent_kernel(x_ref, g_ref, w_ref, gq_ref, gkv_ref, cos_ref, sin_ref, cast_in_ref,
                   xn_ref, cq_ref, ckv_ref, kpe_ref, cast_out_ref):
    cast_out_ref[...] = cast_in_ref[...].astype(cast_out_ref.dtype)
    g = g_ref[...]
    c = _dot_nt((x_ref[...] * g).astype(BF16), w_ref[...])
    x = x_ref[...]
    inv = lax.rsqrt(jnp.mean(x * x, axis=-1, keepdims=True) + NORM_EPS)
    xn_ref[...] = ((x * inv) * g).astype(xn_ref.dtype)
    c = c * inv
    cq_ref[...] = _rmsnorm_rows(c[:, :Q_LORA], gq_ref[...]).astype(cq_ref.dtype)
    ckv_ref[...] = _rmsnorm_rows(c[:, Q_LORA:Q_LORA + KV_LORA], gkv_ref[...]).astype(ckv_ref.dtype)
    kpe_ref[...] = _rope_low_half(c[:, Q_LORA + KV_LORA:], cos_ref[...],
                                  sin_ref[...]).astype(kpe_ref.dtype)


def _latents(x, g_in, w_t, gq, gkv, cos, sin, w_cast, cast_row0, cast_rows, seq, tm):
    m, d = x.shape
    tps = seq // tm
    n = w_t.shape[0]
    assert n == Q_LORA + KV_LORA + LANES
    cr = cast_rows // (m // tm)
    assert cr * (m // tm) == cast_rows and cr % CAST_PIECE_ROWS == 0 and cast_row0 % SUBLANES == 0
    return pl.pallas_call(
        _latent_kernel,
        grid=(m // tm,),
        in_specs=[pl.BlockSpec((tm, d), lambda i: (i, 0)),
                  pl.BlockSpec((1, d), lambda i: (0, 0)),
                  pl.BlockSpec((n, d), lambda i: (0, 0), pipeline_mode=pl.Buffered(1)),
                  pl.BlockSpec((1, Q_LORA), lambda i: (0, 0)),
                  pl.BlockSpec((1, KV_LORA), lambda i: (0, 0)),
                  pl.BlockSpec((tm, LANES), lambda i: (i % tps, 0)),
                  pl.BlockSpec((tm, LANES), lambda i: (i % tps, 0)),
                  pl.BlockSpec((pl.Element(cr), pl.Element(w_cast.shape[1])),
                               lambda i: (pl.multiple_of(cast_row0 + i * cr, SUBLANES), 0))],
        out_specs=[pl.BlockSpec((tm, d), lambda i: (i, 0)),
                   pl.BlockSpec((tm, Q_LORA), lambda i: (i, 0)),
                   pl.BlockSpec((tm, KV_LORA), lambda i: (i, 0)),
                   pl.BlockSpec((tm, LANES), lambda i: (i, 0)),
                   pl.BlockSpec((cr, w_cast.shape[1]), lambda i: (i, 0))],
        out_shape=[jax.ShapeDtypeStruct((m, d), BF16),
                   jax.ShapeDtypeStruct((m, Q_LORA), BF16),
                   jax.ShapeDtypeStruct((m, KV_LORA), BF16),
                   jax.ShapeDtypeStruct((m, LANES), BF16),
                   jax.ShapeDtypeStruct((cast_rows, w_cast.shape[1]), BF16)],
        compiler_params=pltpu.CompilerParams(dimension_semantics=("parallel",),
                                             vmem_limit_bytes=BIG_VMEM_LIMIT_BYTES),
        name="latents",
    )(x, g_in.reshape(1, d), w_t, gq.reshape(1, -1), gkv.reshape(1, -1), cos, sin, w_cast)


def _gate_kernel(xn_ref, w_ref, o_ref):
    o_ref[...] = _silu(_dot_nt(xn_ref[...], w_ref[...])).astype(o_ref.dtype)


def _gate(xn, w_z_t, tm, tn):
    m, d = xn.shape
    n = w_z_t.shape[0]
    return pl.pallas_call(
        _gate_kernel,
        grid=(m // tm, n // tn),
        in_specs=[pl.BlockSpec((tm, d), lambda i, j: (i, 0)),
                  pl.BlockSpec((tn, d), lambda i, j: (j, 0))],
        out_specs=pl.BlockSpec((tm, tn), lambda i, j: (i, j)),
        out_shape=jax.ShapeDtypeStruct((m, n), BF16),
        compiler_params=_params("parallel", "parallel"),
        name="attn_gate",
    )(xn, w_z_t)


def _q_weight_kernel(w_ref, cast_in_ref, nope_ref, rope_ref, cast_out_ref):
    cast_out_ref[...] = cast_in_ref[...].astype(cast_out_ref.dtype)
    lane = lax.broadcasted_iota(jnp.int32, (w_ref.shape[0], LANES), 1)
    t0 = w_ref[:, :LANES]
    t1 = w_ref[:, LANES:2 * LANES]
    t2 = w_ref[:, 2 * LANES:]
    nope_ref[:, :LANES] = t0.astype(nope_ref.dtype)
    nope_ref[:, LANES:] = jnp.where(lane < ROPE_DIM, pltpu.roll(t1, ROPE_DIM, 1),
                                    pltpu.roll(t2, ROPE_DIM, 1)).astype(nope_ref.dtype)
    rope_ref[...] = jnp.where(lane < ROPE_DIM, t1, t2).astype(rope_ref.dtype)


def _q_weights(w_uq, w_cast, cast_row0, cast_rows):
    k = w_uq.shape[0]
    pair_in = 2 * (QK_NOPE_DIM + ROPE_DIM)
    steps = N_HEADS // 2
    cr = cast_rows // steps
    assert cr * steps == cast_rows and cr % CAST_PIECE_ROWS == 0 and cast_row0 % SUBLANES == 0
    return pl.pallas_call(
        _q_weight_kernel,
        grid=(steps,),
        in_specs=[pl.BlockSpec((k, pair_in), lambda j: (0, j)),
                  pl.BlockSpec((pl.Element(cr), pl.Element(w_cast.shape[1])),
                               lambda j: (pl.multiple_of(cast_row0 + j * cr, SUBLANES), 0))],
        out_specs=[pl.BlockSpec((k, 2 * QK_NOPE_DIM), lambda j: (0, j)),
                   pl.BlockSpec((k, 2 * ROPE_DIM), lambda j: (0, j)),
                   pl.BlockSpec((cr, w_cast.shape[1]), lambda j: (j, 0))],
        out_shape=[jax.ShapeDtypeStruct((k, N_HEADS * QK_NOPE_DIM), BF16),
                   jax.ShapeDtypeStruct((k, N_HEADS * ROPE_DIM), BF16),
                   jax.ShapeDtypeStruct((cast_rows, w_cast.shape[1]), BF16)],
        compiler_params=_params("parallel"),
        name="q_weights",
    )(w_uq, w_cast)


def _qkv_up_kernel(cq_ref, ckv_ref, wqn_ref, wqr_ref, wkv_ref, cos_ref, sin_ref,
                   q_ref, k_ref, vt_ref, *, blk):
    cq = cq_ref[...]
    nope = _dot(cq, wqn_ref[...])
    rope = _dot(cq, wqr_ref[...])
    kv = _dot(ckv_ref[...], wkv_ref[...])
    cos = cos_ref[...]
    sin = sin_ref[...]
    for h in range(N_HEADS):
        lo = h * QK_PAD_DIM
        pair = rope[:, (h // 2) * LANES:(h // 2 + 1) * LANES]
        if h % 2:
            pair = pltpu.roll(pair, ROPE_DIM, 1)
        q_ref[:, lo:lo + LANES] = (
            nope[:, h * QK_NOPE_DIM:(h + 1) * QK_NOPE_DIM] * Q_SCALE).astype(q_ref.dtype)
        q_ref[:, lo + LANES:lo + QK_PAD_DIM] = (
            _rope_low_half(pair, cos, sin) * Q_SCALE).astype(q_ref.dtype)
    ones = jnp.ones((V_AUG_DIM - V_HEAD_DIM, blk), vt_ref.dtype)
    for h in range(N_HEADS):
        lo = h * (QK_NOPE_DIM + V_HEAD_DIM)
        k_ref[:, h * QK_NOPE_DIM:(h + 1) * QK_NOPE_DIM] = (
            kv[:, lo:lo + QK_NOPE_DIM].astype(k_ref.dtype))
        v = kv[:, lo + QK_NOPE_DIM:lo + QK_NOPE_DIM + V_HEAD_DIM]
        for t in range(v.shape[0] // blk):
            vt_ref[t, h * V_AUG_DIM:h * V_AUG_DIM + V_HEAD_DIM, :] = (
                v[t * blk:(t + 1) * blk, :].T.astype(vt_ref.dtype))
            vt_ref[t, h * V_AUG_DIM + V_HEAD_DIM:(h + 1) * V_AUG_DIM, :] = ones


def _qkv_up(cq, ckv, w_qn, w_qr, w_kv, cos, sin, seq, tm, blk):
    m = cq.shape[0]
    tps = seq // tm
    per = tm // blk
    nq = N_HEADS * QK_PAD_DIM
    nk = N_HEADS * QK_NOPE_DIM
    nv = N_HEADS * V_AUG_DIM
    return pl.pallas_call(
        functools.partial(_qkv_up_kernel, blk=blk),
        grid=(m // tm,),
        in_specs=[pl.BlockSpec((tm, cq.shape[1]), lambda i: (i, 0)),
                  pl.BlockSpec((tm, ckv.shape[1]), lambda i: (i, 0)),
                  pl.BlockSpec(w_qn.shape, lambda i: (0, 0)),
                  pl.BlockSpec(w_qr.shape, lambda i: (0, 0)),
                  pl.BlockSpec(w_kv.shape, lambda i: (0, 0)),
                  pl.BlockSpec((tm, LANES), lambda i: (i % tps, 0)),
                  pl.BlockSpec((tm, LANES), lambda i: (i % tps, 0))],
        out_specs=[pl.BlockSpec((tm, nq), lambda i: (i, 0)),
                   pl.BlockSpec((tm, nk), lambda i: (i, 0)),
                   pl.BlockSpec((per, nv, blk), lambda i: (i, 0, 0))],
        out_shape=[jax.ShapeDtypeStruct((m, nq), BF16),
                   jax.ShapeDtypeStruct((m, nk), BF16),
                   jax.ShapeDtypeStruct((m // blk, nv, blk), BF16)],
        compiler_params=_params("parallel"),
        name="qkv_up",
    )(cq, ckv, w_qn, w_qr, w_kv, cos, sin)


def _attention_kernel(q_ref, kn_ref, kpe_ref, vt_ref, gate_ref, *rest, blk, n_casts):
    cast_in = rest[:n_casts]
    o_ref = rest[n_casts]
    cast_out = rest[n_casts + 1:2 * n_casts + 1]
    kfull_ref = rest[2 * n_casts + 1]
    cast_pieces = [(src, dst, r) for src, dst in zip(cast_in, cast_out)
                   for r in range(0, src.shape[0], CAST_PIECE_ROWS)]
    seq = q_ref.shape[0]
    nblk = seq // blk
    kfull_ref[:, :LANES] = kn_ref[...]
    kfull_ref[:, LANES:] = kpe_ref[...]

    key_chunk = lax.broadcasted_iota(jnp.int32, (blk, blk), 0) // CHUNK
    qry_chunk = lax.broadcasted_iota(jnp.int32, (blk, blk), 1) // CHUNK
    visible = key_chunk <= qry_chunk

    pairs = [(qi, kj) for qi in range(nblk) for kj in range(qi + 1)]
    half = blk // 2

    def scores(qi, kj):
        k0, q0 = kj * blk, qi * blk
        q_lo = q_ref[q0:q0 + half, :]
        q_hi = q_ref[q0 + half:q0 + blk, :]
        if kj != qi:
            kb = kfull_ref[k0:k0 + blk, :]
            return _dot_nt(kb, q_lo), _dot_nt(kb, q_hi)
        s_lo = _dot_nt(kfull_ref[k0:k0 + half, :], q_lo)
        s_hi = _dot_nt(kfull_ref[k0:k0 + blk, :], q_hi)
        return (jnp.where(visible[:half, :half], s_lo, -jnp.inf),
                jnp.where(visible[:, half:], s_hi, -jnp.inf))

    def softmax_step(state, s, vt):
        m_blk = jnp.max(s, axis=0, keepdims=True)
        if state is None:
            return m_blk, _dot(vt, jnp.exp2((s - m_blk).astype(BF16)))
        m_run, acc = state
        m_new = jnp.maximum(m_run, m_blk)
        alpha = jnp.exp2(m_run - m_new)
        return m_new, alpha * acc + _dot(vt, jnp.exp2((s - m_new).astype(BF16)))

    lo_state = hi_state = None
    ahead = [scores(*pairs[n]) for n in range(min(SCORE_LOOKAHEAD, len(pairs)))]
    for n, (qi, kj) in enumerate(pairs):
        s_lo, s_hi = ahead.pop(0)
        if n + SCORE_LOOKAHEAD < len(pairs):
            ahead.append(scores(*pairs[n + SCORE_LOOKAHEAD]))
        vt_lo = vt_ref[kj, :, :half] if kj == qi else vt_ref[kj]
        lo_state = softmax_step(lo_state, s_lo, vt_lo)
        hi_state = softmax_step(hi_state, s_hi, vt_ref[kj])
        if kj == qi:
            acc = jnp.concatenate([lo_state[1], hi_state[1]], axis=1)
            out = (acc[:V_HEAD_DIM] / acc[V_HEAD_DIM:V_HEAD_DIM + 1]).T
            gate = gate_ref[qi * blk:(qi + 1) * blk, :].astype(F32)
            o_ref[qi * blk:(qi + 1) * blk, :] = (out * gate).astype(o_ref.dtype)
            lo_state = hi_state = None
        lo = n * len(cast_pieces) // len(pairs)
        hi = (n + 1) * len(cast_pieces) // len(pairs)
        for src, dst, r in cast_pieces[lo:hi]:
            dst[r:r + CAST_PIECE_ROWS, :] = src[r:r + CAST_PIECE_ROWS, :].astype(dst.dtype)


def _attention(q, k_nope, k_pe, v_t, gate, batch, seq, blk, casts):
    m = q.shape[0]
    nblk = seq // blk
    steps = batch * N_HEADS
    cast_specs, cast_shapes = [], []
    for w, n_rows in casts:
        rows = n_rows // steps
        assert rows * steps == n_rows and rows % CAST_PIECE_ROWS == 0
        cast_specs.append(pl.BlockSpec((rows, w.shape[1]), lambda b, h: (b * N_HEADS + h, 0)))
        cast_shapes.append(jax.ShapeDtypeStruct((n_rows, w.shape[1]), BF16))
    return pl.pallas_call(
        functools.partial(_attention_kernel, blk=blk, n_casts=len(casts)),
        grid=(batch, N_HEADS),
        in_specs=[pl.BlockSpec((seq, QK_PAD_DIM), lambda b, h: (b, h)),
                  pl.BlockSpec((seq, LANES), lambda b, h: (b, h)),
                  pl.BlockSpec((seq, LANES), lambda b, h: (b, 0)),
                  pl.BlockSpec((nblk, V_AUG_DIM, blk), lambda b, h: (b, h, 0)),
                  pl.BlockSpec((seq, LANES), lambda b, h: (b, h))] + cast_specs,
        out_specs=[pl.BlockSpec((seq, V_HEAD_DIM), lambda b, h: (b, h))] + cast_specs,
        out_shape=[jax.ShapeDtypeStruct((m, N_HEADS * V_HEAD_DIM), BF16)] + cast_shapes,
        scratch_shapes=[pltpu.VMEM((seq, QK_PAD_DIM), BF16)],
        compiler_params=_params("parallel", "parallel"),
        name="attention",
    )(q, k_nope, k_pe, v_t, gate, *[w for w, _ in casts])


def _out_proj_kernel(yc_ref, ya_ref, w_ref, x_ref, g_ref, o_ref, *, final_norm, tn):
    kc = yc_ref.shape[1]
    d = o_ref.shape[1]
    yc = yc_ref[...]
    ya = ya_ref[...]
    ss = None
    for t in range(d // tn):
        cols = slice(t * tn, (t + 1) * tn)
        h = x_ref[:, cols] + (_dot(yc, w_ref[:kc, cols]) + _dot(ya, w_ref[kc:, cols]))
        o_ref[:, cols] = h.astype(o_ref.dtype)
        if final_norm:
            part = jnp.sum(h * h, axis=-1, keepdims=True)
            ss = part if ss is None else ss + part
    if final_norm:
        inv = lax.rsqrt(ss / d + NORM_EPS)
        for t in range(d // tn):
            cols = slice(t * tn, (t + 1) * tn)
            o_ref[:, cols] = (o_ref[:, cols] * inv) * g_ref[:, cols]


def _out_proj(y_conv, y_attn, w_o, x, g, final_norm, tm, tn):
    m, d = x.shape
    kc = y_conv.shape[1]
    ka = y_attn.shape[1]
    return pl.pallas_call(
        functools.partial(_out_proj_kernel, final_norm=final_norm, tn=tn),
        grid=(m // tm,),
        in_specs=[pl.BlockSpec((tm, kc), lambda i: (i, 0)),
                  pl.BlockSpec((tm, ka), lambda i: (i, 0)),
                  pl.BlockSpec((kc + ka, d), lambda i: (0, 0), pipeline_mode=pl.Buffered(1)),
                  pl.BlockSpec((tm, d), lambda i: (i, 0)),
                  pl.BlockSpec((1, d), lambda i: (0, 0))],
        out_specs=pl.BlockSpec((tm, d), lambda i: (i, 0)),
        out_shape=jax.ShapeDtypeStruct((m, d), x.dtype),
        compiler_params=pltpu.CompilerParams(dimension_semantics=("parallel",),
                                             vmem_limit_bytes=BIG_VMEM_LIMIT_BYTES),
        name="out_proj",
    )(y_conv, y_attn, w_o, x, g.reshape(1, d))


def _rope_tables(seq):
    pos = jnp.arange(seq, dtype=F32)
    inv_freq = 1.0 / (ROPE_THETA ** (jnp.arange(0, ROPE_DIM, 2, dtype=F32) / ROPE_DIM))
    ang = pos[:, None] * inv_freq[None, :]
    zeros = jnp.zeros((seq, LANES - ROPE_DIM), F32)
    cos = jnp.concatenate([jnp.cos(ang), jnp.cos(ang), zeros], axis=-1)
    sin = jnp.concatenate([jnp.sin(ang), jnp.sin(ang), zeros], axis=-1)
    return cos, sin


def _tile(n, pref):
    return pref if n % pref == 0 else n


def _layer(h, g_in, w_in, conv_w, q_norm_g, w_uq, kv_norm_g, w_ukv, w_out, g_out,
           final_norm, cos, sin, batch, seq):
    m, d = h.shape
    dc = conv_w.shape[1]
    o = 4 * dc

    w_in_t = w_in.T
    kr0 = o + Q_LORA + KV_LORA
    n_lat = Q_LORA + KV_LORA + LANES
    w_qn, w_qr, w_lat_t = _q_weights(w_uq, w_in_t, o, n_lat)
    w_kv = w_ukv.astype(BF16)

    blk = _tile(seq, ATTN_BLOCK)
    xn, cq, ckv, k_pe, w_z_t = _latents(h, g_in, w_lat_t, q_norm_g, kv_norm_g, cos, sin,
                                        w_in_t, kr0 + ROPE_DIM, w_in_t.shape[0] - kr0 - ROPE_DIM,
                                        seq, _tile(seq, 512))
    gate = _gate(xn, w_z_t, _tile(m, 1024), _tile(w_z_t.shape[0], 1024))
    q, k_nope, v_t = _qkv_up(cq, ckv, w_qn, w_qr, w_kv, cos, sin, seq, _tile(seq, 512), blk)
    y_attn, w_conv_t, w_o = _attention(q, k_nope, k_pe, v_t, gate, batch, seq, blk,
                                       [(w_in_t, o), (w_out, w_out.shape[0])])
    y_conv = _conv_group(xn, w_conv_t, conv_w, seq, _tile(seq, 1024), _tile(dc, 256))
    return _out_proj(y_conv, y_attn, w_o, h, g_out, final_norm, _tile(m, 256), _tile(d, 512))


def kernel(x, g_in, w_in, conv_w, q_norm_g, w_uq, kv_norm_g, w_ukv, w_out, g_final):
    batch, seq, d = x.shape
    depth = g_in.shape[0]
    cos, sin = _rope_tables(seq)
    h = x.reshape(batch * seq, d)
    for l in range(depth):
        h = _layer(h, g_in[l], w_in[l], conv_w[l], q_norm_g[l], w_uq[l], kv_norm_g[l],
                   w_ukv[l], w_out[l], g_final, l == depth - 1, cos, sin, batch, seq)
    return h.reshape(batch, seq, d)
```

```python
import functools

import jax
import jax.numpy as jnp
import numpy as np
from jax import lax
from jax.experimental import pallas as pl
from jax.experimental.pallas import tpu as pltpu

CHUNK = 64
CONV_WIDTH = 3
N_HEADS = 16
V_HEAD_DIM = 128
QK_NOPE_DIM = 128
ROPE_DIM = 64
ROPE_HALF = ROPE_DIM // 2
Q_LORA = 1024
KV_LORA = 512
ROPE_THETA = 10000.0
NORM_EPS = 1e-6
ATTN_SCALE = (QK_NOPE_DIM + ROPE_DIM) ** -0.5
LOG2_E = float(np.log2(np.e))
Q_SCALE = ATTN_SCALE * LOG2_E

LANES = 128
SUBLANES = 8
QK_PAD_DIM = 2 * LANES
V_AUG_DIM = V_HEAD_DIM + 16
ATTN_BLOCK = 512
SCORE_LOOKAHEAD = 2
ROW_PARTS = 2
CAST_PIECE_ROWS = 16
VMEM_LIMIT_BYTES = 56 * 1024 * 1024
BIG_VMEM_LIMIT_BYTES = 60 * 1024 * 1024

BF16 = jnp.bfloat16
F32 = jnp.float32


def _params(*semantics):
    return pltpu.CompilerParams(dimension_semantics=semantics,
                                vmem_limit_bytes=VMEM_LIMIT_BYTES)


def _dot(a, b):
    return jnp.dot(a, b, preferred_element_type=F32)


def _dot_nt(a, b):
    return lax.dot_general(a, b, (((1,), (1,)), ((), ())), preferred_element_type=F32)


def _rmsnorm_rows(x, g):
    ms = jnp.mean(x * x, axis=-1, keepdims=True)
    return (x * lax.rsqrt(ms + NORM_EPS)) * g


def _silu(z):
    return z * (0.5 * (jnp.tanh(0.5 * z) + 1.0))


def _conv_group_kernel(xn_ref, wb_ref, wc_ref, wh_ref, wz_ref, cw_ref, y_ref, carry_ref,
                       *, tiles_per_seq):
    i = pl.program_id(0)
    j = pl.program_id(1)
    tc = y_ref.shape[1]
    part = xn_ref.shape[0] // ROW_PARTS

    @pl.when(i % tiles_per_seq == 0)
    def _():
        carry_ref[j] = jnp.zeros((SUBLANES, tc), F32)

    prev = carry_ref[j]
    row = lax.broadcasted_iota(jnp.int32, (SUBLANES, tc), 0)
    cw = cw_ref[...]
    for p in range(ROW_PARTS):
        rows = slice(p * part, (p + 1) * part)
        xn = xn_ref[rows, :]
        u = _dot_nt(xn, wc_ref[...]) * _dot_nt(xn, wh_ref[...])

        def shifted(k):
            body = pltpu.roll(u, k, 0)
            head = jnp.where(row < k, pltpu.roll(prev, k, 0), body[:SUBLANES, :])
            return jnp.concatenate([head, body[SUBLANES:, :]], axis=0)

        conv = cw[0:1, :] * shifted(2) + cw[1:2, :] * shifted(1) + cw[2:3, :] * u
        y = (_dot_nt(xn, wb_ref[...]) * conv) * _silu(_dot_nt(xn, wz_ref[...]))
        y_ref[rows, :] = y.astype(y_ref.dtype)
        prev = u[part - SUBLANES:, :]
    carry_ref[j] = prev


def _conv_group(xn, w_t, conv_w, seq, tm, tc):
    m, d = xn.shape
    dc = conv_w.shape[1]
    nj = dc // tc

    def w_spec(k):
        return pl.BlockSpec((tc, d), lambda i, j: (j + k * nj, 0))

    return pl.pallas_call(
        functools.partial(_conv_group_kernel, tiles_per_seq=seq // tm),
        grid=(m // tm, nj),
        in_specs=[pl.BlockSpec((tm, d), lambda i, j: (i, 0)),
                  w_spec(0), w_spec(1), w_spec(2), w_spec(3),
                  pl.BlockSpec((CONV_WIDTH, tc), lambda i, j: (0, j))],
        out_specs=pl.BlockSpec((tm, tc), lambda i, j: (i, j)),
        out_shape=jax.ShapeDtypeStruct((m, dc), BF16),
        scratch_shapes=[pltpu.VMEM((nj, SUBLANES, tc), F32)],
        compiler_params=_params("arbitrary", "arbitrary"),
        name="conv_group",
    )(xn, w_t, w_t, w_t, w_t, conv_w)


def _rope_low_half(t, cos, sin):
    lane = lax.broadcasted_iota(jnp.int32, t.shape, 1)
    partner = jnp.where(lane < ROPE_HALF,
                        -pltpu.roll(t, LANES - ROPE_HALF, 1),
                        pltpu.roll(t, ROPE_HALF, 1))
    return jnp.where(lane < ROPE_DIM, t * cos + partner * sin, 0.0)


def _latent_kernel(x_ref, g_ref, w_ref, gq_ref, gkv_ref, cos_ref, sin_ref, cast_in_ref,
                   xn_ref, cq_ref, ckv_ref, kpe_ref, cast_out_ref):
    cast_out_ref[...] = cast_in_ref[...].astype(cast_out_ref.dtype)
    g = g_ref[...]
    part = x_ref.shape[0] // ROW_PARTS
    for p in range(ROW_PARTS):
        rows = slice(p * part, (p + 1) * part)
        c = _dot_nt((x_ref[rows, :] * g).astype(BF16), w_ref[...])
        x = x_ref[rows, :]
        inv = lax.rsqrt(jnp.mean(x * x, axis=-1, keepdims=True) + NORM_EPS)
        xn_ref[rows, :] = ((x * inv) * g).astype(xn_ref.dtype)
        c = c * inv
        cq_ref[rows, :] = _rmsnorm_rows(c[:, :Q_LORA], gq_ref[...]).astype(cq_ref.dtype)
        ckv_ref[rows, :] = _rmsnorm_rows(c[:, Q_LORA:Q_LORA + KV_LORA],
                                         gkv_ref[...]).astype(ckv_ref.dtype)
        kpe_ref[rows, :] = _rope_low_half(c[:, Q_LORA + KV_LORA:], cos_ref[rows, :],
                                          sin_ref[rows, :]).astype(kpe_ref.dtype)


def _latents(x, g_in, w_t, gq, gkv, cos, sin, w_cast, cast_row0, cast_rows, seq, tm):
    m, d = x.shape
    tps = seq // tm
    n = w_t.shape[0]
    assert n == Q_LORA + KV_LORA + LANES
    cr = cast_rows // (m // tm)
    assert cr * (m // tm) == cast_rows and cr % CAST_PIECE_ROWS == 0 and cast_row0 % SUBLANES == 0
    return pl.pallas_call(
        _latent_kernel,
        grid=(m // tm,),
        in_specs=[pl.BlockSpec((tm, d), lambda i: (i, 0)),
                  pl.BlockSpec((1, d), lambda i: (0, 0)),
                  pl.BlockSpec((n, d), lambda i: (0, 0), pipeline_mode=pl.Buffered(1)),
                  pl.BlockSpec((1, Q_LORA), lambda i: (0, 0)),
                  pl.BlockSpec((1, KV_LORA), lambda i: (0, 0)),
                  pl.BlockSpec((tm, LANES), lambda i: (i % tps, 0)),
                  pl.BlockSpec((tm, LANES), lambda i: (i % tps, 0)),
                  pl.BlockSpec((pl.Element(cr), pl.Element(w_cast.shape[1])),
                               lambda i: (pl.multiple_of(cast_row0 + i * cr, SUBLANES), 0))],
        out_specs=[pl.BlockSpec((tm, d), lambda i: (i, 0)),
                   pl.BlockSpec((tm, Q_LORA), lambda i: (i, 0)),
                   pl.BlockSpec((tm, KV_LORA), lambda i: (i, 0)),
                   pl.BlockSpec((tm, LANES), lambda i: (i, 0)),
                   pl.BlockSpec((cr, w_cast.shape[1]), lambda i: (i, 0))],
        out_shape=[jax.ShapeDtypeStruct((m, d), BF16),
                   jax.ShapeDtypeStruct((m, Q_LORA), BF16),
                   jax.ShapeDtypeStruct((m, KV_LORA), BF16),
                   jax.ShapeDtypeStruct((m, LANES), BF16),
                   jax.ShapeDtypeStruct((cast_rows, w_cast.shape[1]), BF16)],
        compiler_params=pltpu.CompilerParams(dimension_semantics=("parallel",),
                                             vmem_limit_bytes=BIG_VMEM_LIMIT_BYTES),
        name="latents",
    )(x, g_in.reshape(1, d), w_t, gq.reshape(1, -1), gkv.reshape(1, -1), cos, sin, w_cast)


def _gate_kernel(xn_ref, w_ref, o_ref):
    o_ref[...] = _silu(_dot_nt(xn_ref[...], w_ref[...])).astype(o_ref.dtype)


def _gate(xn, w_z_t, tm, tn):
    m, d = xn.shape
    n = w_z_t.shape[0]
    return pl.pallas_call(
        _gate_kernel,
        grid=(m // tm, n // tn),
        in_specs=[pl.BlockSpec((tm, d), lambda i, j: (i, 0)),
                  pl.BlockSpec((tn, d), lambda i, j: (j, 0))],
        out_specs=pl.BlockSpec((tm, tn), lambda i, j: (i, j)),
        out_shape=jax.ShapeDtypeStruct((m, n), BF16),
        compiler_params=_params("parallel", "parallel"),
        name="attn_gate",
    )(xn, w_z_t)


def _q_weight_kernel(w_ref, cast_in_ref, nope_ref, rope_ref, cast_out_ref):
    cast_out_ref[...] = cast_in_ref[...].astype(cast_out_ref.dtype)
    lane = lax.broadcasted_iota(jnp.int32, (w_ref.shape[0], LANES), 1)
    t0 = w_ref[:, :LANES]
    t1 = w_ref[:, LANES:2 * LANES]
    t2 = w_ref[:, 2 * LANES:]
    nope_ref[:, :LANES] = t0.astype(nope_ref.dtype)
    nope_ref[:, LANES:] = jnp.where(lane < ROPE_DIM, pltpu.roll(t1, ROPE_DIM, 1),
                                    pltpu.roll(t2, ROPE_DIM, 1)).astype(nope_ref.dtype)
    rope_ref[...] = jnp.where(lane < ROPE_DIM, t1, t2).astype(rope_ref.dtype)


def _q_weights(w_uq, w_cast, cast_row0, cast_rows):
    k = w_uq.shape[0]
    pair_in = 2 * (QK_NOPE_DIM + ROPE_DIM)
    steps = N_HEADS // 2
    cr = cast_rows // steps
    assert cr * steps == cast_rows and cr % CAST_PIECE_ROWS == 0 and cast_row0 % SUBLANES == 0
    return pl.pallas_call(
        _q_weight_kernel,
        grid=(steps,),
        in_specs=[pl.BlockSpec((k, pair_in), lambda j: (0, j)),
                  pl.BlockSpec((pl.Element(cr), pl.Element(w_cast.shape[1])),
                               lambda j: (pl.multiple_of(cast_row0 + j * cr, SUBLANES), 0))],
        out_specs=[pl.BlockSpec((k, 2 * QK_NOPE_DIM), lambda j: (0, j)),
                   pl.BlockSpec((k, 2 * ROPE_DIM), lambda j: (0, j)),
                   pl.BlockSpec((cr, w_cast.shape[1]), lambda j: (j, 0))],
        out_shape=[jax.ShapeDtypeStruct((k, N_HEADS * QK_NOPE_DIM), BF16),
                   jax.ShapeDtypeStruct((k, N_HEADS * ROPE_DIM), BF16),
                   jax.ShapeDtypeStruct((cast_rows, w_cast.shape[1]), BF16)],
        compiler_params=_params("parallel"),
        name="q_weights",
    )(w_uq, w_cast)


def _qkv_up_kernel(cq_ref, ckv_ref, wqn_ref, wqr_ref, wkv_ref, cos_ref, sin_ref,
                   q_ref, k_ref, vt_ref, *, blk):
    cq = cq_ref[...]
    nope = _dot(cq, wqn_ref[...])
    rope = _dot(cq, wqr_ref[...])
    kv = _dot(ckv_ref[...], wkv_ref[...])
    cos = cos_ref[...]
    sin = sin_ref[...]
    for h in range(N_HEADS):
        lo = h * QK_PAD_DIM
        pair = rope[:, (h // 2) * LANES:(h // 2 + 1) * LANES]
        if h % 2:
            pair = pltpu.roll(pair, ROPE_DIM, 1)
        q_ref[:, lo:lo + LANES] = (
            nope[:, h * QK_NOPE_DIM:(h + 1) * QK_NOPE_DIM] * Q_SCALE).astype(q_ref.dtype)
        q_ref[:, lo + LANES:lo + QK_PAD_DIM] = (
            _rope_low_half(pair, cos, sin) * Q_SCALE).astype(q_ref.dtype)
    ones = jnp.ones((V_AUG_DIM - V_HEAD_DIM, blk), vt_ref.dtype)
    for h in range(N_HEADS):
        lo = h * (QK_NOPE_DIM + V_HEAD_DIM)
        k_ref[:, h * QK_NOPE_DIM:(h + 1) * QK_NOPE_DIM] = (
            kv[:, lo:lo + QK_NOPE_DIM].astype(k_ref.dtype))
        v = kv[:, lo + QK_NOPE_DIM:lo + QK_NOPE_DIM + V_HEAD_DIM]
        for t in range(v.shape[0] // blk):
            vt_ref[t, h * V_AUG_DIM:h * V_AUG_DIM + V_HEAD_DIM, :] = (
                v[t * blk:(t + 1) * blk, :].T.astype(vt_ref.dtype))
            vt_ref[t, h * V_AUG_DIM + V_HEAD_DIM:(h + 1) * V_AUG_DIM, :] = ones


def _qkv_up(cq, ckv, w_qn, w_qr, w_kv, cos, sin, seq, tm, blk):
    m = cq.shape[0]
    tps = seq // tm
    per = tm // blk
    nq = N_HEADS * QK_PAD_DIM
    nk = N_HEADS * QK_NOPE_DIM
    nv = N_HEADS * V_AUG_DIM
    return pl.pallas_call(
        functools.partial(_qkv_up_kernel, blk=blk),
        grid=(m // tm,),
        in_specs=[pl.BlockSpec((tm, cq.shape[1]), lambda i: (i, 0)),
                  pl.BlockSpec((tm, ckv.shape[1]), lambda i: (i, 0)),
                  pl.BlockSpec(w_qn.shape, lambda i: (0, 0)),
                  pl.BlockSpec(w_qr.shape, lambda i: (0, 0)),
                  pl.BlockSpec(w_kv.shape, lambda i: (0, 0)),
                  pl.BlockSpec((tm, LANES), lambda i: (i % tps, 0)),
                  pl.BlockSpec((tm, LANES), lambda i: (i % tps, 0))],
        out_specs=[pl.BlockSpec((tm, nq), lambda i: (i, 0)),
                   pl.BlockSpec((tm, nk), lambda i: (i, 0)),
                   pl.BlockSpec((per, nv, blk), lambda i: (i, 0, 0))],
        out_shape=[jax.ShapeDtypeStruct((m, nq), BF16),
                   jax.ShapeDtypeStruct((m, nk), BF16),
                   jax.ShapeDtypeStruct((m // blk, nv, blk), BF16)],
        compiler_params=_params("parallel"),
        name="qkv_up",
    )(cq, ckv, w_qn, w_qr, w_kv, cos, sin)


def _attention_kernel(q_ref, kn_ref, kpe_ref, vt_ref, gate_ref, *rest, blk, n_casts):
    cast_in = rest[:n_casts]
    o_ref = rest[n_casts]
    cast_out = rest[n_casts + 1:2 * n_casts + 1]
    kfull_ref = rest[2 * n_casts + 1]
    cast_pieces = [(src, dst, r) for src, dst in zip(cast_in, cast_out)
                   for r in range(0, src.shape[0], CAST_PIECE_ROWS)]
    seq = q_ref.shape[0]
    nblk = seq // blk
    kfull_ref[:, :LANES] = kn_ref[...]
    kfull_ref[:, LANES:] = kpe_ref[...]

    key_chunk = lax.broadcasted_iota(jnp.int32, (blk, blk), 0) // CHUNK
    qry_chunk = lax.broadcasted_iota(jnp.int32, (blk, blk), 1) // CHUNK
    visible = key_chunk <= qry_chunk

    pairs = [(qi, kj) for qi in range(nblk) for kj in range(qi + 1)]
    half = blk // 2

    def scores(qi, kj):
        k0, q0 = kj * blk, qi * blk
        q_lo = q_ref[q0:q0 + half, :]
        q_hi = q_ref[q0 + half:q0 + blk, :]
        if kj != qi:
            kb = kfull_ref[k0:k0 + blk, :]
            return _dot_nt(kb, q_lo), _dot_nt(kb, q_hi)
        s_lo = _dot_nt(kfull_ref[k0:k0 + half, :], q_lo)
        s_hi = _dot_nt(kfull_ref[k0:k0 + blk, :], q_hi)
        return (jnp.where(visible[:half, :half], s_lo, -jnp.inf),
                jnp.where(visible[:, half:], s_hi, -jnp.inf))

    def softmax_step(state, s, vt):
        m_blk = jnp.max(s, axis=0, keepdims=True)
        if state is None:
            return m_blk, _dot(vt, jnp.exp2((s - m_blk).astype(BF16)))
        m_run, acc = state
        m_new = jnp.maximum(m_run, m_blk)
        alpha = jnp.exp2(m_run - m_new)
        return m_new, alpha * acc + _dot(vt, jnp.exp2((s - m_new).astype(BF16)))

    lo_state = hi_state = None
    ahead = [scores(*pairs[n]) for n in range(min(SCORE_LOOKAHEAD, len(pairs)))]
    for n, (qi, kj) in enumerate(pairs):
        s_lo, s_hi = ahead.pop(0)
        if n + SCORE_LOOKAHEAD < len(pairs):
            ahead.append(scores(*pairs[n + SCORE_LOOKAHEAD]))
        vt_lo = vt_ref[kj, :, :half] if kj == qi else vt_ref[kj]
        lo_state = softmax_step(lo_state, s_lo, vt_lo)
        hi_state = softmax_step(hi_state, s_hi, vt_ref[kj])
        if kj == qi:
            acc = jnp.concatenate([lo_state[1], hi_state[1]], axis=1)
            out = (acc[:V_HEAD_DIM] / acc[V_HEAD_DIM:V_HEAD_DIM + 1]).T
            gate = gate_ref[qi * blk:(qi + 1) * blk, :].astype(F32)
            o_ref[qi * blk:(qi + 1) * blk, :] = (out * gate).astype(o_ref.dtype)
            lo_state = hi_state = None
        lo = n * len(cast_pieces) // len(pairs)
        hi = (n + 1) * len(cast_pieces) // len(pairs)
        for src, dst, r in cast_pieces[lo:hi]:
            dst[r:r + CAST_PIECE_ROWS, :] = src[r:r + CAST_PIECE_ROWS, :].astype(dst.dtype)


def _attention(q, k_nope, k_pe, v_t, gate, batch, seq, blk, casts):
    m = q.shape[0]
    nblk = seq // blk
    steps = batch * N_HEADS
    cast_specs, cast_shapes = [], []
    for w, n_rows in casts:
        rows = n_rows // steps
        assert rows * steps == n_rows and rows % CAST_PIECE_ROWS == 0
        cast_specs.append(pl.BlockSpec((rows, w.shape[1]), lambda b, h: (b * N_HEADS + h, 0)))
        cast_shapes.append(jax.ShapeDtypeStruct((n_rows, w.shape[1]), BF16))
    return pl.pallas_call(
        functools.partial(_attention_kernel, blk=blk, n_casts=len(casts)),
        grid=(batch, N_HEADS),
        in_specs=[pl.BlockSpec((seq, QK_PAD_DIM), lambda b, h: (b, h)),
                  pl.BlockSpec((seq, LANES), lambda b, h: (b, h)),
                  pl.BlockSpec((seq, LANES), lambda b, h: (b, 0)),
                  pl.BlockSpec((nblk, V_AUG_DIM, blk), lambda b, h: (b, h, 0)),
                  pl.BlockSpec((seq, LANES), lambda b, h: (b, h))] + cast_specs,
        out_specs=[pl.BlockSpec((seq, V_HEAD_DIM), lambda b, h: (b, h))] + cast_specs,
        out_shape=[jax.ShapeDtypeStruct((m, N_HEADS * V_HEAD_DIM), BF16)] + cast_shapes,
        scratch_shapes=[pltpu.VMEM((seq, QK_PAD_DIM), BF16)],
        compiler_params=_params("parallel", "parallel"),
        name="attention",
    )(q, k_nope, k_pe, v_t, gate, *[w for w, _ in casts])


def _out_proj_kernel(yc_ref, ya_ref, w_ref, x_ref, g_ref, o_ref, *, final_norm, tn):
    kc = yc_ref.shape[1]
    d = o_ref.shape[1]
    yc = yc_ref[...]
    ya = ya_ref[...]
    ss = None
    for t in range(d // tn):
        cols = slice(t * tn, (t + 1) * tn)
        h = x_ref[:, cols] + (_dot(yc, w_ref[:kc, cols]) + _dot(ya, w_ref[kc:, cols]))
        o_ref[:, cols] = h.astype(o_ref.dtype)
        if final_norm:
            part = jnp.sum(h * h, axis=-1, keepdims=True)
            ss = part if ss is None else ss + part
    if final_norm:
        inv = lax.rsqrt(ss / d + NORM_EPS)
        for t in range(d // tn):
            cols = slice(t * tn, (t + 1) * tn)
            o_ref[:, cols] = (o_ref[:, cols] * inv) * g_ref[:, cols]


def _out_proj(y_conv, y_attn, w_o, x, g, final_norm, tm, tn):
    m, d = x.shape
    kc = y_conv.shape[1]
    ka = y_attn.shape[1]
    return pl.pallas_call(
        functools.partial(_out_proj_kernel, final_norm=final_norm, tn=tn),
        grid=(m // tm,),
        in_specs=[pl.BlockSpec((tm, kc), lambda i: (i, 0)),
                  pl.BlockSpec((tm, ka), lambda i: (i, 0)),
                  pl.BlockSpec((kc + ka, d), lambda i: (0, 0), pipeline_mode=pl.Buffered(1)),
                  pl.BlockSpec((tm, d), lambda i: (i, 0)),
                  pl.BlockSpec((1, d), lambda i: (0, 0))],
        out_specs=pl.BlockSpec((tm, d), lambda i: (i, 0)),
        out_shape=jax.ShapeDtypeStruct((m, d), x.dtype),
        compiler_params=pltpu.CompilerParams(dimension_semantics=("parallel",),
                                             vmem_limit_bytes=BIG_VMEM_LIMIT_BYTES),
        name="out_proj",
    )(y_conv, y_attn, w_o, x, g.reshape(1, d))


def _rope_tables(seq):
    pos = jnp.arange(seq, dtype=F32)
    inv_freq = 1.0 / (ROPE_THETA ** (jnp.arange(0, ROPE_DIM, 2, dtype=F32) / ROPE_DIM))
    ang = pos[:, None] * inv_freq[None, :]
    zeros = jnp.zeros((seq, LANES - ROPE_DIM), F32)
    cos = jnp.concatenate([jnp.cos(ang), jnp.cos(ang), zeros], axis=-1)
    sin = jnp.concatenate([jnp.sin(ang), jnp.sin(ang), zeros], axis=-1)
    return cos, sin


def _tile(n, pref):
    return pref if n % pref == 0 else n


def _layer(h, g_in, w_in, conv_w, q_norm_g, w_uq, kv_norm_g, w_ukv, w_out, g_out,
           final_norm, cos, sin, batch, seq):
    m, d = h.shape
    dc = conv_w.shape[1]
    o = 4 * dc

    w_in_t = w_in.T
    kr0 = o + Q_LORA + KV_LORA
    n_lat = Q_LORA + KV_LORA + LANES
    w_qn, w_qr, w_lat_t = _q_weights(w_uq, w_in_t, o, n_lat)
    w_kv = w_ukv.astype(BF16)

    blk = _tile(seq, ATTN_BLOCK)
    xn, cq, ckv, k_pe, w_z_t = _latents(h, g_in, w_lat_t, q_norm_g, kv_norm_g, cos, sin,
                                        w_in_t, kr0 + ROPE_DIM, w_in_t.shape[0] - kr0 - ROPE_DIM,
                                        seq, _tile(seq, 512))
    gate = _gate(xn, w_z_t, _tile(m, 1024), _tile(w_z_t.shape[0], 1024))
    q, k_nope, v_t = _qkv_up(cq, ckv, w_qn, w_qr, w_kv, cos, sin, seq, _tile(seq, 512), blk)
    y_attn, w_conv_t, w_o = _attention(q, k_nope, k_pe, v_t, gate, batch, seq, blk,
                                       [(w_in_t, o), (w_out, w_out.shape[0])])
    y_conv = _conv_group(xn, w_conv_t, conv_w, seq, _tile(seq, 1024), _tile(dc, 256))
    return _out_proj(y_conv, y_attn, w_o, h, g_out, final_norm, _tile(m, 256), _tile(d, 512))


def kernel(x, g_in, w_in, conv_w, q_norm_g, w_uq, kv_norm_g, w_ukv, w_out, g_final):
    batch, seq, d = x.shape
    depth = g_in.shape[0]
    cos, sin = _rope_tables(seq)
    h = x.reshape(batch * seq, d)
    for l in range(depth):
        h = _layer(h, g_in[l], w_in[l], conv_w[l], q_norm_g[l], w_uq[l], kv_norm_g[l],
                   w_ukv[l], w_out[l], g_final, l == depth - 1, cos, sin, batch, seq)
    return h.reshape(batch, seq, d)
```

```python
import functools

import jax
import jax.numpy as jnp
import numpy as np
from jax import lax
from jax.experimental import pallas as pl
from jax.experimental.pallas import tpu as pltpu

CHUNK = 64
CONV_WIDTH = 3
N_HEADS = 16
V_HEAD_DIM = 128
QK_NOPE_DIM = 128
ROPE_DIM = 64
ROPE_HALF = ROPE_DIM // 2
Q_LORA = 1024
KV_LORA = 512
ROPE_THETA = 10000.0
NORM_EPS = 1e-6
ATTN_SCALE = (QK_NOPE_DIM + ROPE_DIM) ** -0.5
LOG2_E = float(np.log2(np.e))
Q_SCALE = ATTN_SCALE * LOG2_E

LANES = 128
SUBLANES = 8
QK_PAD_DIM = 2 * LANES
V_AUG_DIM = V_HEAD_DIM + 16
ATTN_BLOCK = 512
SCORE_LOOKAHEAD = 2
CONV_PIECE_ROWS = 512
ROW_PARTS = 2
CAST_PIECE_ROWS = 16
VMEM_LIMIT_BYTES = 56 * 1024 * 1024
BIG_VMEM_LIMIT_BYTES = 60 * 1024 * 1024

BF16 = jnp.bfloat16
F32 = jnp.float32


def _params(*semantics):
    return pltpu.CompilerParams(dimension_semantics=semantics,
                                vmem_limit_bytes=VMEM_LIMIT_BYTES)


def _dot(a, b):
    return jnp.dot(a, b, preferred_element_type=F32)


def _dot_nt(a, b):
    return lax.dot_general(a, b, (((1,), (1,)), ((), ())), preferred_element_type=F32)


def _rmsnorm_rows(x, g):
    ms = jnp.mean(x * x, axis=-1, keepdims=True)
    return (x * lax.rsqrt(ms + NORM_EPS)) * g


def _silu(z):
    return z * (0.5 * (jnp.tanh(0.5 * z) + 1.0))


def _conv_group_kernel(xn_ref, wb_ref, wc_ref, wh_ref, wz_ref, cw_ref, y_ref, carry_ref,
                       *, tiles_per_seq):
    i = pl.program_id(0)
    j = pl.program_id(1)
    tc = y_ref.shape[1]
    part = CONV_PIECE_ROWS
    n_parts = xn_ref.shape[0] // part

    @pl.when(i % tiles_per_seq == 0)
    def _():
        carry_ref[j] = jnp.zeros((SUBLANES, tc), F32)

    prev = carry_ref[j]
    row = lax.broadcasted_iota(jnp.int32, (SUBLANES, tc), 0)
    cw = cw_ref[...]
    for p in range(n_parts):
        rows = slice(p * part, (p + 1) * part)
        xn = xn_ref[rows, :]
        u = _dot_nt(xn, wc_ref[...]) * _dot_nt(xn, wh_ref[...])

        def shifted(k):
            body = pltpu.roll(u, k, 0)
            head = jnp.where(row < k, pltpu.roll(prev, k, 0), body[:SUBLANES, :])
            return jnp.concatenate([head, body[SUBLANES:, :]], axis=0)

        conv = cw[0:1, :] * shifted(2) + cw[1:2, :] * shifted(1) + cw[2:3, :] * u
        y = (_dot_nt(xn, wb_ref[...]) * conv) * _silu(_dot_nt(xn, wz_ref[...]))
        y_ref[rows, :] = y.astype(y_ref.dtype)
        prev = u[part - SUBLANES:, :]
    carry_ref[j] = prev


def _conv_group(xn, w_t, conv_w, seq, tm, tc):
    m, d = xn.shape
    dc = conv_w.shape[1]
    nj = dc // tc

    def w_spec(k):
        return pl.BlockSpec((tc, d), lambda i, j: (j + k * nj, 0))

    return pl.pallas_call(
        functools.partial(_conv_group_kernel, tiles_per_seq=seq // tm),
        grid=(m // tm, nj),
        in_specs=[pl.BlockSpec((tm, d), lambda i, j: (i, 0)),
                  w_spec(0), w_spec(1), w_spec(2), w_spec(3),
                  pl.BlockSpec((CONV_WIDTH, tc), lambda i, j: (0, j))],
        out_specs=pl.BlockSpec((tm, tc), lambda i, j: (i, j)),
        out_shape=jax.ShapeDtypeStruct((m, dc), BF16),
        scratch_shapes=[pltpu.VMEM((nj, SUBLANES, tc), F32)],
        compiler_params=pltpu.CompilerParams(dimension_semantics=("arbitrary", "arbitrary"),
                                             vmem_limit_bytes=BIG_VMEM_LIMIT_BYTES),
        name="conv_group",
    )(xn, w_t, w_t, w_t, w_t, conv_w)


def _rope_low_half(t, cos, sin):
    lane = lax.broadcasted_iota(jnp.int32, t.shape, 1)
    partner = jnp.where(lane < ROPE_HALF,
                        -pltpu.roll(t, LANES - ROPE_HALF, 1),
                        pltpu.roll(t, ROPE_HALF, 1))
    return jnp.where(lane < ROPE_DIM, t * cos + partner * sin, 0.0)


def _latent_kernel(x_ref, g_ref, w_ref, gq_ref, gkv_ref, cos_ref, sin_ref, cast_in_ref,
                   xn_ref, cq_ref, ckv_ref, kpe_ref, cast_out_ref):
    cast_out_ref[...] = cast_in_ref[...].astype(cast_out_ref.dtype)
    g = g_ref[...]
    part = x_ref.shape[0] // ROW_PARTS
    for p in range(ROW_PARTS):
        rows = slice(p * part, (p + 1) * part)
        c = _dot_nt((x_ref[rows, :] * g).astype(BF16), w_ref[...])
        x = x_ref[rows, :]
        inv = lax.rsqrt(jnp.mean(x * x, axis=-1, keepdims=True) + NORM_EPS)
        xn_ref[rows, :] = ((x * inv) * g).astype(xn_ref.dtype)
        c = c * inv
        cq_ref[rows, :] = _rmsnorm_rows(c[:, :Q_LORA], gq_ref[...]).astype(cq_ref.dtype)
        ckv_ref[rows, :] = _rmsnorm_rows(c[:, Q_LORA:Q_LORA + KV_LORA],
                                         gkv_ref[...]).astype(ckv_ref.dtype)
        kpe_ref[rows, :] = _rope_low_half(c[:, Q_LORA + KV_LORA:], cos_ref[rows, :],
                                          sin_ref[rows, :]).astype(kpe_ref.dtype)


def _latents(x, g_in, w_t, gq, gkv, cos, sin, w_cast, cast_row0, cast_rows, seq, tm):
    m, d = x.shape
    tps = seq // tm
    n = w_t.shape[0]
    assert n == Q_LORA + KV_LORA + LANES
    cr = cast_rows // (m // tm)
    assert cr * (m // tm) == cast_rows and cr % CAST_PIECE_ROWS == 0 and cast_row0 % SUBLANES == 0
    return pl.pallas_call(
        _latent_kernel,
        grid=(m // tm,),
        in_specs=[pl.BlockSpec((tm, d), lambda i: (i, 0)),
                  pl.BlockSpec((1, d), lambda i: (0, 0)),
                  pl.BlockSpec((n, d), lambda i: (0, 0), pipeline_mode=pl.Buffered(1)),
                  pl.BlockSpec((1, Q_LORA), lambda i: (0, 0)),
                  pl.BlockSpec((1, KV_LORA), lambda i: (0, 0)),
                  pl.BlockSpec((tm, LANES), lambda i: (i % tps, 0)),
                  pl.BlockSpec((tm, LANES), lambda i: (i % tps, 0)),
                  pl.BlockSpec((pl.Element(cr), pl.Element(w_cast.shape[1])),
                               lambda i: (pl.multiple_of(cast_row0 + i * cr, SUBLANES), 0))],
        out_specs=[pl.BlockSpec((tm, d), lambda i: (i, 0)),
                   pl.BlockSpec((tm, Q_LORA), lambda i: (i, 0)),
                   pl.BlockSpec((tm, KV_LORA), lambda i: (i, 0)),
                   pl.BlockSpec((tm, LANES), lambda i: (i, 0)),
                   pl.BlockSpec((cr, w_cast.shape[1]), lambda i: (i, 0))],
        out_shape=[jax.ShapeDtypeStruct((m, d), BF16),
                   jax.ShapeDtypeStruct((m, Q_LORA), BF16),
                   jax.ShapeDtypeStruct((m, KV_LORA), BF16),
                   jax.ShapeDtypeStruct((m, LANES), BF16),
                   jax.ShapeDtypeStruct((cast_rows, w_cast.shape[1]), BF16)],
        compiler_params=pltpu.CompilerParams(dimension_semantics=("parallel",),
                                             vmem_limit_bytes=BIG_VMEM_LIMIT_BYTES),
        name="latents",
    )(x, g_in.reshape(1, d), w_t, gq.reshape(1, -1), gkv.reshape(1, -1), cos, sin, w_cast)


def _gate_kernel(xn_ref, w_ref, o_ref):
    o_ref[...] = _silu(_dot_nt(xn_ref[...], w_ref[...])).astype(o_ref.dtype)


def _gate(xn, w_z_t, tm, tn):
    m, d = xn.shape
    n = w_z_t.shape[0]
    return pl.pallas_call(
        _gate_kernel,
        grid=(m // tm, n // tn),
        in_specs=[pl.BlockSpec((tm, d), lambda i, j: (i, 0)),
                  pl.BlockSpec((tn, d), lambda i, j: (j, 0))],
        out_specs=pl.BlockSpec((tm, tn), lambda i, j: (i, j)),
        out_shape=jax.ShapeDtypeStruct((m, n), BF16),
        compiler_params=_params("parallel", "parallel"),
        name="attn_gate",
    )(xn, w_z_t)


def _q_weight_kernel(w_ref, cast_in_ref, nope_ref, rope_ref, cast_out_ref):
    cast_out_ref[...] = cast_in_ref[...].astype(cast_out_ref.dtype)
    lane = lax.broadcasted_iota(jnp.int32, (w_ref.shape[0], LANES), 1)
    t0 = w_ref[:, :LANES]
    t1 = w_ref[:, LANES:2 * LANES]
    t2 = w_ref[:, 2 * LANES:]
    nope_ref[:, :LANES] = t0.astype(nope_ref.dtype)
    nope_ref[:, LANES:] = jnp.where(lane < ROPE_DIM, pltpu.roll(t1, ROPE_DIM, 1),
                                    pltpu.roll(t2, ROPE_DIM, 1)).astype(nope_ref.dtype)
    rope_ref[...] = jnp.where(lane < ROPE_DIM, t1, t2).astype(rope_ref.dtype)


def _q_weights(w_uq, w_cast, cast_row0, cast_rows):
    k = w_uq.shape[0]
    pair_in = 2 * (QK_NOPE_DIM + ROPE_DIM)
    steps = N_HEADS // 2
    cr = cast_rows // steps
    assert cr * steps == cast_rows and cr % CAST_PIECE_ROWS == 0 and cast_row0 % SUBLANES == 0
    return pl.pallas_call(
        _q_weight_kernel,
        grid=(steps,),
        in_specs=[pl.BlockSpec((k, pair_in), lambda j: (0, j)),
                  pl.BlockSpec((pl.Element(cr), pl.Element(w_cast.shape[1])),
                               lambda j: (pl.multiple_of(cast_row0 + j * cr, SUBLANES), 0))],
        out_specs=[pl.BlockSpec((k, 2 * QK_NOPE_DIM), lambda j: (0, j)),
                   pl.BlockSpec((k, 2 * ROPE_DIM), lambda j: (0, j)),
                   pl.BlockSpec((cr, w_cast.shape[1]), lambda j: (j, 0))],
        out_shape=[jax.ShapeDtypeStruct((k, N_HEADS * QK_NOPE_DIM), BF16),
                   jax.ShapeDtypeStruct((k, N_HEADS * ROPE_DIM), BF16),
                   jax.ShapeDtypeStruct((cast_rows, w_cast.shape[1]), BF16)],
        compiler_params=_params("parallel"),
        name="q_weights",
    )(w_uq, w_cast)


def _qkv_up_kernel(cq_ref, ckv_ref, wqn_ref, wqr_ref, wkv_ref, cos_ref, sin_ref,
                   q_ref, k_ref, vt_ref, *, blk):
    cq = cq_ref[...]
    nope = _dot(cq, wqn_ref[...])
    rope = _dot(cq, wqr_ref[...])
    kv = _dot(ckv_ref[...], wkv_ref[...])
    cos = cos_ref[...]
    sin = sin_ref[...]
    for h in range(N_HEADS):
        lo = h * QK_PAD_DIM
        pair = rope[:, (h // 2) * LANES:(h // 2 + 1) * LANES]
        if h % 2:
            pair = pltpu.roll(pair, ROPE_DIM, 1)
        q_ref[:, lo:lo + LANES] = (
            nope[:, h * QK_NOPE_DIM:(h + 1) * QK_NOPE_DIM] * Q_SCALE).astype(q_ref.dtype)
        q_ref[:, lo + LANES:lo + QK_PAD_DIM] = (
            _rope_low_half(pair, cos, sin) * Q_SCALE).astype(q_ref.dtype)
    ones = jnp.ones((V_AUG_DIM - V_HEAD_DIM, blk), vt_ref.dtype)
    for h in range(N_HEADS):
        lo = h * (QK_NOPE_DIM + V_HEAD_DIM)
        k_ref[:, h * QK_NOPE_DIM:(h + 1) * QK_NOPE_DIM] = (
            kv[:, lo:lo + QK_NOPE_DIM].astype(k_ref.dtype))
        v = kv[:, lo + QK_NOPE_DIM:lo + QK_NOPE_DIM + V_HEAD_DIM]
        for t in range(v.shape[0] // blk):
            vt_ref[t, h * V_AUG_DIM:h * V_AUG_DIM + V_HEAD_DIM, :] = (
                v[t * blk:(t + 1) * blk, :].T.astype(vt_ref.dtype))
            vt_ref[t, h * V_AUG_DIM + V_HEAD_DIM:(h + 1) * V_AUG_DIM, :] = ones


def _qkv_up(cq, ckv, w_qn, w_qr, w_kv, cos, sin, seq, tm, blk):
    m = cq.shape[0]
    tps = seq // tm
    per = tm // blk
    nq = N_HEADS * QK_PAD_DIM
    nk = N_HEADS * QK_NOPE_DIM
    nv = N_HEADS * V_AUG_DIM
    return pl.pallas_call(
        functools.partial(_qkv_up_kernel, blk=blk),
        grid=(m // tm,),
        in_specs=[pl.BlockSpec((tm, cq.shape[1]), lambda i: (i, 0)),
                  pl.BlockSpec((tm, ckv.shape[1]), lambda i: (i, 0)),
                  pl.BlockSpec(w_qn.shape, lambda i: (0, 0)),
                  pl.BlockSpec(w_qr.shape, lambda i: (0, 0)),
                  pl.BlockSpec(w_kv.shape, lambda i: (0, 0)),
                  pl.BlockSpec((tm, LANES), lambda i: (i % tps, 0)),
                  pl.BlockSpec((tm, LANES), lambda i: (i % tps, 0))],
        out_specs=[pl.BlockSpec((tm, nq), lambda i: (i, 0)),
                   pl.BlockSpec((tm, nk), lambda i: (i, 0)),
                   pl.BlockSpec((per, nv, blk), lambda i: (i, 0, 0))],
        out_shape=[jax.ShapeDtypeStruct((m, nq), BF16),
                   jax.ShapeDtypeStruct((m, nk), BF16),
                   jax.ShapeDtypeStruct((m // blk, nv, blk), BF16)],
        compiler_params=_params("parallel"),
        name="qkv_up",
    )(cq, ckv, w_qn, w_qr, w_kv, cos, sin)


def _attention_kernel(q_ref, kn_ref, kpe_ref, vt_ref, gate_ref, *rest, blk, n_casts):
    cast_in = rest[:n_casts]
    o_ref = rest[n_casts]
    cast_out = rest[n_casts + 1:2 * n_casts + 1]
    kfull_ref = rest[2 * n_casts + 1]
    cast_pieces = [(src, dst, r) for src, dst in zip(cast_in, cast_out)
                   for r in range(0, src.shape[0], CAST_PIECE_ROWS)]
    seq = q_ref.shape[0]
    nblk = seq // blk
    kfull_ref[:, :LANES] = kn_ref[...]
    kfull_ref[:, LANES:] = kpe_ref[...]

    key_chunk = lax.broadcasted_iota(jnp.int32, (blk, blk), 0) // CHUNK
    qry_chunk = lax.broadcasted_iota(jnp.int32, (blk, blk), 1) // CHUNK
    visible = key_chunk <= qry_chunk

    pairs = [(qi, kj) for qi in range(nblk) for kj in range(qi + 1)]
    half = blk // 2

    def scores(qi, kj):
        k0, q0 = kj * blk, qi * blk
        q_lo = q_ref[q0:q0 + half, :]
        q_hi = q_ref[q0 + half:q0 + blk, :]
        if kj != qi:
            kb = kfull_ref[k0:k0 + blk, :]
            return _dot_nt(kb, q_lo), _dot_nt(kb, q_hi)
        s_lo = _dot_nt(kfull_ref[k0:k0 + half, :], q_lo)
        s_hi = _dot_nt(kfull_ref[k0:k0 + blk, :], q_hi)
        return (jnp.where(visible[:half, :half], s_lo, -jnp.inf),
                jnp.where(visible[:, half:], s_hi, -jnp.inf))

    def softmax_step(state, s, vt):
        m_blk = jnp.max(s, axis=0, keepdims=True)
        if state is None:
            return m_blk, _dot(vt, jnp.exp2((s - m_blk).astype(BF16)))
        m_run, acc = state
        m_new = jnp.maximum(m_run, m_blk)
        alpha = jnp.exp2(m_run - m_new)
        return m_new, alpha * acc + _dot(vt, jnp.exp2((s - m_new).astype(BF16)))

    lo_state = hi_state = None
    ahead = [scores(*pairs[n]) for n in range(min(SCORE_LOOKAHEAD, len(pairs)))]
    for n, (qi, kj) in enumerate(pairs):
        s_lo, s_hi = ahead.pop(0)
        if n + SCORE_LOOKAHEAD < len(pairs):
            ahead.append(scores(*pairs[n + SCORE_LOOKAHEAD]))
        vt_lo = vt_ref[kj, :, :half] if kj == qi else vt_ref[kj]
        lo_state = softmax_step(lo_state, s_lo, vt_lo)
        hi_state = softmax_step(hi_state, s_hi, vt_ref[kj])
        if kj == qi:
            acc = jnp.concatenate([lo_state[1], hi_state[1]], axis=1)
            out = (acc[:V_HEAD_DIM] / acc[V_HEAD_DIM:V_HEAD_DIM + 1]).T
            gate = gate_ref[qi * blk:(qi + 1) * blk, :].astype(F32)
            o_ref[qi * blk:(qi + 1) * blk, :] = (out * gate).astype(o_ref.dtype)
            lo_state = hi_state = None
        lo = n * len(cast_pieces) // len(pairs)
        hi = (n + 1) * len(cast_pieces) // len(pairs)
        for src, dst, r in cast_pieces[lo:hi]:
            dst[r:r + CAST_PIECE_ROWS, :] = src[r:r + CAST_PIECE_ROWS, :].astype(dst.dtype)


def _attention(q, k_nope, k_pe, v_t, gate, batch, seq, blk, casts):
    m = q.shape[0]
    nblk = seq // blk
    steps = batch * N_HEADS
    cast_specs, cast_shapes = [], []
    for w, n_rows in casts:
        rows = n_rows // steps
        assert rows * steps == n_rows and rows % CAST_PIECE_ROWS == 0
        cast_specs.append(pl.BlockSpec((rows, w.shape[1]), lambda b, h: (b * N_HEADS + h, 0)))
        cast_shapes.append(jax.ShapeDtypeStruct((n_rows, w.shape[1]), BF16))
    return pl.pallas_call(
        functools.partial(_attention_kernel, blk=blk, n_casts=len(casts)),
        grid=(batch, N_HEADS),
        in_specs=[pl.BlockSpec((seq, QK_PAD_DIM), lambda b, h: (b, h)),
                  pl.BlockSpec((seq, LANES), lambda b, h: (b, h)),
                  pl.BlockSpec((seq, LANES), lambda b, h: (b, 0)),
                  pl.BlockSpec((nblk, V_AUG_DIM, blk), lambda b, h: (b, h, 0)),
                  pl.BlockSpec((seq, LANES), lambda b, h: (b, h))] + cast_specs,
        out_specs=[pl.BlockSpec((seq, V_HEAD_DIM), lambda b, h: (b, h))] + cast_specs,
        out_shape=[jax.ShapeDtypeStruct((m, N_HEADS * V_HEAD_DIM), BF16)] + cast_shapes,
        scratch_shapes=[pltpu.VMEM((seq, QK_PAD_DIM), BF16)],
        compiler_params=_params("parallel", "parallel"),
        name="attention",
    )(q, k_nope, k_pe, v_t, gate, *[w for w, _ in casts])


def _out_proj_kernel(yc_ref, ya_ref, w_ref, x_ref, g_ref, o_ref, *, final_norm, tn):
    kc = yc_ref.shape[1]
    d = o_ref.shape[1]
    yc = yc_ref[...]
    ya = ya_ref[...]
    ss = None
    for t in range(d // tn):
        cols = slice(t * tn, (t + 1) * tn)
        h = x_ref[:, cols] + (_dot(yc, w_ref[:kc, cols]) + _dot(ya, w_ref[kc:, cols]))
        o_ref[:, cols] = h.astype(o_ref.dtype)
        if final_norm:
            part = jnp.sum(h * h, axis=-1, keepdims=True)
            ss = part if ss is None else ss + part
    if final_norm:
        inv = lax.rsqrt(ss / d + NORM_EPS)
        for t in range(d // tn):
            cols = slice(t * tn, (t + 1) * tn)
            o_ref[:, cols] = (o_ref[:, cols] * inv) * g_ref[:, cols]


def _out_proj(y_conv, y_attn, w_o, x, g, final_norm, tm, tn):
    m, d = x.shape
    kc = y_conv.shape[1]
    ka = y_attn.shape[1]
    return pl.pallas_call(
        functools.partial(_out_proj_kernel, final_norm=final_norm, tn=tn),
        grid=(m // tm,),
        in_specs=[pl.BlockSpec((tm, kc), lambda i: (i, 0)),
                  pl.BlockSpec((tm, ka), lambda i: (i, 0)),
                  pl.BlockSpec((kc + ka, d), lambda i: (0, 0), pipeline_mode=pl.Buffered(1)),
                  pl.BlockSpec((tm, d), lambda i: (i, 0)),
                  pl.BlockSpec((1, d), lambda i: (0, 0))],
        out_specs=pl.BlockSpec((tm, d), lambda i: (i, 0)),
        out_shape=jax.ShapeDtypeStruct((m, d), x.dtype),
        compiler_params=pltpu.CompilerParams(dimension_semantics=("parallel",),
                                             vmem_limit_bytes=BIG_VMEM_LIMIT_BYTES),
        name="out_proj",
    )(y_conv, y_attn, w_o, x, g.reshape(1, d))


def _rope_tables(seq):
    pos = jnp.arange(seq, dtype=F32)
    inv_freq = 1.0 / (ROPE_THETA ** (jnp.arange(0, ROPE_DIM, 2, dtype=F32) / ROPE_DIM))
    ang = pos[:, None] * inv_freq[None, :]
    zeros = jnp.zeros((seq, LANES - ROPE_DIM), F32)
    cos = jnp.concatenate([jnp.cos(ang), jnp.cos(ang), zeros], axis=-1)
    sin = jnp.concatenate([jnp.sin(ang), jnp.sin(ang), zeros], axis=-1)
    return cos, sin


def _tile(n, pref):
    return pref if n % pref == 0 else n


def _layer(h, g_in, w_in, conv_w, q_norm_g, w_uq, kv_norm_g, w_ukv, w_out, g_out,
           final_norm, cos, sin, batch, seq):
    m, d = h.shape
    dc = conv_w.shape[1]
    o = 4 * dc

    w_in_t = w_in.T
    kr0 = o + Q_LORA + KV_LORA
    n_lat = Q_LORA + KV_LORA + LANES
    w_qn, w_qr, w_lat_t = _q_weights(w_uq, w_in_t, o, n_lat)
    w_kv = w_ukv.astype(BF16)

    blk = _tile(seq, ATTN_BLOCK)
    xn, cq, ckv, k_pe, w_z_t = _latents(h, g_in, w_lat_t, q_norm_g, kv_norm_g, cos, sin,
                                        w_in_t, kr0 + ROPE_DIM, w_in_t.shape[0] - kr0 - ROPE_DIM,
                                        seq, _tile(seq, 512))
    gate = _gate(xn, w_z_t, _tile(m, 1024), _tile(w_z_t.shape[0], 1024))
    q, k_nope, v_t = _qkv_up(cq, ckv, w_qn, w_qr, w_kv, cos, sin, seq, _tile(seq, 512), blk)
    y_attn, w_conv_t, w_o = _attention(q, k_nope, k_pe, v_t, gate, batch, seq, blk,
                                       [(w_in_t, o), (w_out, w_out.shape[0])])
    y_conv = _conv_group(xn, w_conv_t, conv_w, seq, _tile(seq, 2048), _tile(dc, 256))
    return _out_proj(y_conv, y_attn, w_o, h, g_out, final_norm, _tile(m, 256), _tile(d, 512))


def kernel(x, g_in, w_in, conv_w, q_norm_g, w_uq, kv_norm_g, w_ukv, w_out, g_final):
    batch, seq, d = x.shape
    depth = g_in.shape[0]
    cos, sin = _rope_tables(seq)
    h = x.reshape(batch * seq, d)
    for l in range(depth):
        h = _layer(h, g_in[l], w_in[l], conv_w[l], q_norm_g[l], w_uq[l], kv_norm_g[l],
                   w_ukv[l], w_out[l], g_final, l == depth - 1, cos, sin, batch, seq)
    return h.reshape(batch, seq, d)
```

```python
import functools

import jax
import jax.numpy as jnp
import numpy as np
from jax import lax
from jax.experimental import pallas as pl
from jax.experimental.pallas import tpu as pltpu

CHUNK = 64
CONV_WIDTH = 3
N_HEADS = 16
V_HEAD_DIM = 128
QK_NOPE_DIM = 128
ROPE_DIM = 64
ROPE_HALF = ROPE_DIM // 2
Q_LORA = 1024
KV_LORA = 512
ROPE_THETA = 10000.0
NORM_EPS = 1e-6
ATTN_SCALE = (QK_NOPE_DIM + ROPE_DIM) ** -0.5
LOG2_E = float(np.log2(np.e))
Q_SCALE = ATTN_SCALE * LOG2_E

LANES = 128
SUBLANES = 8
QK_PAD_DIM = 2 * LANES
V_AUG_DIM = V_HEAD_DIM + 16
ATTN_BLOCK = 512
SCORE_LOOKAHEAD = 2
CONV_PIECE_ROWS = 512
ROW_PARTS = 2
CAST_PIECE_ROWS = 16
VMEM_LIMIT_BYTES = 56 * 1024 * 1024
BIG_VMEM_LIMIT_BYTES = 60 * 1024 * 1024

BF16 = jnp.bfloat16
F32 = jnp.float32


def _params(*semantics):
    return pltpu.CompilerParams(dimension_semantics=semantics,
                                vmem_limit_bytes=VMEM_LIMIT_BYTES)


def _dot(a, b):
    return jnp.dot(a, b, preferred_element_type=F32)


def _dot_nt(a, b):
    return lax.dot_general(a, b, (((1,), (1,)), ((), ())), preferred_element_type=F32)


def _rmsnorm_rows(x, g):
    ms = jnp.mean(x * x, axis=-1, keepdims=True)
    return (x * lax.rsqrt(ms + NORM_EPS)) * g


def _silu(z):
    return z * (0.5 * (jnp.tanh(0.5 * z) + 1.0))


def _conv_group_kernel(xn_ref, wb_ref, wc_ref, wh_ref, wz_ref, cw_ref, y_ref, carry_ref,
                       *, tiles_per_seq):
    i = pl.program_id(0)
    j = pl.program_id(1)
    tc = y_ref.shape[1]
    part = CONV_PIECE_ROWS
    n_parts = xn_ref.shape[0] // part

    @pl.when(i % tiles_per_seq == 0)
    def _():
        carry_ref[j] = jnp.zeros((SUBLANES, tc), F32)

    prev = carry_ref[j]
    row = lax.broadcasted_iota(jnp.int32, (SUBLANES, tc), 0)
    cw = cw_ref[...]
    for p in range(n_parts):
        rows = slice(p * part, (p + 1) * part)
        xn = xn_ref[rows, :]
        u = _dot_nt(xn, wc_ref[...]) * _dot_nt(xn, wh_ref[...])

        def shifted(k):
            body = pltpu.roll(u, k, 0)
            head = jnp.where(row < k, pltpu.roll(prev, k, 0), body[:SUBLANES, :])
            return jnp.concatenate([head, body[SUBLANES:, :]], axis=0)

        conv = cw[0:1, :] * shifted(2) + cw[1:2, :] * shifted(1) + cw[2:3, :] * u
        y = (_dot_nt(xn, wb_ref[...]) * conv) * _silu(_dot_nt(xn, wz_ref[...]))
        y_ref[rows, :] = y.astype(y_ref.dtype)
        prev = u[part - SUBLANES:, :]
    carry_ref[j] = prev


def _conv_group(xn, w_t, conv_w, seq, tm, tc):
    m, d = xn.shape
    dc = conv_w.shape[1]
    nj = dc // tc

    def w_spec(k):
        return pl.BlockSpec((tc, d), lambda i, j: (j + k * nj, 0))

    return pl.pallas_call(
        functools.partial(_conv_group_kernel, tiles_per_seq=seq // tm),
        grid=(m // tm, nj),
        in_specs=[pl.BlockSpec((tm, d), lambda i, j: (i, 0)),
                  w_spec(0), w_spec(1), w_spec(2), w_spec(3),
                  pl.BlockSpec((CONV_WIDTH, tc), lambda i, j: (0, j))],
        out_specs=pl.BlockSpec((tm, tc), lambda i, j: (i, j)),
        out_shape=jax.ShapeDtypeStruct((m, dc), BF16),
        scratch_shapes=[pltpu.VMEM((nj, SUBLANES, tc), F32)],
        compiler_params=pltpu.CompilerParams(dimension_semantics=("arbitrary", "arbitrary"),
                                             vmem_limit_bytes=BIG_VMEM_LIMIT_BYTES),
        name="conv_group",
    )(xn, w_t, w_t, w_t, w_t, conv_w)


def _rope_low_half(t, cos, sin):
    lane = lax.broadcasted_iota(jnp.int32, t.shape, 1)
    partner = jnp.where(lane < ROPE_HALF,
                        -pltpu.roll(t, LANES - ROPE_HALF, 1),
                        pltpu.roll(t, ROPE_HALF, 1))
    return jnp.where(lane < ROPE_DIM, t * cos + partner * sin, 0.0)


def _latent_kernel(x_ref, g_ref, w_ref, gq_ref, gkv_ref, cos_ref, sin_ref, cast_in_ref,
                   xn_ref, cq_ref, ckv_ref, kpe_ref, cast_out_ref):
    cast_out_ref[...] = cast_in_ref[...].astype(cast_out_ref.dtype)
    g = g_ref[...]
    part = x_ref.shape[0] // ROW_PARTS
    for p in range(ROW_PARTS):
        rows = slice(p * part, (p + 1) * part)
        c = _dot_nt((x_ref[rows, :] * g).astype(BF16), w_ref[...])
        x = x_ref[rows, :]
        inv = lax.rsqrt(jnp.mean(x * x, axis=-1, keepdims=True) + NORM_EPS)
        xn_ref[rows, :] = ((x * inv) * g).astype(xn_ref.dtype)
        c = c * inv
        cq_ref[rows, :] = _rmsnorm_rows(c[:, :Q_LORA], gq_ref[...]).astype(cq_ref.dtype)
        ckv_ref[rows, :] = _rmsnorm_rows(c[:, Q_LORA:Q_LORA + KV_LORA],
                                         gkv_ref[...]).astype(ckv_ref.dtype)
        kpe_ref[rows, :] = _rope_low_half(c[:, Q_LORA + KV_LORA:], cos_ref[rows, :],
                                          sin_ref[rows, :]).astype(kpe_ref.dtype)


def _latents(x, g_in, w_t, gq, gkv, cos, sin, w_cast, cast_row0, cast_rows, seq, tm):
    m, d = x.shape
    tps = seq // tm
    n = w_t.shape[0]
    assert n == Q_LORA + KV_LORA + LANES
    cr = cast_rows // (m // tm)
    assert cr * (m // tm) == cast_rows and cr % CAST_PIECE_ROWS == 0 and cast_row0 % SUBLANES == 0
    return pl.pallas_call(
        _latent_kernel,
        grid=(m // tm,),
        in_specs=[pl.BlockSpec((tm, d), lambda i: (i, 0)),
                  pl.BlockSpec((1, d), lambda i: (0, 0)),
                  pl.BlockSpec((n, d), lambda i: (0, 0), pipeline_mode=pl.Buffered(1)),
                  pl.BlockSpec((1, Q_LORA), lambda i: (0, 0)),
                  pl.BlockSpec((1, KV_LORA), lambda i: (0, 0)),
                  pl.BlockSpec((tm, LANES), lambda i: (i % tps, 0)),
                  pl.BlockSpec((tm, LANES), lambda i: (i % tps, 0)),
                  pl.BlockSpec((pl.Element(cr), pl.Element(w_cast.shape[1])),
                               lambda i: (pl.multiple_of(cast_row0 + i * cr, SUBLANES), 0))],
        out_specs=[pl.BlockSpec((tm, d), lambda i: (i, 0)),
                   pl.BlockSpec((tm, Q_LORA), lambda i: (i, 0)),
                   pl.BlockSpec((tm, KV_LORA), lambda i: (i, 0)),
                   pl.BlockSpec((tm, LANES), lambda i: (i, 0)),
                   pl.BlockSpec((cr, w_cast.shape[1]), lambda i: (i, 0))],
        out_shape=[jax.ShapeDtypeStruct((m, d), BF16),
                   jax.ShapeDtypeStruct((m, Q_LORA), BF16),
                   jax.ShapeDtypeStruct((m, KV_LORA), BF16),
                   jax.ShapeDtypeStruct((m, LANES), BF16),
                   jax.ShapeDtypeStruct((cast_rows, w_cast.shape[1]), BF16)],
        compiler_params=pltpu.CompilerParams(dimension_semantics=("parallel",),
                                             vmem_limit_bytes=BIG_VMEM_LIMIT_BYTES),
        name="latents",
    )(x, g_in.reshape(1, d), w_t, gq.reshape(1, -1), gkv.reshape(1, -1), cos, sin, w_cast)


def _gate_kernel(xn_ref, w_ref, o_ref):
    o_ref[...] = _silu(_dot_nt(xn_ref[...], w_ref[...])).astype(o_ref.dtype)


def _gate(xn, w_z_t, tm, tn):
    m, d = xn.shape
    n = w_z_t.shape[0]
    return pl.pallas_call(
        _gate_kernel,
        grid=(m // tm, n // tn),
        in_specs=[pl.BlockSpec((tm, d), lambda i, j: (i, 0)),
                  pl.BlockSpec((tn, d), lambda i, j: (j, 0))],
        out_specs=pl.BlockSpec((tm, tn), lambda i, j: (i, j)),
        out_shape=jax.ShapeDtypeStruct((m, n), BF16),
        compiler_params=_params("parallel", "parallel"),
        name="attn_gate",
    )(xn, w_z_t)


def _q_weight_kernel(w_ref, cast_in_ref, freq_ref, nope_ref, rope_ref, cast_out_ref,
                     cos_ref, sin_ref):
    cast_out_ref[...] = cast_in_ref[...].astype(cast_out_ref.dtype)
    rows = cos_ref.shape[0]
    pos = (pl.program_id(0) * rows
           + lax.broadcasted_iota(jnp.int32, (rows, LANES), 0)).astype(F32)
    ang = pos * freq_ref[...]
    table_lane = lax.broadcasted_iota(jnp.int32, (rows, LANES), 1)
    cos_ref[...] = jnp.where(table_lane < ROPE_DIM, jnp.cos(ang), 0.0)
    sin_ref[...] = jnp.sin(ang)
    lane = lax.broadcasted_iota(jnp.int32, (w_ref.shape[0], LANES), 1)
    t0 = w_ref[:, :LANES]
    t1 = w_ref[:, LANES:2 * LANES]
    t2 = w_ref[:, 2 * LANES:]
    nope_ref[:, :LANES] = t0.astype(nope_ref.dtype)
    nope_ref[:, LANES:] = jnp.where(lane < ROPE_DIM, pltpu.roll(t1, ROPE_DIM, 1),
                                    pltpu.roll(t2, ROPE_DIM, 1)).astype(nope_ref.dtype)
    rope_ref[...] = jnp.where(lane < ROPE_DIM, t1, t2).astype(rope_ref.dtype)


def _q_weights(w_uq, w_cast, cast_row0, cast_rows, seq):
    k = w_uq.shape[0]
    pair_in = 2 * (QK_NOPE_DIM + ROPE_DIM)
    steps = N_HEADS // 2
    cr = cast_rows // steps
    assert cr * steps == cast_rows and cr % CAST_PIECE_ROWS == 0 and cast_row0 % SUBLANES == 0
    tr = seq // steps
    assert tr * steps == seq and tr % SUBLANES == 0
    inv_freq = 1.0 / (ROPE_THETA ** (jnp.arange(0, ROPE_DIM, 2, dtype=F32) / ROPE_DIM))
    freq = jnp.concatenate([inv_freq, inv_freq, jnp.zeros((LANES - ROPE_DIM,), F32)])
    return pl.pallas_call(
        _q_weight_kernel,
        grid=(steps,),
        in_specs=[pl.BlockSpec((k, pair_in), lambda j: (0, j)),
                  pl.BlockSpec((pl.Element(cr), pl.Element(w_cast.shape[1])),
                               lambda j: (pl.multiple_of(cast_row0 + j * cr, SUBLANES), 0)),
                  pl.BlockSpec((1, LANES), lambda j: (0, 0))],
        out_specs=[pl.BlockSpec((k, 2 * QK_NOPE_DIM), lambda j: (0, j)),
                   pl.BlockSpec((k, 2 * ROPE_DIM), lambda j: (0, j)),
                   pl.BlockSpec((cr, w_cast.shape[1]), lambda j: (j, 0)),
                   pl.BlockSpec((tr, LANES), lambda j: (j, 0)),
                   pl.BlockSpec((tr, LANES), lambda j: (j, 0))],
        out_shape=[jax.ShapeDtypeStruct((k, N_HEADS * QK_NOPE_DIM), BF16),
                   jax.ShapeDtypeStruct((k, N_HEADS * ROPE_DIM), BF16),
                   jax.ShapeDtypeStruct((cast_rows, w_cast.shape[1]), BF16),
                   jax.ShapeDtypeStruct((seq, LANES), F32),
                   jax.ShapeDtypeStruct((seq, LANES), F32)],
        compiler_params=_params("parallel"),
        name="q_weights",
    )(w_uq, w_cast, freq.reshape(1, LANES))


def _qkv_up_kernel(cq_ref, ckv_ref, wqn_ref, wqr_ref, wkv_ref, cos_ref, sin_ref,
                   q_ref, k_ref, vt_ref, *, blk):
    cq = cq_ref[...]
    nope = _dot(cq, wqn_ref[...])
    rope = _dot(cq, wqr_ref[...])
    kv = _dot(ckv_ref[...], wkv_ref[...])
    cos = cos_ref[...]
    sin = sin_ref[...]
    for h in range(N_HEADS):
        lo = h * QK_PAD_DIM
        pair = rope[:, (h // 2) * LANES:(h // 2 + 1) * LANES]
        if h % 2:
            pair = pltpu.roll(pair, ROPE_DIM, 1)
        q_ref[:, lo:lo + LANES] = (
            nope[:, h * QK_NOPE_DIM:(h + 1) * QK_NOPE_DIM] * Q_SCALE).astype(q_ref.dtype)
        q_ref[:, lo + LANES:lo + QK_PAD_DIM] = (
            _rope_low_half(pair, cos, sin) * Q_SCALE).astype(q_ref.dtype)
    ones = jnp.ones((V_AUG_DIM - V_HEAD_DIM, blk), vt_ref.dtype)
    for h in range(N_HEADS):
        lo = h * (QK_NOPE_DIM + V_HEAD_DIM)
        k_ref[:, h * QK_NOPE_DIM:(h + 1) * QK_NOPE_DIM] = (
            kv[:, lo:lo + QK_NOPE_DIM].astype(k_ref.dtype))
        v = kv[:, lo + QK_NOPE_DIM:lo + QK_NOPE_DIM + V_HEAD_DIM]
        for t in range(v.shape[0] // blk):
            vt_ref[t, h * V_AUG_DIM:h * V_AUG_DIM + V_HEAD_DIM, :] = (
                v[t * blk:(t + 1) * blk, :].T.astype(vt_ref.dtype))
            vt_ref[t, h * V_AUG_DIM + V_HEAD_DIM:(h + 1) * V_AUG_DIM, :] = ones


def _qkv_up(cq, ckv, w_qn, w_qr, w_kv, cos, sin, seq, tm, blk):
    m = cq.shape[0]
    tps = seq // tm
    per = tm // blk
    nq = N_HEADS * QK_PAD_DIM
    nk = N_HEADS * QK_NOPE_DIM
    nv = N_HEADS * V_AUG_DIM
    return pl.pallas_call(
        functools.partial(_qkv_up_kernel, blk=blk),
        grid=(m // tm,),
        in_specs=[pl.BlockSpec((tm, cq.shape[1]), lambda i: (i, 0)),
                  pl.BlockSpec((tm, ckv.shape[1]), lambda i: (i, 0)),
                  pl.BlockSpec(w_qn.shape, lambda i: (0, 0)),
                  pl.BlockSpec(w_qr.shape, lambda i: (0, 0)),
                  pl.BlockSpec(w_kv.shape, lambda i: (0, 0)),
                  pl.BlockSpec((tm, LANES), lambda i: (i % tps, 0)),
                  pl.BlockSpec((tm, LANES), lambda i: (i % tps, 0))],
        out_specs=[pl.BlockSpec((tm, nq), lambda i: (i, 0)),
                   pl.BlockSpec((tm, nk), lambda i: (i, 0)),
                   pl.BlockSpec((per, nv, blk), lambda i: (i, 0, 0))],
        out_shape=[jax.ShapeDtypeStruct((m, nq), BF16),
                   jax.ShapeDtypeStruct((m, nk), BF16),
                   jax.ShapeDtypeStruct((m // blk, nv, blk), BF16)],
        compiler_params=_params("parallel"),
        name="qkv_up",
    )(cq, ckv, w_qn, w_qr, w_kv, cos, sin)


def _attention_kernel(q_ref, kn_ref, kpe_ref, vt_ref, gate_ref, *rest, blk, n_casts):
    cast_in = rest[:n_casts]
    o_ref = rest[n_casts]
    cast_out = rest[n_casts + 1:2 * n_casts + 1]
    kfull_ref = rest[2 * n_casts + 1]
    cast_pieces = [(src, dst, r) for src, dst in zip(cast_in, cast_out)
                   for r in range(0, src.shape[0], CAST_PIECE_ROWS)]
    seq = q_ref.shape[0]
    nblk = seq // blk
    kfull_ref[:, :LANES] = kn_ref[...]
    kfull_ref[:, LANES:] = kpe_ref[...]

    key_chunk = lax.broadcasted_iota(jnp.int32, (blk, blk), 0) // CHUNK
    qry_chunk = lax.broadcasted_iota(jnp.int32, (blk, blk), 1) // CHUNK
    visible = key_chunk <= qry_chunk

    pairs = [(qi, kj) for qi in range(nblk) for kj in range(qi + 1)]
    half = blk // 2

    def scores(qi, kj):
        k0, q0 = kj * blk, qi * blk
        q_lo = q_ref[q0:q0 + half, :]
        q_hi = q_ref[q0 + half:q0 + blk, :]
        if kj != qi:
            kb = kfull_ref[k0:k0 + blk, :]
            return _dot_nt(kb, q_lo), _dot_nt(kb, q_hi)
        s_lo = _dot_nt(kfull_ref[k0:k0 + half, :], q_lo)
        s_hi = _dot_nt(kfull_ref[k0:k0 + blk, :], q_hi)
        return (jnp.where(visible[:half, :half], s_lo, -jnp.inf),
                jnp.where(visible[:, half:], s_hi, -jnp.inf))

    def softmax_step(state, s, vt):
        m_blk = jnp.max(s, axis=0, keepdims=True)
        if state is None:
            return m_blk, _dot(vt, jnp.exp2((s - m_blk).astype(BF16)))
        m_run, acc = state
        m_new = jnp.maximum(m_run, m_blk)
        alpha = jnp.exp2(m_run - m_new)
        return m_new, alpha * acc + _dot(vt, jnp.exp2((s - m_new).astype(BF16)))

    lo_state = hi_state = None
    ahead = [scores(*pairs[n]) for n in range(min(SCORE_LOOKAHEAD, len(pairs)))]
    for n, (qi, kj) in enumerate(pairs):
        s_lo, s_hi = ahead.pop(0)
        if n + SCORE_LOOKAHEAD < len(pairs):
            ahead.append(scores(*pairs[n + SCORE_LOOKAHEAD]))
        vt_lo = vt_ref[kj, :, :half] if kj == qi else vt_ref[kj]
        lo_state = softmax_step(lo_state, s_lo, vt_lo)
        hi_state = softmax_step(hi_state, s_hi, vt_ref[kj])
        if kj == qi:
            acc = jnp.concatenate([lo_state[1], hi_state[1]], axis=1)
            out = (acc[:V_HEAD_DIM] / acc[V_HEAD_DIM:V_HEAD_DIM + 1]).T
            gate = gate_ref[qi * blk:(qi + 1) * blk, :].astype(F32)
            o_ref[qi * blk:(qi + 1) * blk, :] = (out * gate).astype(o_ref.dtype)
            lo_state = hi_state = None
        lo = n * len(cast_pieces) // len(pairs)
        hi = (n + 1) * len(cast_pieces) // len(pairs)
        for src, dst, r in cast_pieces[lo:hi]:
            dst[r:r + CAST_PIECE_ROWS, :] = src[r:r + CAST_PIECE_ROWS, :].astype(dst.dtype)


def _attention(q, k_nope, k_pe, v_t, gate, batch, seq, blk, casts):
    m = q.shape[0]
    nblk = seq // blk
    steps = batch * N_HEADS
    cast_specs, cast_shapes = [], []
    for w, n_rows in casts:
        rows = n_rows // steps
        assert rows * steps == n_rows and rows % CAST_PIECE_ROWS == 0
        cast_specs.append(pl.BlockSpec((rows, w.shape[1]), lambda b, h: (b * N_HEADS + h, 0)))
        cast_shapes.append(jax.ShapeDtypeStruct((n_rows, w.shape[1]), BF16))
    return pl.pallas_call(
        functools.partial(_attention_kernel, blk=blk, n_casts=len(casts)),
        grid=(batch, N_HEADS),
        in_specs=[pl.BlockSpec((seq, QK_PAD_DIM), lambda b, h: (b, h)),
                  pl.BlockSpec((seq, LANES), lambda b, h: (b, h)),
                  pl.BlockSpec((seq, LANES), lambda b, h: (b, 0)),
                  pl.BlockSpec((nblk, V_AUG_DIM, blk), lambda b, h: (b, h, 0)),
                  pl.BlockSpec((seq, LANES), lambda b, h: (b, h))] + cast_specs,
        out_specs=[pl.BlockSpec((seq, V_HEAD_DIM), lambda b, h: (b, h))] + cast_specs,
        out_shape=[jax.ShapeDtypeStruct((m, N_HEADS * V_HEAD_DIM), BF16)] + cast_shapes,
        scratch_shapes=[pltpu.VMEM((seq, QK_PAD_DIM), BF16)],
        compiler_params=_params("parallel", "parallel"),
        name="attention",
    )(q, k_nope, k_pe, v_t, gate, *[w for w, _ in casts])


def _out_proj_kernel(yc_ref, ya_ref, w_ref, x_ref, g_ref, o_ref, *, final_norm, tn):
    kc = yc_ref.shape[1]
    d = o_ref.shape[1]
    yc = yc_ref[...]
    ya = ya_ref[...]
    ss = None
    for t in range(d // tn):
        cols = slice(t * tn, (t + 1) * tn)
        h = x_ref[:, cols] + (_dot(yc, w_ref[:kc, cols]) + _dot(ya, w_ref[kc:, cols]))
        o_ref[:, cols] = h.astype(o_ref.dtype)
        if final_norm:
            part = jnp.sum(h * h, axis=-1, keepdims=True)
            ss = part if ss is None else ss + part
    if final_norm:
        inv = lax.rsqrt(ss / d + NORM_EPS)
        for t in range(d // tn):
            cols = slice(t * tn, (t + 1) * tn)
            o_ref[:, cols] = (o_ref[:, cols] * inv) * g_ref[:, cols]


def _out_proj(y_conv, y_attn, w_o, x, g, final_norm, tm, tn):
    m, d = x.shape
    kc = y_conv.shape[1]
    ka = y_attn.shape[1]
    return pl.pallas_call(
        functools.partial(_out_proj_kernel, final_norm=final_norm, tn=tn),
        grid=(m // tm,),
        in_specs=[pl.BlockSpec((tm, kc), lambda i: (i, 0)),
                  pl.BlockSpec((tm, ka), lambda i: (i, 0)),
                  pl.BlockSpec((kc + ka, d), lambda i: (0, 0), pipeline_mode=pl.Buffered(1)),
                  pl.BlockSpec((tm, d), lambda i: (i, 0)),
                  pl.BlockSpec((1, d), lambda i: (0, 0))],
        out_specs=pl.BlockSpec((tm, d), lambda i: (i, 0)),
        out_shape=jax.ShapeDtypeStruct((m, d), x.dtype),
        compiler_params=pltpu.CompilerParams(dimension_semantics=("parallel",),
                                             vmem_limit_bytes=BIG_VMEM_LIMIT_BYTES),
        name="out_proj",
    )(y_conv, y_attn, w_o, x, g.reshape(1, d))


def _tile(n, pref):
    return pref if n % pref == 0 else n


def _layer(h, g_in, w_in, conv_w, q_norm_g, w_uq, kv_norm_g, w_ukv, w_out, g_out,
           final_norm, batch, seq):
    m, d = h.shape
    dc = conv_w.shape[1]
    o = 4 * dc

    w_in_t = w_in.T
    kr0 = o + Q_LORA + KV_LORA
    n_lat = Q_LORA + KV_LORA + LANES
    w_qn, w_qr, w_lat_t, cos, sin = _q_weights(w_uq, w_in_t, o, n_lat, seq)
    w_kv = w_ukv.astype(BF16)

    blk = _tile(seq, ATTN_BLOCK)
    xn, cq, ckv, k_pe, w_z_t = _latents(h, g_in, w_lat_t, q_norm_g, kv_norm_g, cos, sin,
                                        w_in_t, kr0 + ROPE_DIM, w_in_t.shape[0] - kr0 - ROPE_DIM,
                                        seq, _tile(seq, 512))
    gate = _gate(xn, w_z_t, _tile(m, 1024), _tile(w_z_t.shape[0], 1024))
    q, k_nope, v_t = _qkv_up(cq, ckv, w_qn, w_qr, w_kv, cos, sin, seq, _tile(seq, 512), blk)
    y_attn, w_conv_t, w_o = _attention(q, k_nope, k_pe, v_t, gate, batch, seq, blk,
                                       [(w_in_t, o), (w_out, w_out.shape[0])])
    y_conv = _conv_group(xn, w_conv_t, conv_w, seq, _tile(seq, 2048), _tile(dc, 256))
    return _out_proj(y_conv, y_attn, w_o, h, g_out, final_norm, _tile(m, 256), _tile(d, 512))


def kernel(x, g_in, w_in, conv_w, q_norm_g, w_uq, kv_norm_g, w_ukv, w_out, g_final):
    batch, seq, d = x.shape
    depth = g_in.shape[0]
    h = x.reshape(batch * seq, d)
    for l in range(depth):
        h = _layer(h, g_in[l], w_in[l], conv_w[l], q_norm_g[l], w_uq[l], kv_norm_g[l],
                   w_ukv[l], w_out[l], g_final, l == depth - 1, batch, seq)
    return h.reshape(batch, seq, d)
```
